```python
import math
import jax
import jax.numpy as jnp
from jax import lax
import numpy as np

D_MODEL = 1024
BATCH = 4
SEQ = 4096
DEPTH = 2
DEC_BATCH = 32
DEC_SEQ = 1
PAST_LEN = 16384
PAGE_SIZE = 128

HEAD_DIM = 64
FOX_HEADS = 4
NSA_HEADS = 8
NSA_KV_GROUPS = 2
NSA_HPG = NSA_HEADS // NSA_KV_GROUPS
DIFF_HEADS = 4
DIFF_QK_DIM = HEAD_DIM // 2
NSA_BLOCK = 64
NSA_TOP_N = 16
NSA_WINDOW = 512
Q_BLOCK = 128
ROPE_THETA = 500000.0
ROPE_FRACTION = 4
N_EXPERTS = 32
TOP_K = 4
D_FF = D_MODEL
SWIGLU_LIMIT = 7.0
SWIGLU_ALPHA = 1.702
EXPERT_ROWS = 64
LN_EPS = 1e-5
RMS_EPS = 1e-5
DEEPNORM_ALPHA = (2 * DEPTH) ** 0.25
DEEPNORM_BETA = (8 * DEPTH) ** -0.25
NEG = -1e30
FORCE_SCORE = 1e4
FOX_ROW = 2 * HEAD_DIM + 1
IN_SIZES = (FOX_HEADS * HEAD_DIM, FOX_HEADS * HEAD_DIM, FOX_HEADS * HEAD_DIM, FOX_HEADS,
            NSA_HEADS * HEAD_DIM,
            NSA_KV_GROUPS * HEAD_DIM, NSA_KV_GROUPS * HEAD_DIM, NSA_KV_GROUPS * HEAD_DIM,
            NSA_KV_GROUPS * HEAD_DIM, NSA_KV_GROUPS * HEAD_DIM, NSA_KV_GROUPS * HEAD_DIM,
            3 * NSA_HEADS,
            DIFF_HEADS * HEAD_DIM, DIFF_HEADS * HEAD_DIM, DIFF_HEADS * HEAD_DIM)
N_IN = sum(IN_SIZES)

kernel_name = 'hybrid_fox_nsa_diff_moe_step'

F32 = jnp.float32


def _normal(key, shape, scale):
    return scale * jax.random.normal(key, shape, F32)


def _layer_norm(x, g, b):
    xf = x.astype(F32)
    mu = jnp.mean(xf, -1, keepdims=True)
    var = jnp.mean(jnp.square(xf - mu), -1, keepdims=True)
    return ((xf - mu) * lax.rsqrt(var + LN_EPS) * g.astype(F32) + b.astype(F32)).astype(x.dtype)


def _rotary(x, pos):
    rd = x.shape[-1] // ROPE_FRACTION
    half = rd // 2
    inv_freq = ROPE_THETA ** (-jnp.arange(half, dtype=F32) / half)
    ang = pos.astype(F32)[:, None] * inv_freq[None, :]
    cos = jnp.cos(ang)[:, None, :]
    sin = jnp.sin(ang)[:, None, :]
    xf = x.astype(F32)
    x1, x2 = xf[..., :half], xf[..., half:rd]
    out = jnp.concatenate([x1 * cos - x2 * sin, x2 * cos + x1 * sin, xf[..., rd:]], -1)
    return out.astype(x.dtype)


def _masked_softmax(logits, mask):
    logits = jnp.where(mask, logits, NEG)
    m = jnp.max(logits, -1, keepdims=True)
    p = jnp.exp(logits - m) * mask
    return p / jnp.maximum(jnp.sum(p, -1, keepdims=True), 1e-30)


def _merge_blocks(out):
    nb, b, qb, f = out.shape
    return jnp.moveaxis(out, 0, 1).reshape(b, nb * qb, f)


def _gather_pages(pool, page_table, layer):
    pages = pool[page_table, layer]
    return pages.reshape((page_table.shape[0], page_table.shape[1] * pool.shape[2]) + pool.shape[3:])


def _mixer_inputs(h, pos, w_in_l, b_forget_l, b_gate_l):
    n_b, n_t, _ = h.shape
    splits = np.cumsum(IN_SIZES)[:-1].tolist()
    (fq, fk, fv, ff, nq, nkc, nvc, nks, nvs, nkw, nvw, ng, dq, dk, dv) = jnp.split(h @ w_in_l, splits, axis=-1)

    def heads(t, n):
        return t.reshape(n_b, n_t, n, -1)

    fox_q = heads(fq, FOX_HEADS)
    logf = jax.nn.log_sigmoid((ff + b_forget_l).astype(F32))
    fox_rows = jnp.concatenate([heads(fk, FOX_HEADS), heads(fv, FOX_HEADS), logf[..., None].astype(h.dtype)], -1)
    g = NSA_KV_GROUPS
    nsa_q = _rotary(heads(nq, NSA_HEADS), pos)
    nsa_gate = jax.nn.sigmoid((ng + b_gate_l).astype(F32)).reshape(n_b, n_t, 3, NSA_HEADS)
    nsa_rows = jnp.stack([_rotary(heads(nkc, g), pos), heads(nvc, g),
                          _rotary(heads(nks, g), pos), heads(nvs, g)], axis=2)
    win_rows = jnp.stack([_rotary(heads(nkw, g), pos), heads(nvw, g)], axis=2)
    diff_q = _rotary(heads(dq, 2 * DIFF_HEADS), pos).reshape(n_b, n_t, DIFF_HEADS, 2, DIFF_QK_DIM)
    diff_k = _rotary(heads(dk, 2 * DIFF_HEADS), pos).reshape(n_b, n_t, DIFF_HEADS, HEAD_DIM)
    diff_rows = jnp.stack([diff_k, heads(dv, DIFF_HEADS)], axis=3)
    return fox_q, fox_rows, nsa_q, nsa_gate, nsa_rows, win_rows, diff_q, diff_rows


def _fox_unpack(rows):
    c = jnp.cumsum(rows[..., 2 * HEAD_DIM].astype(F32), axis=1)
    return rows[..., :HEAD_DIM], rows[..., HEAD_DIM:2 * HEAD_DIM], c


def _fox_attend(q, cq, qpos, k, v, c):
    kpos = jnp.arange(k.shape[1])
    logits = jnp.einsum('bqhd,bkhd->bhqk', q, k, preferred_element_type=F32) * (HEAD_DIM ** -0.5)
    logits = logits + jnp.swapaxes(cq, 1, 2)[:, :, :, None] - jnp.swapaxes(c, 1, 2)[:, :, None, :]
    mask = kpos[None, :] <= qpos[:, None]
    p = jax.nn.softmax(jnp.where(mask, logits, NEG), axis=-1)
    o = jnp.einsum('bhqk,bkhd->bqhd', p.astype(v.dtype), v)
    return o.reshape(o.shape[0], o.shape[1], -1)


def _fox_prompt(q, rows):
    k, v, c = _fox_unpack(rows)

    def block(start):
        qb = lax.dynamic_slice_in_dim(q, start, Q_BLOCK, 1)
        cq = lax.dynamic_slice_in_dim(c, start, Q_BLOCK, 1)
        return _fox_attend(qb, cq, start + jnp.arange(Q_BLOCK), k, v, c)

    return _merge_blocks(lax.map(block, jnp.arange(q.shape[1] // Q_BLOCK) * Q_BLOCK))


def _fox_sample(q, rows_all, past):
    k, v, c = _fox_unpack(rows_all)
    return _fox_attend(q, c[:, past:], past + jnp.arange(q.shape[1]), k, v, c)


def _nsa_context(rows, cmp_pos, cmp_w):
    n_b, n_l = rows.shape[0], rows.shape[1]
    n_cmp = n_l // NSA_BLOCK
    n_blk = -(-n_l // NSA_BLOCK)

    def compress(t, pe, w):
        blk = t[:, :n_cmp * NSA_BLOCK].reshape(n_b, n_cmp, NSA_BLOCK, NSA_KV_GROUPS, HEAD_DIM)
        blk = blk + pe[None, None, :, None, :]
        return jnp.einsum('bnigd,ide->bnge', blk, w)

    def blocks(t):
        t = jnp.pad(t, ((0, 0), (0, n_blk * NSA_BLOCK - n_l), (0, 0), (0, 0)))
        return t.reshape(n_b, n_blk, NSA_BLOCK, NSA_KV_GROUPS, HEAD_DIM).transpose(0, 3, 1, 2, 4)

    return (compress(rows[:, :, 0], cmp_pos[0], cmp_w[0]), compress(rows[:, :, 1], cmp_pos[1], cmp_w[1]),
            blocks(rows[:, :, 2]), blocks(rows[:, :, 3]))


def _nsa_attend(q, gate, qpos, k_cmp, v_cmp, k_sel, v_sel, k_win, v_win, wpos):
    n_b, n_q = q.shape[0], q.shape[1]
    scale = HEAD_DIM ** -0.5
    qg = q.reshape(n_b, n_q, NSA_KV_GROUPS, NSA_HPG, HEAD_DIM).transpose(0, 2, 1, 3, 4)
    gate = gate.reshape(n_b, n_q, 3, NSA_KV_GROUPS, NSA_HPG).transpose(2, 0, 3, 1, 4)[..., None]
    n_cmp = k_cmp.shape[1]
    cmp_end = jnp.arange(n_cmp) * NSA_BLOCK + (NSA_BLOCK - 1)
    lc = jnp.einsum('bgqhd,bngd->bgqhn', qg, k_cmp, preferred_element_type=F32) * scale
    pc = _masked_softmax(lc, (cmp_end[None, :] <= qpos[:, None])[None, None, :, None, :])
    oc = jnp.einsum('bgqhn,bngd->bgqhd', pc.astype(v_cmp.dtype), v_cmp)
    n_blk = k_sel.shape[2]
    imp = jnp.pad(jnp.sum(pc, axis=3), ((0, 0), (0, 0), (0, 0), (0, n_blk - n_cmp)))
    jb = jnp.arange(n_blk)[None, :]
    cur = (qpos // NSA_BLOCK)[:, None]
    forced = (jb == 0) | (jb == cur) | (jb == cur - 1)
    score = jnp.where(jb <= cur, jnp.where(forced, FORCE_SCORE, imp), NEG)
    _, idx = lax.top_k(score, min(NSA_TOP_N, n_blk))
    n_sel = idx.shape[-1]
    bi = jnp.arange(n_b)[:, None, None, None]
    gi = jnp.arange(NSA_KV_GROUPS)[None, :, None, None]
    ks = k_sel[bi, gi, idx].reshape(n_b, NSA_KV_GROUPS, n_q, n_sel * NSA_BLOCK, HEAD_DIM)
    vs = v_sel[bi, gi, idx].reshape(n_b, NSA_KV_GROUPS, n_q, n_sel * NSA_BLOCK, HEAD_DIM)
    kpos = (idx[..., None] * NSA_BLOCK + jnp.arange(NSA_BLOCK)).reshape(n_b, NSA_KV_GROUPS, n_q, 1, n_sel * NSA_BLOCK)
    ls = jnp.einsum('bgqhd,bgqkd->bgqhk', qg, ks, preferred_element_type=F32) * scale
    ps = _masked_softmax(ls, kpos <= qpos[None, None, :, None, None])
    os_ = jnp.einsum('bgqhk,bgqkd->bgqhd', ps.astype(vs.dtype), vs)
    lw = jnp.einsum('bgqhd,bkgd->bgqhk', qg, k_win, preferred_element_type=F32) * scale
    wmask = (wpos[None, :] <= qpos[:, None]) & (wpos[None, :] > qpos[:, None] - NSA_WINDOW) & (wpos[None, :] >= 0)
    pw = _masked_softmax(lw, wmask[None, None, :, None, :])
    ow = jnp.einsum('bgqhk,bkgd->bgqhd', pw.astype(v_win.dtype), v_win)
    o = gate[0] * oc + gate[1] * os_ + gate[2] * ow
    return o.transpose(0, 2, 1, 3, 4).reshape(n_b, n_q, NSA_HEADS * HEAD_DIM).astype(q.dtype)


def _nsa_prompt(q, gate, rows, win_rows, cmp_pos, cmp_w):
    k_cmp, v_cmp, k_sel, v_sel = _nsa_context(rows, cmp_pos, cmp_w)
    pad = ((0, 0), (NSA_WINDOW, 0), (0, 0), (0, 0))
    kw_pad = jnp.pad(win_rows[:, :, 0], pad)
    vw_pad = jnp.pad(win_rows[:, :, 1], pad)
    span = NSA_WINDOW + Q_BLOCK

    def block(start):
        qb = lax.dynamic_slice_in_dim(q, start, Q_BLOCK, 1)
        gb = lax.dynamic_slice_in_dim(gate, start, Q_BLOCK, 1)
        kw = lax.dynamic_slice_in_dim(kw_pad, start, span, 1)
        vw = lax.dynamic_slice_in_dim(vw_pad, start, span, 1)
        wpos = start - NSA_WINDOW + jnp.arange(span)
        return _nsa_attend(qb, gb, start + jnp.arange(Q_BLOCK), k_cmp, v_cmp, k_sel, v_sel, kw, vw, wpos)

    return _merge_blocks(lax.map(block, jnp.arange(q.shape[1] // Q_BLOCK) * Q_BLOCK))


def _nsa_sample(q, gate, rows_all, win_all, past, cmp_pos, cmp_w):
    n_new = q.shape[1]
    k_cmp, v_cmp, k_sel, v_sel = _nsa_context(rows_all, cmp_pos, cmp_w)
    n_w = win_all.shape[1]
    wpos = past + n_new - n_w + jnp.arange(n_w)
    return _nsa_attend(q, gate, past + jnp.arange(n_new), k_cmp, v_cmp, k_sel, v_sel,
                       win_all[:, :, 0], win_all[:, :, 1], wpos)


def _diff_lambda(lp, layer):
    lam_init = 0.8 - 0.6 * math.exp(-0.3 * layer)
    lp = lp.astype(F32)
    lam = jnp.exp(jnp.sum(lp[0] * lp[1])) - jnp.exp(jnp.sum(lp[2] * lp[3])) + lam_init
    return lam, lam_init


def _diff_unpack(rows):
    n_b, n_l = rows.shape[0], rows.shape[1]
    return rows[:, :, :, 0].reshape(n_b, n_l, DIFF_HEADS, 2, DIFF_QK_DIM), rows[:, :, :, 1]


def _diff_attend(q, qpos, k, v, lam, lam_init, g):
    kpos = jnp.arange(k.shape[1])
    logits = jnp.einsum('bqhmd,bkhmd->bhmqk', q, k, preferred_element_type=F32) * (DIFF_QK_DIM ** -0.5)
    mask = kpos[None, :] <= qpos[:, None]
    p = jax.nn.softmax(jnp.where(mask, logits, NEG), axis=-1)
    a = p[:, :, 0] - lam * p[:, :, 1]
    o = jnp.einsum('bhqk,bkhd->bqhd', a.astype(v.dtype), v).astype(F32)
    o = o * lax.rsqrt(jnp.mean(o * o, -1, keepdims=True) + RMS_EPS) * g.astype(F32) * (1.0 - lam_init)
    return o.reshape(o.shape[0], o.shape[1], -1).astype(v.dtype)


def _diff_prompt(q, rows, lam, lam_init, g):
    k, v = _diff_unpack(rows)

    def block(start):
        qb = lax.dynamic_slice_in_dim(q, start, Q_BLOCK, 1)
        return _diff_attend(qb, start + jnp.arange(Q_BLOCK), k, v, lam, lam_init, g)

    return _merge_blocks(lax.map(block, jnp.arange(q.shape[1] // Q_BLOCK) * Q_BLOCK))


def _diff_sample(q, rows_all, past, lam, lam_init, g):
    k, v = _diff_unpack(rows_all)
    return _diff_attend(q, past + jnp.arange(q.shape[1]), k, v, lam, lam_init, g)


def _moe(x, w_router, b_router, w_up, b_up, w_down, b_down):
    n_t, d = x.shape
    logits = jnp.dot(x, w_router, preferred_element_type=F32) + b_router.astype(F32)
    top_val, top_idx = lax.top_k(logits, TOP_K)
    gate = jax.nn.softmax(top_val, axis=-1)
    n = n_t * TOP_K
    flat_e = top_idx.reshape(n)
    order = jnp.argsort(flat_e)
    sorted_e = flat_e[order]
    sorted_tok = (order // TOP_K).astype(jnp.int32)
    sorted_gate = gate.reshape(n)[order]
    counts = jnp.bincount(flat_e, length=N_EXPERTS)
    padded = (counts + EXPERT_ROWS - 1) // EXPERT_ROWS * EXPERT_ROWS
    starts = jnp.cumsum(counts) - counts
    pends = jnp.cumsum(padded)
    dest = (pends - padded)[sorted_e] + jnp.arange(n) - starts[sorted_e]
    n_rows = -(-n // EXPERT_ROWS) * EXPERT_ROWS + N_EXPERTS * EXPERT_ROWS
    n_blk = n_rows // EXPERT_ROWS
    row_tok = jnp.full((n_rows,), n_t, jnp.int32).at[dest].set(sorted_tok)
    row_gate = jnp.zeros((n_rows,), F32).at[dest].set(sorted_gate)
    blk_e = jnp.minimum(jnp.searchsorted(pends, jnp.arange(n_blk) * EXPERT_ROWS, side='right'), N_EXPERTS - 1)
    xs = jnp.concatenate([x, jnp.zeros((1, d), x.dtype)], 0)[row_tok].reshape(n_blk, EXPERT_ROWS, d)

    def expert_block(args):
        xb, e = args
        hcat = xb @ w_up[e] + b_up[e]
        glu = jnp.minimum(hcat[:, 0::2], SWIGLU_LIMIT)
        lin = jnp.clip(hcat[:, 1::2], -SWIGLU_LIMIT, SWIGLU_LIMIT)
        act = glu * jax.nn.sigmoid(SWIGLU_ALPHA * glu) * (lin + 1.0)
        return act @ w_down[e] + b_down[e]

    out = lax.map(expert_block, (xs, blk_e)).reshape(n_rows, d)
    y = jax.ops.segment_sum(out.astype(F32) * row_gate[:, None], row_tok, num_segments=n_t + 1)
    return y[:n_t].astype(x.dtype)


def _tail(h, mix, w_out_l, ln1_g, ln1_b, w_router_l, b_router_l, w_up_l, b_up_l, w_down_l, b_down_l, ln2_g, ln2_b):
    h = _layer_norm(DEEPNORM_ALPHA * h + mix @ w_out_l, ln1_g, ln1_b)
    f = _moe(h.reshape(-1, h.shape[-1]), w_router_l, b_router_l, w_up_l, b_up_l, w_down_l, b_down_l)
    return _layer_norm(DEEPNORM_ALPHA * h + f.reshape(h.shape), ln2_g, ln2_b)


def setup_inputs(seed: int = 0) -> dict:
    key = jax.random.key(seed)
    ks = jax.random.split(key, 26)
    n_pages = PAST_LEN // PAGE_SIZE
    n_pool = (5 * DEC_BATCH * n_pages + 3) // 4
    w_buf = min(NSA_WINDOW, PAST_LEN)
    x_prompt = _normal(ks[0], (BATCH, SEQ, D_MODEL), 1.0)
    x_sample = _normal(ks[1], (DEC_BATCH, DEC_SEQ, D_MODEL), 1.0)
    fox = _normal(ks[2], (n_pool, DEPTH, PAGE_SIZE, FOX_HEADS, FOX_ROW), 1.0)
    cache_fox = fox.at[..., -1].set(jax.nn.log_sigmoid(fox[..., -1]))
    cache_nsa = _normal(ks[3], (n_pool, DEPTH, PAGE_SIZE, 4, NSA_KV_GROUPS, HEAD_DIM), 1.0)
    cache_diff = _normal(ks[4], (n_pool, DEPTH, PAGE_SIZE, DIFF_HEADS, 2, HEAD_DIM), 1.0)
    state_nsa_win = _normal(ks[5], (DEC_BATCH, DEPTH, w_buf, 2, NSA_KV_GROUPS, HEAD_DIM), 1.0)
    page_table = jax.random.permutation(ks[6], n_pool)[:DEC_BATCH * n_pages].reshape(DEC_BATCH, n_pages).astype(jnp.int32)
    return {
        'x_prompt': x_prompt,
        'x_sample': x_sample,
        'cache_fox': cache_fox,
        'cache_nsa': cache_nsa,
        'cache_diff': cache_diff,
        'state_nsa_win': state_nsa_win,
        'page_table': page_table,
        'w_in': _normal(ks[7], (DEPTH, D_MODEL, N_IN), D_MODEL ** -0.5),
        'b_forget': _normal(ks[8], (DEPTH, FOX_HEADS), 0.1),
        'b_nsa_gate': _normal(ks[9], (DEPTH, 3 * NSA_HEADS), 0.1),
        'nsa_cmp_pos': _normal(ks[10], (DEPTH, 2, NSA_BLOCK, HEAD_DIM), 0.02),
        'nsa_cmp_w': _normal(ks[11], (DEPTH, 2, NSA_BLOCK, HEAD_DIM, HEAD_DIM), (NSA_BLOCK * HEAD_DIM) ** -0.5),
        'diff_lambda': _normal(ks[12], (DEPTH, 4, DIFF_QK_DIM), 0.1),
        'diff_subln_g': 1.0 + _normal(ks[13], (DEPTH, HEAD_DIM), 0.02),
        'w_out': _normal(ks[14], (DEPTH, D_MODEL, D_MODEL), DEEPNORM_BETA * D_MODEL ** -0.5),
        'ln_attn_g': 1.0 + _normal(ks[15], (DEPTH, D_MODEL), 0.02),
        'ln_attn_b': _normal(ks[16], (DEPTH, D_MODEL), 0.02),
        'w_router': _normal(ks[17], (DEPTH, D_MODEL, N_EXPERTS), D_MODEL ** -0.5),
        'b_router': _normal(ks[18], (DEPTH, N_EXPERTS), 0.01),
        'w_up': _normal(ks[19], (DEPTH, N_EXPERTS, D_MODEL, 2 * D_FF), D_MODEL ** -0.5),
        'b_up': _normal(ks[20], (DEPTH, N_EXPERTS, 2 * D_FF), 0.02),
        'w_down': _normal(ks[21], (DEPTH, N_EXPERTS, D_FF, D_MODEL), DEEPNORM_BETA * D_FF ** -0.5),
        'b_down': _normal(ks[22], (DEPTH, N_EXPERTS, D_MODEL), 0.02),
        'ln_ffn_g': 1.0 + _normal(ks[23], (DEPTH, D_MODEL), 0.02),
        'ln_ffn_b': _normal(ks[24], (DEPTH, D_MODEL), 0.02),
    }


def reference(x_prompt, x_sample, cache_fox, cache_nsa, cache_diff, state_nsa_win, page_table,
              w_in, b_forget, b_nsa_gate, nsa_cmp_pos, nsa_cmp_w, diff_lambda, diff_subln_g,
              w_out, ln_attn_g, ln_attn_b, w_router, b_router, w_up, b_up, w_down, b_down,
              ln_ffn_g, ln_ffn_b):
    seq = x_prompt.shape[1]
    n_new = x_sample.shape[1]
    past = page_table.shape[1] * cache_fox.shape[2]
    pos_p = jnp.arange(seq)
    pos_s = past + jnp.arange(n_new)
    hp, hs = x_prompt, x_sample
    fox_p, fox_s = [], []
    nsa_p, nsa_s = [], []
    diff_p, diff_s = [], []
    win_p, win_s = [], []
    for l in range(DEPTH):
        lam, lam_init = _diff_lambda(diff_lambda[l], l)
        tail_w = (w_out[l], ln_attn_g[l], ln_attn_b[l], w_router[l], b_router[l], w_up[l], b_up[l],
                  w_down[l], b_down[l], ln_ffn_g[l], ln_ffn_b[l])
        fq, fox_rows, nq, ng, nsa_rows, win_rows, dq, diff_rows = _mixer_inputs(hp, pos_p, w_in[l], b_forget[l], b_nsa_gate[l])
        mix = jnp.concatenate([
            _fox_prompt(fq, fox_rows),
            _nsa_prompt(nq, ng, nsa_rows, win_rows, nsa_cmp_pos[l], nsa_cmp_w[l]),
            _diff_prompt(dq, diff_rows, lam, lam_init, diff_subln_g[l])], axis=-1)
        hp = _tail(hp, mix, *tail_w)
        fox_p.append(fox_rows)
        nsa_p.append(nsa_rows)
        diff_p.append(diff_rows)
        win_p.append(win_rows[:, seq - min(NSA_WINDOW, seq):])
        fq, fox_rows, nq, ng, nsa_rows, win_rows, dq, diff_rows = _mixer_inputs(hs, pos_s, w_in[l], b_forget[l], b_nsa_gate[l])
        fox_all = jnp.concatenate([_gather_pages(cache_fox, page_table, l), fox_rows.astype(cache_fox.dtype)], 1)
        nsa_all = jnp.concatenate([_gather_pages(cache_nsa, page_table, l), nsa_rows.astype(cache_nsa.dtype)], 1)
        diff_all = jnp.concatenate([_gather_pages(cache_diff, page_table, l), diff_rows.astype(cache_diff.dtype)], 1)
        win_all = jnp.concatenate([state_nsa_win[:, l], win_rows.astype(state_nsa_win.dtype)], 1)
        mix = jnp.concatenate([
            _fox_sample(fq, fox_all, past),
            _nsa_sample(nq, ng, nsa_all, win_all, past, nsa_cmp_pos[l], nsa_cmp_w[l]),
            _diff_sample(dq, diff_all, past, lam, lam_init, diff_subln_g[l])], axis=-1)
        hs = _tail(hs, mix, *tail_w)
        fox_s.append(fox_rows)
        nsa_s.append(nsa_rows)
        diff_s.append(diff_rows)
        n_keep = min(NSA_WINDOW, win_all.shape[1])
        win_s.append(win_all[:, win_all.shape[1] - n_keep:])
    return (hp, hs,
            jnp.stack(fox_p, axis=1), jnp.stack(fox_s, axis=1),
            jnp.stack(nsa_p, axis=1), jnp.stack(nsa_s, axis=1),
            jnp.stack(diff_p, axis=1), jnp.stack(diff_s, axis=1),
            jnp.stack(win_p, axis=1), jnp.stack(win_s, axis=1))
```

```python
import functools
import math

import numpy as np
import jax
import jax.numpy as jnp
from jax import lax
from jax.experimental import pallas as pl
from jax.experimental.pallas import tpu as pltpu

F32 = jnp.float32
BF16 = jnp.bfloat16

D_MODEL = 1024
DEPTH = 2
PAGE_SIZE = 128
HEAD_DIM = 64
FOX_HEADS = 4
NSA_HEADS = 8
NSA_KV_GROUPS = 2
NSA_HPG = NSA_HEADS // NSA_KV_GROUPS
DIFF_HEADS = 4
DIFF_QK_DIM = HEAD_DIM // 2
NSA_BLOCK = 64
BLOCK_SHIFT = 6
NSA_TOP_N = 16
NSA_WINDOW = 512
ROPE_THETA = 500000.0
N_EXPERTS = 32
TOP_K = 4
SWIGLU_LIMIT = 7.0
SWIGLU_ALPHA = 1.702
LN_EPS = 1e-5
RMS_EPS = 1e-5
DEEPNORM_ALPHA = (2 * DEPTH) ** 0.25
NEG = -1e30
FORCE_SCORE = 1e4

SUBLANES = 8
LANES = 128
VMEM_LIMIT_BYTES = 48 * 1024 * 1024

C_FQ, C_FK, C_FV, C_FF = 0, 256, 512, 768
C_NQ, C_NKC, C_NKW, C_NVW, C_NG = 772, 1284, 1796, 1924, 2052
C_DQ, C_DK, C_DV = 2076, 2332, 2588
N_IN = 2844

R_NQ = 0
R_NSA = 512
R_DIFF = 1024
R_FQ = 1536
R_FK = 1792
R_FV = 2048
R_WIN = 2304
R_DQ = 2560
R_GATE = 2816
R_FOX = 2848
R_LOGF = R_FOX + 512
NP_ROWS = 3368


def _build_perm():
    perm = np.full((NP_ROWS,), -1, np.int64)
    perm[R_NQ:R_NQ + 512] = C_NQ + np.arange(512)
    perm[R_NSA:R_NSA + 512] = C_NKC + np.arange(512)
    for h in range(DIFF_HEADS):
        perm[R_DIFF + h * 128:R_DIFF + h * 128 + 64] = C_DK + h * 64 + np.arange(64)
        perm[R_DIFF + h * 128 + 64:R_DIFF + h * 128 + 128] = C_DV + h * 64 + np.arange(64)
    perm[R_FQ:R_FQ + 256] = C_FQ + np.arange(256)
    perm[R_FK:R_FK + 256] = C_FK + np.arange(256)
    perm[R_FV:R_FV + 256] = C_FV + np.arange(256)
    perm[R_WIN:R_WIN + 256] = C_NKW + np.arange(256)
    perm[R_DQ:R_DQ + 256] = C_DQ + np.arange(256)
    perm[R_GATE:R_GATE + 24] = C_NG + np.arange(24)
    for c in range(64):
        for h in range(FOX_HEADS):
            perm[R_FOX + c * 4 + h] = C_FK + h * 64 + c
            perm[R_FOX + (64 + c) * 4 + h] = C_FV + h * 64 + c
    for h in range(FOX_HEADS):
        perm[R_FOX + 512 + h] = C_FF + h
    return perm


_PERM = _build_perm()
_ROT64_ROWS = tuple([R_NQ + h * 64 for h in range(NSA_HEADS)]
                    + [R_NSA + g * 64 for g in range(NSA_KV_GROUPS)]
                    + [R_NSA + 256 + g * 64 for g in range(NSA_KV_GROUPS)]
                    + [R_WIN + g * 64 for g in range(NSA_KV_GROUPS)])
_ROT32_ROWS = tuple([R_DQ + i * 32 for i in range(2 * DIFF_HEADS)]
                    + [R_DIFF + h * 128 + m * 32 for h in range(DIFF_HEADS) for m in range(2)])


def _cparams(sem):
    return pltpu.CompilerParams(dimension_semantics=sem, vmem_limit_bytes=VMEM_LIMIT_BYTES)


def _inproj_kernel(x_ref, w_ref, wg_ref, bgate_ref, bforget_ref, bgrow_ref, cn_ref, sn_ref, cd_ref, sd_ref,
                   o_ref, g_ref):
    x = x_ref[...].astype(BF16)
    o_ref[...] = lax.dot_general(w_ref[...], x, (((1,), (1,)), ((), ())), preferred_element_type=F32)
    cn = cn_ref[...]
    sn = sn_ref[...]
    for r in _ROT64_ROWS:
        x1 = o_ref[r:r + 8, :]
        x2 = o_ref[r + 8:r + 16, :]
        o_ref[r:r + 8, :] = x1 * cn - x2 * sn
        o_ref[r + 8:r + 16, :] = x2 * cn + x1 * sn
    cd = cd_ref[...]
    sd = sd_ref[...]
    for r in _ROT32_ROWS:
        v = o_ref[r:r + 8, :]
        o_ref[r:r + 8, :] = v * cd + pltpu.roll(v, 4, 0) * sd
    z = o_ref[R_GATE:R_GATE + 32, :] + bgate_ref[...]
    o_ref[R_GATE:R_GATE + 32, :] = 1.0 / (1.0 + jnp.exp(-z))
    z = o_ref[R_LOGF:R_LOGF + 8, :] + bforget_ref[...]
    o_ref[R_LOGF:R_LOGF + 8, :] = jnp.minimum(z, 0.0) - jnp.log1p(jnp.exp(-jnp.abs(z)))
    zg = jnp.dot(x, wg_ref[...], preferred_element_type=F32) + bgrow_ref[...]
    g_ref[...] = 1.0 / (1.0 + jnp.exp(-zg))


def _inproj(x, wt, wg, bgate_col, bforget_col, bgate_row, tabs, tn):
    nb, nt, k = x.shape
    cn, sn, cd, sd = tabs
    const = lambda b, i: (0, 0)
    tab = lambda b, i: (0, i)
    return pl.pallas_call(
        _inproj_kernel,
        grid=(nb, nt // tn),
        in_specs=[
            pl.BlockSpec((None, tn, k), lambda b, i: (b, i, 0)),
            pl.BlockSpec((NP_ROWS, k), const),
            pl.BlockSpec((k, LANES), const),
            pl.BlockSpec((32, 1), const),
            pl.BlockSpec((8, 1), const),
            pl.BlockSpec((1, LANES), const),
            pl.BlockSpec((8, tn), tab),
            pl.BlockSpec((8, tn), tab),
            pl.BlockSpec((8, tn), tab),
            pl.BlockSpec((8, tn), tab),
        ],
        out_specs=[
            pl.BlockSpec((None, NP_ROWS, tn), lambda b, i: (b, 0, i)),
            pl.BlockSpec((None, tn, LANES), lambda b, i: (b, i, 0)),
        ],
        out_shape=[jax.ShapeDtypeStruct((nb, NP_ROWS, nt), F32),
                   jax.ShapeDtypeStruct((nb, nt, LANES), F32)],
        compiler_params=_cparams(("parallel", "parallel")),
        name="inproj",
    )(x, wt, wg, bgate_col, bforget_col, bgate_row, cn, sn, cd, sd)


def _rope_tables(pos):
    inv8 = ROPE_THETA ** (-jnp.arange(8, dtype=F32) / 8)
    ang8 = pos[:, None] * inv8[None, :]
    cn, sn = jnp.cos(ang8).T, jnp.sin(ang8).T
    inv4 = ROPE_THETA ** (-jnp.arange(4, dtype=F32) / 4)
    ang4 = pos[:, None] * inv4[None, :]
    c4, s4 = jnp.cos(ang4).T, jnp.sin(ang4).T
    cd = jnp.concatenate([c4, c4], 0)
    sd = jnp.concatenate([-s4, s4], 0)
    return cn, sn, cd, sd


def _tri_pairs(nq, lo_tiles=None):
    qi, kj = [], []
    for i in range(nq):
        j0 = 0 if lo_tiles is None else max(0, i - lo_tiles)
        for j in range(j0, i + 1):
            qi.append(i)
            kj.append(j)
    first = [1 if (p == 0 or qi[p] != qi[p - 1]) else 0 for p in range(len(qi))]
    return (jnp.asarray(qi, jnp.int32), jnp.asarray(kj, jnp.int32), jnp.asarray(first, jnp.int32))


def _online_update(s, v_t, m_ref, l_ref, acc_ref):
    m_prev = m_ref[:, 0:1]
    m_new = jnp.maximum(m_prev, jnp.max(s, axis=1, keepdims=True))
    alpha = jnp.exp(m_prev - m_new)
    p = jnp.exp(s - m_new)
    l_ref[...] = jnp.broadcast_to(alpha * l_ref[:, 0:1] + jnp.sum(p, axis=1, keepdims=True), l_ref.shape)
    m_ref[...] = jnp.broadcast_to(m_new, m_ref.shape)
    pv = lax.dot_general(p.astype(BF16), v_t, (((1,), (1,)), ((), ())), preferred_element_type=F32)
    acc_ref[...] = alpha * acc_ref[...] + pv


def _fox_kernel(qi_ref, kj_ref, first_ref, q_ref, k_ref, v_ref, c_ref, o_ref, q_s, m_s, l_s, acc_s, *, tq, tk):
    p_id = pl.program_id(1)
    qi = qi_ref[p_id]
    kj = kj_ref[p_id]
    scale = HEAD_DIM ** -0.5

    @pl.when(first_ref[p_id] == 1)
    def _():
        for h in range(FOX_HEADS):
            q_s[h] = q_ref[h * 64:(h + 1) * 64, :].T.astype(BF16)
        m_s[...] = jnp.full(m_s.shape, NEG, F32)
        l_s[...] = jnp.zeros(l_s.shape, F32)
        acc_s[...] = jnp.zeros(acc_s.shape, F32)

    row = qi * tq + lax.broadcasted_iota(jnp.int32, (tq, tk), 0)
    col = kj * tk + lax.broadcasted_iota(jnp.int32, (tq, tk), 1)
    mask = col <= row
    for h in range(FOX_HEADS):
        k_t = k_ref[h * 64:(h + 1) * 64, :].astype(BF16)
        v_t = v_ref[h * 64:(h + 1) * 64, :].astype(BF16)
        s = jnp.dot(q_s[h], k_t, preferred_element_type=F32) * scale - c_ref[h:h + 1, :]
        s = jnp.where(mask, s, NEG)
        _online_update(s, v_t, m_s.at[h], l_s.at[h], acc_s.at[h])

    @pl.when(kj == qi)
    def _():
        for h in range(FOX_HEADS):
            o_ref[:, h * 64:(h + 1) * 64] = acc_s[h] / l_s[h][:, 0:1]


def _fox_attention(pt, c, tq):
    nb, _, nt = pt.shape
    tk = tq
    qi, kj, first = _tri_pairs(nt // tq)
    blk = 256
    grid_spec = pltpu.PrefetchScalarGridSpec(
        num_scalar_prefetch=3,
        grid=(nb, qi.shape[0]),
        in_specs=[
            pl.BlockSpec((None, blk, tq), lambda b, p, qi, kj, f: (b, R_FQ // blk, qi[p])),
            pl.BlockSpec((None, blk, tk), lambda b, p, qi, kj, f: (b, R_FK // blk, kj[p])),
            pl.BlockSpec((None, blk, tk), lambda b, p, qi, kj, f: (b, R_FV // blk, kj[p])),
            pl.BlockSpec((None, FOX_HEADS, tk), lambda b, p, qi, kj, f: (b, 0, kj[p])),
        ],
        out_specs=pl.BlockSpec((None, tq, blk), lambda b, p, qi, kj, f: (b, qi[p], 0)),
        scratch_shapes=[
            pltpu.VMEM((FOX_HEADS, tq, HEAD_DIM), BF16),
            pltpu.VMEM((FOX_HEADS, tq, LANES), F32),
            pltpu.VMEM((FOX_HEADS, tq, LANES), F32),
            pltpu.VMEM((FOX_HEADS, tq, HEAD_DIM), F32),
        ],
    )
    return pl.pallas_call(
        functools.partial(_fox_kernel, tq=tq, tk=tk),
        grid_spec=grid_spec,
        out_shape=jax.ShapeDtypeStruct((nb, nt, blk), F32),
        compiler_params=_cparams(("parallel", "arbitrary")),
        name="fox_attn",
    )(qi, kj, first, pt, pt, pt, c)


def _diff_kernel(qi_ref, kj_ref, first_ref, lam_ref, q_ref, kv_ref, g_ref, o_ref, q_s, m_s, l_s, acc_s,
                 *, tq, tk, out_scale):
    p_id = pl.program_id(1)
    qi = qi_ref[p_id]
    kj = kj_ref[p_id]
    scale = DIFF_QK_DIM ** -0.5

    @pl.when(first_ref[p_id] == 1)
    def _():
        for i in range(2 * DIFF_HEADS):
            q_s[i] = q_ref[i * 32:(i + 1) * 32, :].T.astype(BF16)
        m_s[...] = jnp.full(m_s.shape, NEG, F32)
        l_s[...] = jnp.zeros(l_s.shape, F32)
        acc_s[...] = jnp.zeros(acc_s.shape, F32)

    row = qi * tq + lax.broadcasted_iota(jnp.int32, (tq, tk), 0)
    col = kj * tk + lax.broadcasted_iota(jnp.int32, (tq, tk), 1)
    mask = col <= row
    for h in range(DIFF_HEADS):
        v_t = kv_ref[h * 128 + 64:h * 128 + 128, :].astype(BF16)
        for m in range(2):
            i = 2 * h + m
            k_t = kv_ref[h * 128 + m * 32:h * 128 + (m + 1) * 32, :].astype(BF16)
            s = jnp.dot(q_s[i], k_t, preferred_element_type=F32) * scale
            s = jnp.where(mask, s, NEG)
            _online_update(s, v_t, m_s.at[i], l_s.at[i], acc_s.at[i])

    @pl.when(kj == qi)
    def _():
        lam = lam_ref[0]
        for h in range(DIFF_HEADS):
            o1 = acc_s[2 * h] / l_s[2 * h][:, 0:1]
            o2 = acc_s[2 * h + 1] / l_s[2 * h + 1][:, 0:1]
            o = o1 - lam * o2
            o = o * lax.rsqrt(jnp.mean(o * o, axis=-1, keepdims=True) + RMS_EPS) * g_ref[...] * out_scale
            o_ref[:, h * 64:(h + 1) * 64] = o


def _diff_attention(pt, lam, g_row, out_scale, tq):
    nb, _, nt = pt.shape
    tk = tq
    qi, kj, first = _tri_pairs(nt // tq)
    grid_spec = pltpu.PrefetchScalarGridSpec(
        num_scalar_prefetch=4,
        grid=(nb, qi.shape[0]),
        in_specs=[
            pl.BlockSpec((None, 256, tq), lambda b, p, qi, kj, f, lam: (b, R_DQ // 256, qi[p])),
            pl.BlockSpec((None, 512, tk), lambda b, p, qi, kj, f, lam: (b, R_DIFF // 512, kj[p])),
            pl.BlockSpec((1, HEAD_DIM), lambda b, p, qi, kj, f, lam: (0, 0)),
        ],
        out_specs=pl.BlockSpec((None, tq, 256), lambda b, p, qi, kj, f, lam: (b, qi[p], 0)),
        scratch_shapes=[
            pltpu.VMEM((2 * DIFF_HEADS, tq, DIFF_QK_DIM), BF16),
            pltpu.VMEM((2 * DIFF_HEADS, tq, LANES), F32),
            pltpu.VMEM((2 * DIFF_HEADS, tq, LANES), F32),
            pltpu.VMEM((2 * DIFF_HEADS, tq, HEAD_DIM), F32),
        ],
    )
    return pl.pallas_call(
        functools.partial(_diff_kernel, tq=tq, tk=tk, out_scale=out_scale),
        grid_spec=grid_spec,
        out_shape=jax.ShapeDtypeStruct((nb, nt, 256), F32),
        compiler_params=_cparams(("parallel", "arbitrary")),
        name="diff_attn",
    )(qi, kj, first, lam, pt, pt, g_row)


def _compress_kernel(x_ref, pe_ref, w_ref, o_ref):
    x = (x_ref[...] + pe_ref[...]).astype(BF16)
    o_ref[...] = jnp.dot(x, w_ref[...].astype(BF16), preferred_element_type=F32)


def _compress(xb, pe, w, tm):
    _, m, kk = xb.shape
    return pl.pallas_call(
        _compress_kernel,
        grid=(2, m // tm),
        in_specs=[
            pl.BlockSpec((None, tm, kk), lambda t, i: (t, i, 0)),
            pl.BlockSpec((None, 1, kk), lambda t, i: (t, 0, 0)),
            pl.BlockSpec((None, kk, HEAD_DIM), lambda t, i: (t, 0, 0)),
        ],
        out_specs=pl.BlockSpec((None, tm, HEAD_DIM), lambda t, i: (t, i, 0)),
        out_shape=jax.ShapeDtypeStruct((2, m, HEAD_DIM), F32),
        compiler_params=_cparams(("parallel", "parallel")),
        name="nsa_compress",
    )(xb, pe, w)


def _select_mask(imp, qpos, n_blk):
    jb = lax.broadcasted_iota(jnp.int32, imp.shape, 1)
    cur = qpos >> BLOCK_SHIFT
    forced = (jb == 0) | (jb == cur) | (jb == cur - 1)
    score = jnp.where(jb <= cur, jnp.where(forced, FORCE_SCORE, imp), NEG)
    rank = jnp.zeros(imp.shape, F32)
    for j in range(n_blk):
        col = score[:, j:j + 1]
        ahead = (col > score) | ((col == score) & (jb > j))
        rank = rank + jnp.where(ahead, 1.0, 0.0)
    return jnp.where(rank < float(NSA_TOP_N), 1.0, 0.0)


def _nsa_cmp_kernel(q_ref, kc_ref, vc_ref, gate_ref, o_ref, sel_ref, *, tq, n_cmp):
    qi = pl.program_id(2)
    g = pl.program_id(1)
    scale = HEAD_DIM ** -0.5
    kc = kc_ref[...].astype(BF16)
    vc = vc_ref[...].astype(BF16)
    qpos = qi * tq + lax.broadcasted_iota(jnp.int32, (tq, 1), 0)
    nb_iota = lax.broadcasted_iota(jnp.int32, (tq, n_cmp), 1)
    mask = nb_iota < ((qpos + 1) >> BLOCK_SHIFT)
    maskf = jnp.where(mask, 1.0, 0.0)
    imp = jnp.zeros((tq, n_cmp), F32)
    gates = gate_ref[...]
    for h in range(NSA_HPG):
        qh = q_ref[h * 64:(h + 1) * 64, :].T.astype(BF16)
        lc = lax.dot_general(qh, kc, (((1,), (1,)), ((), ())), preferred_element_type=F32) * scale
        lc = jnp.where(mask, lc, NEG)
        m = jnp.max(lc, axis=1, keepdims=True)
        p = jnp.exp(lc - m) * maskf
        pc = p / jnp.maximum(jnp.sum(p, axis=1, keepdims=True), 1e-30)
        imp = imp + pc
        oc = jnp.dot(pc.astype(BF16), vc, preferred_element_type=F32)
        gcol = jnp.where(g == 0, gates[:, h:h + 1], gates[:, NSA_HPG + h:NSA_HPG + h + 1])
        o_ref[:, h * 64:(h + 1) * 64] = oc * gcol
    sel_ref[...] = _select_mask(imp, qpos, n_cmp)


def _nsa_cmp(pt, gates, kc, vc, tq):
    nb, _, nt = pt.shape
    n_cmp = kc.shape[2]
    return pl.pallas_call(
        functools.partial(_nsa_cmp_kernel, tq=tq, n_cmp=n_cmp),
        grid=(nb, NSA_KV_GROUPS, nt // tq),
        in_specs=[
            pl.BlockSpec((None, 256, tq), lambda b, g, i: (b, R_NQ // 256 + g, i)),
            pl.BlockSpec((None, None, n_cmp, HEAD_DIM), lambda b, g, i: (b, g, 0, 0)),
            pl.BlockSpec((None, None, n_cmp, HEAD_DIM), lambda b, g, i: (b, g, 0, 0)),
            pl.BlockSpec((None, tq, LANES), lambda b, g, i: (b, i, 0)),
        ],
        out_specs=[
            pl.BlockSpec((None, tq, 256), lambda b, g, i: (b, i, g)),
            pl.BlockSpec((None, None, tq, n_cmp), lambda b, g, i: (b, g, i, 0)),
        ],
        out_shape=[jax.ShapeDtypeStruct((nb, nt, 512), F32),
                   jax.ShapeDtypeStruct((nb, NSA_KV_GROUPS, nt, n_cmp), F32)],
        compiler_params=_cparams(("parallel", "parallel", "parallel")),
        name="nsa_cmp",
    )(pt, kc, vc, gates)


def _nsa_band_kernel(qi_ref, kj_ref, first_ref, last_ref, q_ref, k_ref, v_ref, gate_ref, *rest,
                     tq, tk, selected, gate_base):
    if selected:
        sel_ref, o_ref, q_s, m_s, l_s, acc_s = rest
    else:
        o_ref, q_s, m_s, l_s, acc_s = rest
    p_id = pl.program_id(2)
    g = pl.program_id(1)
    qi = qi_ref[p_id]
    kj = kj_ref[p_id]
    scale = HEAD_DIM ** -0.5

    @pl.when(first_ref[p_id] == 1)
    def _():
        for h in range(NSA_HPG):
            q_s[h * tq:(h + 1) * tq, :] = q_ref[h * 64:(h + 1) * 64, :].T.astype(BF16)
        m_s[...] = jnp.full(m_s.shape, NEG, F32)
        l_s[...] = jnp.zeros(l_s.shape, F32)
        acc_s[...] = jnp.zeros(acc_s.shape, F32)

    row = qi * tq + lax.broadcasted_iota(jnp.int32, (tq, tk), 0)
    col = kj * tk + lax.broadcasted_iota(jnp.int32, (tq, tk), 1)
    mask = col <= row
    if selected:
        n_blk = sel_ref.shape[-1]
        blk_of_col = kj * (tk // NSA_BLOCK) + (lax.broadcasted_iota(jnp.int32, (n_blk, tk), 1) >> BLOCK_SHIFT)
        expand = jnp.where(lax.broadcasted_iota(jnp.int32, (n_blk, tk), 0) == blk_of_col, 1.0, 0.0).astype(BF16)
        picked = jnp.dot(sel_ref[...].astype(BF16), expand, preferred_element_type=F32)
        mask = mask & (picked > 0.5)
    else:
        mask = mask & (col > row - NSA_WINDOW)
    keep = jnp.where(mask, 1.0, 0.0)
    k_t = k_ref[...].astype(BF16)
    v_t = v_ref[...].astype(BF16)
    s = jnp.dot(q_s[...], k_t, preferred_element_type=F32) * scale
    s = jnp.where(jnp.concatenate([keep] * NSA_HPG, axis=0) > 0.5, s, NEG)
    _online_update(s, v_t, m_s, l_s, acc_s)

    @pl.when(last_ref[p_id] == 1)
    def _():
        gates = gate_ref[...]
        for h in range(NSA_HPG):
            c0 = gate_base + h
            gcol = jnp.where(g == 0, gates[:, c0:c0 + 1], gates[:, c0 + NSA_HPG:c0 + NSA_HPG + 1])
            o_ref[:, h * 64:(h + 1) * 64] = acc_s[h * tq:(h + 1) * tq, :] / l_s[h * tq:(h + 1) * tq, 0:1] * gcol


def _nsa_band(pt, gates, sel, tq, selected):
    nb, _, nt = pt.shape
    tk = tq
    nq = nt // tq
    qi, kj, first = _tri_pairs(nq, None if selected else -(-NSA_WINDOW // tk))
    last = jnp.concatenate([first[1:], jnp.ones((1,), jnp.int32)])
    if selected:
        rk, rv, gate_base = R_NSA + 256, R_NSA + 384, 8
    else:
        rk, rv, gate_base = R_WIN, R_WIN + 128, 16
    idx = lambda f: (lambda b, g, p, qi, kj, fi, la: f(b, g, p, qi, kj))
    in_specs = [
        pl.BlockSpec((None, 256, tq), idx(lambda b, g, p, qi, kj: (b, R_NQ // 256 + g, qi[p]))),
        pl.BlockSpec((None, HEAD_DIM, tk), idx(lambda b, g, p, qi, kj: (b, rk // 64 + g, kj[p]))),
        pl.BlockSpec((None, HEAD_DIM, tk), idx(lambda b, g, p, qi, kj: (b, rv // 64 + g, kj[p]))),
        pl.BlockSpec((None, tq, LANES), idx(lambda b, g, p, qi, kj: (b, qi[p], 0))),
    ]
    args = [pt, pt, pt, gates]
    if selected:
        in_specs.append(pl.BlockSpec((None, None, tq, sel.shape[-1]),
                                     idx(lambda b, g, p, qi, kj: (b, g, qi[p], 0))))
        args.append(sel)
    grid_spec = pltpu.PrefetchScalarGridSpec(
        num_scalar_prefetch=4,
        grid=(nb, NSA_KV_GROUPS, qi.shape[0]),
        in_specs=in_specs,
        out_specs=pl.BlockSpec((None, tq, 256), idx(lambda b, g, p, qi, kj: (b, qi[p], g))),
        scratch_shapes=[
            pltpu.VMEM((NSA_HPG * tq, HEAD_DIM), BF16),
            pltpu.VMEM((NSA_HPG * tq, LANES), F32),
            pltpu.VMEM((NSA_HPG * tq, LANES), F32),
            pltpu.VMEM((NSA_HPG * tq, HEAD_DIM), F32),
        ],
    )
    return pl.pallas_call(
        functools.partial(_nsa_band_kernel, tq=tq, tk=tk, selected=selected, gate_base=gate_base),
        grid_spec=grid_spec,
        out_shape=jax.ShapeDtypeStruct((nb, nt, 512), F32),
        compiler_params=_cparams(("parallel", "parallel", "arbitrary")),
        name="nsa_sel" if selected else "nsa_win",
    )(qi, kj, first, last, *args)


def _layer_norm_rows(x, g, b):
    mu = jnp.mean(x, axis=-1, keepdims=True)
    xc = x - mu
    var = jnp.mean(xc * xc, axis=-1, keepdims=True)
    return xc * lax.rsqrt(var + LN_EPS) * g + b


def _outproj_kernel(fox_ref, n1_ref, n2_ref, n3_ref, diff_ref, h_ref, w_ref, g_ref, b_ref, wr_ref, br_ref,
                    o_ref, lg_ref):
    nsa = (n1_ref[...] + n2_ref[...] + n3_ref[...]).astype(BF16)
    y = jnp.dot(fox_ref[...].astype(BF16), w_ref[0:256, :], preferred_element_type=F32)
    y = y + jnp.dot(nsa, w_ref[256:768, :], preferred_element_type=F32)
    y = y + jnp.dot(diff_ref[...].astype(BF16), w_ref[768:1024, :], preferred_element_type=F32)
    h1 = _layer_norm_rows(DEEPNORM_ALPHA * h_ref[...] + y, g_ref[...], b_ref[...])
    o_ref[...] = h1
    lg_ref[...] = jnp.dot(h1.astype(BF16), wr_ref[...], preferred_element_type=F32) + br_ref[...]


def _outproj(fox, n1, n2, n3, diff, h, w, g, b, wr, br, tm):
    m = h.shape[0]
    row = lambda i: (i, 0)
    const = lambda i: (0, 0)
    return pl.pallas_call(
        _outproj_kernel,
        grid=(m // tm,),
        in_specs=[
            pl.BlockSpec((tm, 256), row), pl.BlockSpec((tm, 512), row), pl.BlockSpec((tm, 512), row),
            pl.BlockSpec((tm, 512), row), pl.BlockSpec((tm, 256), row), pl.BlockSpec((tm, D_MODEL), row),
            pl.BlockSpec((D_MODEL, D_MODEL), const), pl.BlockSpec((1, D_MODEL), const),
            pl.BlockSpec((1, D_MODEL), const), pl.BlockSpec((D_MODEL, LANES), const),
            pl.BlockSpec((1, LANES), const),
        ],
        out_specs=[pl.BlockSpec((tm, D_MODEL), row), pl.BlockSpec((tm, LANES), row)],
        out_shape=[jax.ShapeDtypeStruct((m, D_MODEL), F32), jax.ShapeDtypeStruct((m, LANES), F32)],
        compiler_params=_cparams(("parallel",)),
        name="outproj_ln",
    )(fox, n1, n2, n3, diff, h, w, g, b, wr, br)


def _moe_kernel(te_ref, nv_ref, x_ref, wg_ref, wl_ref, bg_ref, bl_ref, wd_ref, bd_ref, o_ref):
    i = pl.program_id(0)

    @pl.when(i < nv_ref[0])
    def _():
        x = x_ref[...]
        hg = jnp.dot(x, wg_ref[...], preferred_element_type=F32) + bg_ref[...]
        hl = jnp.dot(x, wl_ref[...], preferred_element_type=F32) + bl_ref[...]
        glu = jnp.minimum(hg, SWIGLU_LIMIT)
        lin = jnp.clip(hl, -SWIGLU_LIMIT, SWIGLU_LIMIT)
        act = glu * (1.0 / (1.0 + jnp.exp(-SWIGLU_ALPHA * glu))) * (lin + 1.0)
        o_ref[...] = jnp.dot(act.astype(BF16), wd_ref[...], preferred_element_type=F32) + bd_ref[...]

    @pl.when(i >= nv_ref[0])
    def _():
        o_ref[...] = jnp.zeros(o_ref.shape, F32)


def _moe_experts(tile_e, n_valid, xs, wg, wl, bg, bl, wd, bd, tm):
    n_rows = xs.shape[0]
    ex = lambda i, te, nv: (te[i], 0, 0)
    grid_spec = pltpu.PrefetchScalarGridSpec(
        num_scalar_prefetch=2,
        grid=(n_rows // tm,),
        in_specs=[
            pl.BlockSpec((tm, D_MODEL), lambda i, te, nv: (i, 0)),
            pl.BlockSpec((None, D_MODEL, D_MODEL), ex),
            pl.BlockSpec((None, D_MODEL, D_MODEL), ex),
            pl.BlockSpec((None, 1, D_MODEL), ex),
            pl.BlockSpec((None, 1, D_MODEL), ex),
            pl.BlockSpec((None, D_MODEL, D_MODEL), ex),
            pl.BlockSpec((None, 1, D_MODEL), ex),
        ],
        out_specs=pl.BlockSpec((tm, D_MODEL), lambda i, te, nv: (i, 0)),
    )
    return pl.pallas_call(
        _moe_kernel,
        grid_spec=grid_spec,
        out_shape=jax.ShapeDtypeStruct((n_rows, D_MODEL), F32),
        compiler_params=_cparams(("arbitrary",)),
        name="moe_experts",
    )(tile_e, n_valid, xs, wg, wl, bg, bl, wd, bd)


def _combine_kernel(h_ref, y_ref, gate_ref, g_ref, b_ref, o_ref):
    gates = gate_ref[...]
    f = y_ref[0] * gates[:, 0:1]
    for k in range(1, TOP_K):
        f = f + y_ref[k] * gates[:, k:k + 1]
    o_ref[...] = _layer_norm_rows(DEEPNORM_ALPHA * h_ref[...] + f, g_ref[...], b_ref[...])


def _combine_ln(h1, yg, gates, g, b, tm):
    m = h1.shape[0]
    row = lambda i: (i, 0)
    const = lambda i: (0, 0)
    return pl.pallas_call(
        _combine_kernel,
        grid=(m // tm,),
        in_specs=[
            pl.BlockSpec((tm, D_MODEL), row),
            pl.BlockSpec((TOP_K, tm, D_MODEL), lambda i: (0, i, 0)),
            pl.BlockSpec((tm, TOP_K), row),
            pl.BlockSpec((1, D_MODEL), const), pl.BlockSpec((1, D_MODEL), const),
        ],
        out_specs=pl.BlockSpec((tm, D_MODEL), row),
        out_shape=jax.ShapeDtypeStruct((m, D_MODEL), F32),
        compiler_params=_cparams(("parallel",)),
        name="moe_combine_ln",
    )(h1, yg, gates, g, b)


def _moe_block(h1, logits, wts, ln_g, ln_b, tm_moe, tm_row):
    wg, wl, bg, bl, wd, bd = wts
    m = h1.shape[0]
    top_val, top_idx = lax.top_k(logits, TOP_K)
    gate = jax.nn.softmax(top_val, axis=-1)
    n = m * TOP_K
    flat_e = top_idx.reshape(n)
    order = jnp.argsort(flat_e)
    sorted_e = flat_e[order]
    counts = jnp.bincount(flat_e, length=N_EXPERTS)
    padded = (counts + tm_moe - 1) // tm_moe * tm_moe
    starts = jnp.cumsum(counts) - counts
    pends = jnp.cumsum(padded)
    dest = ((pends - padded)[sorted_e] + jnp.arange(n) - starts[sorted_e]).astype(jnp.int32)
    n_rows = -(-n // tm_moe) * tm_moe + N_EXPERTS * tm_moe
    n_tiles = n_rows // tm_moe
    row_tok = jnp.full((n_rows,), m, jnp.int32).at[dest].set((order // TOP_K).astype(jnp.int32))
    tile_e = jnp.minimum(jnp.searchsorted(pends, jnp.arange(n_tiles) * tm_moe, side='right'),
                         N_EXPERTS - 1).astype(jnp.int32)
    n_valid = (pends[-1] // tm_moe).astype(jnp.int32).reshape(1)
    pos = jnp.zeros((n,), jnp.int32).at[order].set(dest).reshape(m, TOP_K)
    xs = jnp.concatenate([h1.astype(BF16), jnp.zeros((1, D_MODEL), BF16)], 0)[row_tok]
    out = _moe_experts(tile_e, n_valid, xs, wg, wl, bg, bl, wd, bd, tm_moe)
    yg = out[pos.T]
    return _combine_ln(h1, yg, gate, ln_g, ln_b, tm_row)


def _masked_softmax(logits, mask):
    logits = jnp.where(mask, logits, NEG)
    m = jnp.max(logits, -1, keepdims=True)
    p = jnp.exp(logits - m) * mask
    return p / jnp.maximum(jnp.sum(p, -1, keepdims=True), 1e-30)


def _gather_pages(pool, page_table, layer):
    pages = pool[page_table, layer]
    return pages.reshape((page_table.shape[0], page_table.shape[1] * pool.shape[2]) + pool.shape[3:])


def _fox_sample(q, rows_all, past):
    c = jnp.cumsum(rows_all[..., 2 * HEAD_DIM].astype(F32), axis=1)
    k, v = rows_all[..., :HEAD_DIM], rows_all[..., HEAD_DIM:2 * HEAD_DIM]
    cq = c[:, past:]
    qpos = past + jnp.arange(q.shape[1])
    kpos = jnp.arange(k.shape[1])
    logits = jnp.einsum('bqhd,bkhd->bhqk', q, k, preferred_element_type=F32) * (HEAD_DIM ** -0.5)
    logits = logits + jnp.swapaxes(cq, 1, 2)[:, :, :, None] - jnp.swapaxes(c, 1, 2)[:, :, None, :]
    mask = kpos[None, :] <= qpos[:, None]
    p = jax.nn.softmax(jnp.where(mask, logits, NEG), axis=-1)
    o = jnp.einsum('bhqk,bkhd->bqhd', p.astype(v.dtype), v)
    return o.reshape(o.shape[0], o.shape[1], -1)


def _nsa_context(rows, cmp_pos, cmp_w):
    n_b, n_l = rows.shape[0], rows.shape[1]
    n_cmp = n_l // NSA_BLOCK
    n_blk = -(-n_l // NSA_BLOCK)

    def compress(t, pe, w):
        blk = t[:, :n_cmp * NSA_BLOCK].reshape(n_b, n_cmp, NSA_BLOCK, NSA_KV_GROUPS, HEAD_DIM)
        blk = blk + pe[None, None, :, None, :]
        return jnp.einsum('bnigd,ide->bnge', blk, w)

    def blocks(t):
        t = jnp.pad(t, ((0, 0), (0, n_blk * NSA_BLOCK - n_l), (0, 0), (0, 0)))
        return t.reshape(n_b, n_blk, NSA_BLOCK, NSA_KV_GROUPS, HEAD_DIM).transpose(0, 3, 1, 2, 4)

    return (compress(rows[:, :, 0], cmp_pos[0], cmp_w[0]), compress(rows[:, :, 1], cmp_pos[1], cmp_w[1]),
            blocks(rows[:, :, 2]), blocks(rows[:, :, 3]))


def _nsa_attend(q, gate, qpos, k_cmp, v_cmp, k_sel, v_sel, k_win, v_win, wpos):
    n_b, n_q = q.shape[0], q.shape[1]
    scale = HEAD_DIM ** -0.5
    qg = q.reshape(n_b, n_q, NSA_KV_GROUPS, NSA_HPG, HEAD_DIM).transpose(0, 2, 1, 3, 4)
    gate = gate.reshape(n_b, n_q, 3, NSA_KV_GROUPS, NSA_HPG).transpose(2, 0, 3, 1, 4)[..., None]
    n_cmp = k_cmp.shape[1]
    cmp_end = jnp.arange(n_cmp) * NSA_BLOCK + (NSA_BLOCK - 1)
    lc = jnp.einsum('bgqhd,bngd->bgqhn', qg, k_cmp, preferred_element_type=F32) * scale
    pc = _masked_softmax(lc, (cmp_end[None, :] <= qpos[:, None])[None, None, :, None, :])
    oc = jnp.einsum('bgqhn,bngd->bgqhd', pc.astype(v_cmp.dtype), v_cmp)
    n_blk = k_sel.shape[2]
    imp = jnp.pad(jnp.sum(pc, axis=3), ((0, 0), (0, 0), (0, 0), (0, n_blk - n_cmp)))
    jb = jnp.arange(n_blk)[None, :]
    cur = (qpos // NSA_BLOCK)[:, None]
    forced = (jb == 0) | (jb == cur) | (jb == cur - 1)
    score = jnp.where(jb <= cur, jnp.where(forced, FORCE_SCORE, imp), NEG)
    _, idx = lax.top_k(score, min(NSA_TOP_N, n_blk))
    n_sel = idx.shape[-1]
    bi = jnp.arange(n_b)[:, None, None, None]
    gi = jnp.arange(NSA_KV_GROUPS)[None, :, None, None]
    ks = k_sel[bi, gi, idx].reshape(n_b, NSA_KV_GROUPS, n_q, n_sel * NSA_BLOCK, HEAD_DIM)
    vs = v_sel[bi, gi, idx].reshape(n_b, NSA_KV_GROUPS, n_q, n_sel * NSA_BLOCK, HEAD_DIM)
    kpos = (idx[..., None] * NSA_BLOCK + jnp.arange(NSA_BLOCK)).reshape(n_b, NSA_KV_GROUPS, n_q, 1, n_sel * NSA_BLOCK)
    ls = jnp.einsum('bgqhd,bgqkd->bgqhk', qg, ks, preferred_element_type=F32) * scale
    ps = _masked_softmax(ls, kpos <= qpos[None, None, :, None, None])
    os_ = jnp.einsum('bgqhk,bgqkd->bgqhd', ps.astype(vs.dtype), vs)
    lw = jnp.einsum('bgqhd,bkgd->bgqhk', qg, k_win, preferred_element_type=F32) * scale
    wmask = (wpos[None, :] <= qpos[:, None]) & (wpos[None, :] > qpos[:, None] - NSA_WINDOW) & (wpos[None, :] >= 0)
    pw = _masked_softmax(lw, wmask[None, None, :, None, :])
    ow = jnp.einsum('bgqhk,bkgd->bgqhd', pw.astype(v_win.dtype), v_win)
    o = gate[0] * oc + gate[1] * os_ + gate[2] * ow
    return o.transpose(0, 2, 1, 3, 4).reshape(n_b, n_q, NSA_HEADS * HEAD_DIM).astype(q.dtype)


def _nsa_sample(q, gate, rows_all, win_all, past, cmp_pos, cmp_w):
    n_new = q.shape[1]
    k_cmp, v_cmp, k_sel, v_sel = _nsa_context(rows_all, cmp_pos, cmp_w)
    n_w = win_all.shape[1]
    wpos = past + n_new - n_w + jnp.arange(n_w)
    return _nsa_attend(q, gate, past + jnp.arange(n_new), k_cmp, v_cmp, k_sel, v_sel,
                       win_all[:, :, 0], win_all[:, :, 1], wpos)


def _diff_sample(q, rows_all, past, lam, lam_init, g):
    n_b, n_l = rows_all.shape[0], rows_all.shape[1]
    k = rows_all[:, :, :, 0].reshape(n_b, n_l, DIFF_HEADS, 2, DIFF_QK_DIM)
    v = rows_all[:, :, :, 1]
    qpos = past + jnp.arange(q.shape[1])
    kpos = jnp.arange(n_l)
    logits = jnp.einsum('bqhmd,bkhmd->bhmqk', q, k, preferred_element_type=F32) * (DIFF_QK_DIM ** -0.5)
    mask = kpos[None, :] <= qpos[:, None]
    p = jax.nn.softmax(jnp.where(mask, logits, NEG), axis=-1)
    a = p[:, :, 0] - lam * p[:, :, 1]
    o = jnp.einsum('bhqk,bkhd->bqhd', a.astype(v.dtype), v).astype(F32)
    o = o * lax.rsqrt(jnp.mean(o * o, -1, keepdims=True) + RMS_EPS) * g.astype(F32) * (1.0 - lam_init)
    return o.reshape(o.shape[0], o.shape[1], -1).astype(v.dtype)


def _diff_lambda(lp, layer):
    lam_init = 0.8 - 0.6 * math.exp(-0.3 * layer)
    lp = lp.astype(F32)
    lam = jnp.exp(jnp.sum(lp[0] * lp[1])) - jnp.exp(jnp.sum(lp[2] * lp[3])) + lam_init
    return lam, lam_init


def _pad_rows(a, n):
    return jnp.concatenate([a, jnp.zeros((n - a.shape[0],) + a.shape[1:], a.dtype)], 0)


def kernel(x_prompt, x_sample, cache_fox, cache_nsa, cache_diff, state_nsa_win, page_table,
           w_in, b_forget, b_nsa_gate, nsa_cmp_pos, nsa_cmp_w, diff_lambda, diff_subln_g,
           w_out, ln_attn_g, ln_attn_b, w_router, b_router, w_up, b_up, w_down, b_down,
           ln_ffn_g, ln_ffn_b):
    nb, seq, _ = x_prompt.shape
    n_dec, n_new, _ = x_sample.shape
    past = page_table.shape[1] * cache_fox.shape[2]
    tabs_p = _rope_tables(jnp.arange(seq, dtype=F32))
    n_dec_pad = LANES
    tabs_s = _rope_tables(jnp.full((n_dec_pad,), past, F32))
    perm = jnp.asarray(np.maximum(_PERM, 0), jnp.int32)
    valid = jnp.asarray((_PERM >= 0).astype(np.float32))[:, None]

    hp = x_prompt
    hs = x_sample.reshape(n_dec, D_MODEL)
    m_p = nb * seq
    m_all = m_p + n_dec
    m_pad = -(-m_all // 512) * 512
    outs = {k: [] for k in ('fox_p', 'fox_s', 'nsa_p', 'nsa_s', 'diff_p', 'diff_s', 'win_p', 'win_s')}
    for l in range(DEPTH):
        lam, lam_init = _diff_lambda(diff_lambda[l], l)
        lam_arr = lam.reshape(1).astype(F32)
        wt = (jnp.take(w_in[l], perm, axis=1).T * valid).astype(BF16)
        wg_tok = jnp.pad(w_in[l][:, C_NG:C_NG + 24], ((0, 0), (0, LANES - 24))).astype(BF16)
        bgate_col = jnp.pad(b_nsa_gate[l], (0, 8)).reshape(32, 1)
        bgate_row = jnp.pad(b_nsa_gate[l], (0, LANES - 24)).reshape(1, LANES)
        bforget_col = jnp.pad(b_forget[l], (0, 4)).reshape(8, 1)
        w_out_b = w_out[l].astype(BF16)
        wr = jnp.pad(w_router[l], ((0, 0), (0, LANES - N_EXPERTS))).astype(BF16)
        br = jnp.pad(b_router[l], (0, LANES - N_EXPERTS)).reshape(1, LANES)
        moe_w = (w_up[l][:, :, 0::2].astype(BF16), w_up[l][:, :, 1::2].astype(BF16),
                 b_up[l][:, None, 0::2], b_up[l][:, None, 1::2],
                 w_down[l].astype(BF16), b_down[l][:, None, :])
        g1, b1 = ln_attn_g[l].reshape(1, -1), ln_attn_b[l].reshape(1, -1)
        g2, b2 = ln_ffn_g[l].reshape(1, -1), ln_ffn_b[l].reshape(1, -1)

        pt, gates = _inproj(hp, wt, wg_tok, bgate_col, bforget_col, bgate_row, tabs_p, 256)
        outs['fox_p'].append(pt[:, R_FOX:R_FOX + 516].reshape(nb, 129, FOX_HEADS, seq).transpose(0, 3, 2, 1))
        outs['nsa_p'].append(pt[:, R_NSA:R_NSA + 512].reshape(nb, 4, NSA_KV_GROUPS, HEAD_DIM, seq)
                             .transpose(0, 4, 1, 2, 3))
        outs['diff_p'].append(pt[:, R_DIFF:R_DIFF + 512].reshape(nb, DIFF_HEADS, 2, HEAD_DIM, seq)
                              .transpose(0, 4, 1, 2, 3))
        outs['win_p'].append(pt[:, R_WIN:R_WIN + 256, seq - NSA_WINDOW:]
                             .reshape(nb, 2, NSA_KV_GROUPS, HEAD_DIM, NSA_WINDOW).transpose(0, 4, 1, 2, 3))
        c = jnp.cumsum(pt[:, R_LOGF:R_LOGF + FOX_HEADS, :], axis=-1)
        o_fox = _fox_attention(pt, c, 512)
        o_diff = _diff_attention(pt, lam_arr, diff_subln_g[l].reshape(1, HEAD_DIM), 1.0 - lam_init, 512)
        n_cmp = seq // NSA_BLOCK
        kcv = pt[:, R_NSA:R_NSA + 256].reshape(nb, 2, NSA_KV_GROUPS, HEAD_DIM, n_cmp, NSA_BLOCK)
        xb = kcv.transpose(1, 0, 2, 4, 5, 3).reshape(2, nb * NSA_KV_GROUPS * n_cmp, NSA_BLOCK * HEAD_DIM)
        cmp = _compress(xb, nsa_cmp_pos[l].reshape(2, 1, NSA_BLOCK * HEAD_DIM),
                        nsa_cmp_w[l].reshape(2, NSA_BLOCK * HEAD_DIM, HEAD_DIM), 128)
        cmp = cmp.reshape(2, nb, NSA_KV_GROUPS, n_cmp, HEAD_DIM)
        o_cmp, sel = _nsa_cmp(pt, gates, cmp[0], cmp[1], 256)
        o_sel = _nsa_band(pt, gates, sel, 256, True)
        o_win = _nsa_band(pt, gates, None, 256, False)

        xs_pad = _pad_rows(hs, n_dec_pad)[None]
        pt_s, _ = _inproj(xs_pad, wt, wg_tok, bgate_col, bforget_col, bgate_row, tabs_s, n_dec_pad)
        ps = pt_s[0, :, :n_dec].T
        fq_s = ps[:, R_FQ:R_FQ + 256].reshape(n_dec, 1, FOX_HEADS, HEAD_DIM)
        fox_rows_s = ps[:, R_FOX:R_FOX + 516].reshape(n_dec, 1, 129, FOX_HEADS).transpose(0, 1, 3, 2)
        nq_s = ps[:, R_NQ:R_NQ + 512].reshape(n_dec, 1, NSA_HEADS, HEAD_DIM)
        ng_s = ps[:, R_GATE:R_GATE + 24].reshape(n_dec, 1, 3, NSA_HEADS)
        nsa_rows_s = ps[:, R_NSA:R_NSA + 512].reshape(n_dec, 1, 4, NSA_KV_GROUPS, HEAD_DIM)
        win_rows_s = ps[:, R_WIN:R_WIN + 256].reshape(n_dec, 1, 2, NSA_KV_GROUPS, HEAD_DIM)
        dq_s = ps[:, R_DQ:R_DQ + 256].reshape(n_dec, 1, DIFF_HEADS, 2, DIFF_QK_DIM)
        diff_rows_s = ps[:, R_DIFF:R_DIFF + 512].reshape(n_dec, 1, DIFF_HEADS, 2, HEAD_DIM)
        fox_all = jnp.concatenate([_gather_pages(cache_fox, page_table, l), fox_rows_s], 1)
        nsa_all = jnp.concatenate([_gather_pages(cache_nsa, page_table, l), nsa_rows_s], 1)
        diff_all = jnp.concatenate([_gather_pages(cache_diff, page_table, l), diff_rows_s], 1)
        win_all = jnp.concatenate([state_nsa_win[:, l], win_rows_s], 1)
        s_fox = _fox_sample(fq_s, fox_all, past).reshape(n_dec, 256)
        s_nsa = _nsa_sample(nq_s, ng_s, nsa_all, win_all, past, nsa_cmp_pos[l], nsa_cmp_w[l]).reshape(n_dec, 512)
        s_diff = _diff_sample(dq_s, diff_all, past, lam, lam_init, diff_subln_g[l]).reshape(n_dec, 256)
        outs['fox_s'].append(fox_rows_s)
        outs['nsa_s'].append(nsa_rows_s)
        outs['diff_s'].append(diff_rows_s)
        n_keep = min(NSA_WINDOW, win_all.shape[1])
        outs['win_s'].append(win_all[:, win_all.shape[1] - n_keep:])

        def rows(a_p, a_s):
            return _pad_rows(jnp.concatenate([a_p.reshape(m_p, -1), a_s], 0), m_pad)

        zero_s = jnp.zeros((n_dec, 512), F32)
        h_all = rows(hp, hs)
        h1, logits = _outproj(rows(o_fox, s_fox), rows(o_cmp, s_nsa), rows(o_sel, zero_s), rows(o_win, zero_s),
                              rows(o_diff, s_diff), h_all, w_out_b, g1, b1, wr, br, 512)
        h2 = _moe_block(h1, logits[:, :N_EXPERTS], moe_w, g2, b2, 512, 512)
        hp = h2[:m_p].reshape(nb, seq, D_MODEL)
        hs = h2[m_p:m_all]

    st = lambda k: jnp.stack(outs[k], axis=1)
    return (hp, hs.reshape(n_dec, n_new, D_MODEL),
            st('fox_p'), st('fox_s'), st('nsa_p'), st('nsa_s'),
            st('diff_p'), st('diff_s'), st('win_p'), st('win_s'))
```

```python
import collections
import functools
import math

import numpy as np
import jax
import jax.numpy as jnp
from jax import lax
from jax.experimental import pallas as pl
from jax.experimental.pallas import tpu as pltpu

F32 = jnp.float32
BF16 = jnp.bfloat16

D_MODEL = 1024
DEPTH = 2
PAGE_SIZE = 128
HEAD_DIM = 64
FOX_HEADS = 4
NSA_HEADS = 8
NSA_KV_GROUPS = 2
NSA_HPG = NSA_HEADS // NSA_KV_GROUPS
DIFF_HEADS = 4
DIFF_QK_DIM = HEAD_DIM // 2
NSA_BLOCK = 64
BLOCK_SHIFT = 6
NSA_TOP_N = 16
NSA_WINDOW = 512
ROPE_THETA = 500000.0
N_EXPERTS = 32
TOP_K = 4
SWIGLU_LIMIT = 7.0
SWIGLU_ALPHA = 1.702
LN_EPS = 1e-5
RMS_EPS = 1e-5
DEEPNORM_ALPHA = (2 * DEPTH) ** 0.25
NEG = -1e30
FORCE_SCORE = 1e4

SUBLANES = 8
LANES = 128
VMEM_LIMIT_BYTES = 48 * 1024 * 1024
SAMPLE_PAGES_PER_STEP = 8

C_FQ, C_FK, C_FV, C_FF = 0, 256, 512, 768
C_NQ, C_NKC, C_NKW, C_NVW, C_NG = 772, 1284, 1796, 1924, 2052
C_DQ, C_DK, C_DV = 2076, 2332, 2588
N_IN = 2844

R_NQ = 0
R_NSA = 512
R_DIFF = 1024
R_FQ = 1536
R_FK = 1792
R_FV = 2048
R_WIN = 2304
R_DQ = 2560
R_GATE = 2816
R_FOX = 2848
R_LOGF = R_FOX + 512
NP_ROWS = 3368


def _build_perm():
    perm = np.full((NP_ROWS,), -1, np.int64)
    perm[R_NQ:R_NQ + 512] = C_NQ + np.arange(512)
    perm[R_NSA:R_NSA + 512] = C_NKC + np.arange(512)
    for h in range(DIFF_HEADS):
        perm[R_DIFF + h * 128:R_DIFF + h * 128 + 64] = C_DK + h * 64 + np.arange(64)
        perm[R_DIFF + h * 128 + 64:R_DIFF + h * 128 + 128] = C_DV + h * 64 + np.arange(64)
    perm[R_FQ:R_FQ + 256] = C_FQ + np.arange(256)
    perm[R_FK:R_FK + 256] = C_FK + np.arange(256)
    perm[R_FV:R_FV + 256] = C_FV + np.arange(256)
    perm[R_WIN:R_WIN + 256] = C_NKW + np.arange(256)
    perm[R_DQ:R_DQ + 256] = C_DQ + np.arange(256)
    perm[R_GATE:R_GATE + 24] = C_NG + np.arange(24)
    for c in range(64):
        for h in range(FOX_HEADS):
            perm[R_FOX + c * 4 + h] = C_FK + h * 64 + c
            perm[R_FOX + (64 + c) * 4 + h] = C_FV + h * 64 + c
    for h in range(FOX_HEADS):
        perm[R_FOX + 512 + h] = C_FF + h
    return perm


_PERM = _build_perm()
_ROT64_ROWS = tuple([R_NQ + h * 64 for h in range(NSA_HEADS)]
                    + [R_NSA + g * 64 for g in range(NSA_KV_GROUPS)]
                    + [R_NSA + 256 + g * 64 for g in range(NSA_KV_GROUPS)]
                    + [R_WIN + g * 64 for g in range(NSA_KV_GROUPS)])
_ROT32_ROWS = tuple([R_DQ + i * 32 for i in range(2 * DIFF_HEADS)]
                    + [R_DIFF + h * 128 + m * 32 for h in range(DIFF_HEADS) for m in range(2)])


def _cparams(sem):
    return pltpu.CompilerParams(dimension_semantics=sem, vmem_limit_bytes=VMEM_LIMIT_BYTES)


def _inproj_kernel(x_ref, w_ref, wg_ref, bgate_ref, bforget_ref, bgrow_ref, cn_ref, sn_ref, cd_ref, sd_ref,
                   o_ref, g_ref):
    x = x_ref[...].astype(BF16)
    o_ref[...] = lax.dot_general(w_ref[...], x, (((1,), (1,)), ((), ())), preferred_element_type=F32)
    cn = cn_ref[...]
    sn = sn_ref[...]
    for r in _ROT64_ROWS:
        x1 = o_ref[r:r + 8, :]
        x2 = o_ref[r + 8:r + 16, :]
        o_ref[r:r + 8, :] = x1 * cn - x2 * sn
        o_ref[r + 8:r + 16, :] = x2 * cn + x1 * sn
    cd = cd_ref[...]
    sd = sd_ref[...]
    for r in _ROT32_ROWS:
        v = o_ref[r:r + 8, :]
        o_ref[r:r + 8, :] = v * cd + pltpu.roll(v, 4, 0) * sd
    z = o_ref[R_GATE:R_GATE + 32, :] + bgate_ref[...]
    o_ref[R_GATE:R_GATE + 32, :] = 1.0 / (1.0 + jnp.exp(-z))
    z = o_ref[R_LOGF:R_LOGF + 8, :] + bforget_ref[...]
    o_ref[R_LOGF:R_LOGF + 8, :] = jnp.minimum(z, 0.0) - jnp.log1p(jnp.exp(-jnp.abs(z)))
    zg = jnp.dot(x, wg_ref[...], preferred_element_type=F32) + bgrow_ref[...]
    g_ref[...] = 1.0 / (1.0 + jnp.exp(-zg))


def _inproj(x, wt, wg, bgate_col, bforget_col, bgate_row, tabs, tn):
    nb, nt, k = x.shape
    cn, sn, cd, sd = tabs
    const = lambda b, i: (0, 0)
    tab = lambda b, i: (0, i)
    return pl.pallas_call(
        _inproj_kernel,
        grid=(nb, nt // tn),
        in_specs=[
            pl.BlockSpec((None, tn, k), lambda b, i: (b, i, 0)),
            pl.BlockSpec((NP_ROWS, k), const),
            pl.BlockSpec((k, LANES), const),
            pl.BlockSpec((32, 1), const),
            pl.BlockSpec((8, 1), const),
            pl.BlockSpec((1, LANES), const),
            pl.BlockSpec((8, tn), tab),
            pl.BlockSpec((8, tn), tab),
            pl.BlockSpec((8, tn), tab),
            pl.BlockSpec((8, tn), tab),
        ],
        out_specs=[
            pl.BlockSpec((None, NP_ROWS, tn), lambda b, i: (b, 0, i)),
            pl.BlockSpec((None, tn, LANES), lambda b, i: (b, i, 0)),
        ],
        out_shape=[jax.ShapeDtypeStruct((nb, NP_ROWS, nt), F32),
                   jax.ShapeDtypeStruct((nb, nt, LANES), F32)],
        compiler_params=_cparams(("parallel", "parallel")),
        name="inproj",
    )(x, wt, wg, bgate_col, bforget_col, bgate_row, cn, sn, cd, sd)


def _rope_tables(pos):
    inv8 = ROPE_THETA ** (-jnp.arange(8, dtype=F32) / 8)
    ang8 = pos[:, None] * inv8[None, :]
    cn, sn = jnp.cos(ang8).T, jnp.sin(ang8).T
    inv4 = ROPE_THETA ** (-jnp.arange(4, dtype=F32) / 4)
    ang4 = pos[:, None] * inv4[None, :]
    c4, s4 = jnp.cos(ang4).T, jnp.sin(ang4).T
    cd = jnp.concatenate([c4, c4], 0)
    sd = jnp.concatenate([-s4, s4], 0)
    return cn, sn, cd, sd


def _tri_pairs(nq, lo_tiles=None):
    qi, kj = [], []
    for i in range(nq):
        j0 = 0 if lo_tiles is None else max(0, i - lo_tiles)
        for j in range(j0, i + 1):
            qi.append(i)
            kj.append(j)
    first = [1 if (p == 0 or qi[p] != qi[p - 1]) else 0 for p in range(len(qi))]
    return (jnp.asarray(qi, jnp.int32), jnp.asarray(kj, jnp.int32), jnp.asarray(first, jnp.int32))


def _online_update(s, v_t, m_ref, l_ref, acc_ref):
    m_prev = m_ref[:, 0:1]
    m_new = jnp.maximum(m_prev, jnp.max(s, axis=1, keepdims=True))
    alpha = jnp.exp(m_prev - m_new)
    p = jnp.exp(s - m_new)
    l_ref[...] = jnp.broadcast_to(alpha * l_ref[:, 0:1] + jnp.sum(p, axis=1, keepdims=True), l_ref.shape)
    m_ref[...] = jnp.broadcast_to(m_new, m_ref.shape)
    pv = lax.dot_general(p.astype(BF16), v_t, (((1,), (1,)), ((), ())), preferred_element_type=F32)
    acc_ref[...] = alpha * acc_ref[...] + pv


def _fox_kernel(qi_ref, kj_ref, first_ref, q_ref, k_ref, v_ref, c_ref, o_ref, q_s, m_s, l_s, acc_s, *, tq, tk):
    p_id = pl.program_id(1)
    qi = qi_ref[p_id]
    kj = kj_ref[p_id]
    scale = HEAD_DIM ** -0.5

    @pl.when(first_ref[p_id] == 1)
    def _():
        for h in range(FOX_HEADS):
            q_s[h] = q_ref[h * 64:(h + 1) * 64, :].T.astype(BF16)
        m_s[...] = jnp.full(m_s.shape, NEG, F32)
        l_s[...] = jnp.zeros(l_s.shape, F32)
        acc_s[...] = jnp.zeros(acc_s.shape, F32)

    row = qi * tq + lax.broadcasted_iota(jnp.int32, (tq, tk), 0)
    col = kj * tk + lax.broadcasted_iota(jnp.int32, (tq, tk), 1)
    mask = col <= row
    for h in range(FOX_HEADS):
        k_t = k_ref[h * 64:(h + 1) * 64, :].astype(BF16)
        v_t = v_ref[h * 64:(h + 1) * 64, :].astype(BF16)
        s = jnp.dot(q_s[h], k_t, preferred_element_type=F32) * scale - c_ref[h:h + 1, :]
        s = jnp.where(mask, s, NEG)
        _online_update(s, v_t, m_s.at[h], l_s.at[h], acc_s.at[h])

    @pl.when(kj == qi)
    def _():
        for h in range(FOX_HEADS):
            o_ref[:, h * 64:(h + 1) * 64] = acc_s[h] / l_s[h][:, 0:1]


def _fox_attention(pt, c, tq):
    nb, _, nt = pt.shape
    tk = tq
    qi, kj, first = _tri_pairs(nt // tq)
    blk = 256
    grid_spec = pltpu.PrefetchScalarGridSpec(
        num_scalar_prefetch=3,
        grid=(nb, qi.shape[0]),
        in_specs=[
            pl.BlockSpec((None, blk, tq), lambda b, p, qi, kj, f: (b, R_FQ // blk, qi[p])),
            pl.BlockSpec((None, blk, tk), lambda b, p, qi, kj, f: (b, R_FK // blk, kj[p])),
            pl.BlockSpec((None, blk, tk), lambda b, p, qi, kj, f: (b, R_FV // blk, kj[p])),
            pl.BlockSpec((None, FOX_HEADS, tk), lambda b, p, qi, kj, f: (b, 0, kj[p])),
        ],
        out_specs=pl.BlockSpec((None, tq, blk), lambda b, p, qi, kj, f: (b, qi[p], 0)),
        scratch_shapes=[
            pltpu.VMEM((FOX_HEADS, tq, HEAD_DIM), BF16),
            pltpu.VMEM((FOX_HEADS, tq, LANES), F32),
            pltpu.VMEM((FOX_HEADS, tq, LANES), F32),
            pltpu.VMEM((FOX_HEADS, tq, HEAD_DIM), F32),
        ],
    )
    return pl.pallas_call(
        functools.partial(_fox_kernel, tq=tq, tk=tk),
        grid_spec=grid_spec,
        out_shape=jax.ShapeDtypeStruct((nb, nt, blk), F32),
        compiler_params=_cparams(("parallel", "arbitrary")),
        name="fox_attn",
    )(qi, kj, first, pt, pt, pt, c)


def _diff_kernel(qi_ref, kj_ref, first_ref, lam_ref, q_ref, kv_ref, g_ref, o_ref, q_s, m_s, l_s, acc_s,
                 *, tq, tk, out_scale):
    p_id = pl.program_id(1)
    qi = qi_ref[p_id]
    kj = kj_ref[p_id]
    scale = DIFF_QK_DIM ** -0.5

    @pl.when(first_ref[p_id] == 1)
    def _():
        for i in range(2 * DIFF_HEADS):
            q_s[i] = q_ref[i * 32:(i + 1) * 32, :].T.astype(BF16)
        m_s[...] = jnp.full(m_s.shape, NEG, F32)
        l_s[...] = jnp.zeros(l_s.shape, F32)
        acc_s[...] = jnp.zeros(acc_s.shape, F32)

    row = qi * tq + lax.broadcasted_iota(jnp.int32, (tq, tk), 0)
    col = kj * tk + lax.broadcasted_iota(jnp.int32, (tq, tk), 1)
    mask = col <= row
    for h in range(DIFF_HEADS):
        v_t = kv_ref[h * 128 + 64:h * 128 + 128, :].astype(BF16)
        for m in range(2):
            i = 2 * h + m
            k_t = kv_ref[h * 128 + m * 32:h * 128 + (m + 1) * 32, :].astype(BF16)
            s = jnp.dot(q_s[i], k_t, preferred_element_type=F32) * scale
            s = jnp.where(mask, s, NEG)
            _online_update(s, v_t, m_s.at[i], l_s.at[i], acc_s.at[i])

    @pl.when(kj == qi)
    def _():
        lam = lam_ref[0]
        for h in range(DIFF_HEADS):
            o1 = acc_s[2 * h] / l_s[2 * h][:, 0:1]
            o2 = acc_s[2 * h + 1] / l_s[2 * h + 1][:, 0:1]
            o = o1 - lam * o2
            o = o * lax.rsqrt(jnp.mean(o * o, axis=-1, keepdims=True) + RMS_EPS) * g_ref[...] * out_scale
            o_ref[:, h * 64:(h + 1) * 64] = o


def _diff_attention(pt, lam, g_row, out_scale, tq):
    nb, _, nt = pt.shape
    tk = tq
    qi, kj, first = _tri_pairs(nt // tq)
    grid_spec = pltpu.PrefetchScalarGridSpec(
        num_scalar_prefetch=4,
        grid=(nb, qi.shape[0]),
        in_specs=[
            pl.BlockSpec((None, 256, tq), lambda b, p, qi, kj, f, lam: (b, R_DQ // 256, qi[p])),
            pl.BlockSpec((None, 512, tk), lambda b, p, qi, kj, f, lam: (b, R_DIFF // 512, kj[p])),
            pl.BlockSpec((1, HEAD_DIM), lambda b, p, qi, kj, f, lam: (0, 0)),
        ],
        out_specs=pl.BlockSpec((None, tq, 256), lambda b, p, qi, kj, f, lam: (b, qi[p], 0)),
        scratch_shapes=[
            pltpu.VMEM((2 * DIFF_HEADS, tq, DIFF_QK_DIM), BF16),
            pltpu.VMEM((2 * DIFF_HEADS, tq, LANES), F32),
            pltpu.VMEM((2 * DIFF_HEADS, tq, LANES), F32),
            pltpu.VMEM((2 * DIFF_HEADS, tq, HEAD_DIM), F32),
        ],
    )
    return pl.pallas_call(
        functools.partial(_diff_kernel, tq=tq, tk=tk, out_scale=out_scale),
        grid_spec=grid_spec,
        out_shape=jax.ShapeDtypeStruct((nb, nt, 256), F32),
        compiler_params=_cparams(("parallel", "arbitrary")),
        name="diff_attn",
    )(qi, kj, first, lam, pt, pt, g_row)


def _compress_kernel(x_ref, pe_ref, w_ref, o_ref):
    x = (x_ref[...] + pe_ref[...]).astype(BF16)
    o_ref[...] = jnp.dot(x, w_ref[...].astype(BF16), preferred_element_type=F32)


def _compress(xb, pe, w, tm):
    _, m, kk = xb.shape
    return pl.pallas_call(
        _compress_kernel,
        grid=(2, m // tm),
        in_specs=[
            pl.BlockSpec((None, tm, kk), lambda t, i: (t, i, 0)),
            pl.BlockSpec((None, 1, kk), lambda t, i: (t, 0, 0)),
            pl.BlockSpec((None, kk, HEAD_DIM), lambda t, i: (t, 0, 0)),
        ],
        out_specs=pl.BlockSpec((None, tm, HEAD_DIM), lambda t, i: (t, i, 0)),
        out_shape=jax.ShapeDtypeStruct((2, m, HEAD_DIM), F32),
        compiler_params=_cparams(("parallel", "parallel")),
        name="nsa_compress",
    )(xb, pe, w)


def _select_mask(imp, qpos, n_blk):
    jb = lax.broadcasted_iota(jnp.int32, imp.shape, 1)
    cur = qpos >> BLOCK_SHIFT
    forced = (jb == 0) | (jb == cur) | (jb == cur - 1)
    score = jnp.where(jb <= cur, jnp.where(forced, FORCE_SCORE, imp), NEG)
    rank = jnp.zeros(imp.shape, F32)
    for j in range(n_blk):
        col = score[:, j:j + 1]
        ahead = (col > score) | ((col == score) & (jb > j))
        rank = rank + jnp.where(ahead, 1.0, 0.0)
    return jnp.where(rank < float(NSA_TOP_N), 1.0, 0.0)


def _nsa_cmp_kernel(q_ref, kc_ref, vc_ref, gate_ref, o_ref, sel_ref, *, tq, n_cmp):
    qi = pl.program_id(2)
    g = pl.program_id(1)
    scale = HEAD_DIM ** -0.5
    kc = kc_ref[...].astype(BF16)
    vc = vc_ref[...].astype(BF16)
    qpos = qi * tq + lax.broadcasted_iota(jnp.int32, (tq, 1), 0)
    nb_iota = lax.broadcasted_iota(jnp.int32, (tq, n_cmp), 1)
    mask = nb_iota < ((qpos + 1) >> BLOCK_SHIFT)
    maskf = jnp.where(mask, 1.0, 0.0)
    imp = jnp.zeros((tq, n_cmp), F32)
    gates = gate_ref[...]
    for h in range(NSA_HPG):
        qh = q_ref[h * 64:(h + 1) * 64, :].T.astype(BF16)
        lc = lax.dot_general(qh, kc, (((1,), (1,)), ((), ())), preferred_element_type=F32) * scale
        lc = jnp.where(mask, lc, NEG)
        m = jnp.max(lc, axis=1, keepdims=True)
        p = jnp.exp(lc - m) * maskf
        pc = p / jnp.maximum(jnp.sum(p, axis=1, keepdims=True), 1e-30)
        imp = imp + pc
        oc = jnp.dot(pc.astype(BF16), vc, preferred_element_type=F32)
        gcol = jnp.where(g == 0, gates[:, h:h + 1], gates[:, NSA_HPG + h:NSA_HPG + h + 1])
        o_ref[:, h * 64:(h + 1) * 64] = oc * gcol
    sel_ref[...] = _select_mask(imp, qpos, n_cmp)


def _nsa_cmp(pt, gates, kc, vc, tq):
    nb, _, nt = pt.shape
    n_cmp = kc.shape[2]
    return pl.pallas_call(
        functools.partial(_nsa_cmp_kernel, tq=tq, n_cmp=n_cmp),
        grid=(nb, NSA_KV_GROUPS, nt // tq),
        in_specs=[
            pl.BlockSpec((None, 256, tq), lambda b, g, i: (b, R_NQ // 256 + g, i)),
            pl.BlockSpec((None, None, n_cmp, HEAD_DIM), lambda b, g, i: (b, g, 0, 0)),
            pl.BlockSpec((None, None, n_cmp, HEAD_DIM), lambda b, g, i: (b, g, 0, 0)),
            pl.BlockSpec((None, tq, LANES), lambda b, g, i: (b, i, 0)),
        ],
        out_specs=[
            pl.BlockSpec((None, tq, 256), lambda b, g, i: (b, i, g)),
            pl.BlockSpec((None, None, tq, n_cmp), lambda b, g, i: (b, g, i, 0)),
        ],
        out_shape=[jax.ShapeDtypeStruct((nb, nt, 512), F32),
                   jax.ShapeDtypeStruct((nb, NSA_KV_GROUPS, nt, n_cmp), F32)],
        compiler_params=_cparams(("parallel", "parallel", "parallel")),
        name="nsa_cmp",
    )(pt, kc, vc, gates)


def _nsa_band_kernel(qi_ref, kj_ref, first_ref, last_ref, q_ref, k_ref, v_ref, gate_ref, *rest,
                     tq, tk, selected, gate_base):
    if selected:
        sel_ref, o_ref, q_s, m_s, l_s, acc_s = rest
    else:
        o_ref, q_s, m_s, l_s, acc_s = rest
    p_id = pl.program_id(2)
    g = pl.program_id(1)
    qi = qi_ref[p_id]
    kj = kj_ref[p_id]
    scale = HEAD_DIM ** -0.5

    @pl.when(first_ref[p_id] == 1)
    def _():
        for h in range(NSA_HPG):
            q_s[h * tq:(h + 1) * tq, :] = q_ref[h * 64:(h + 1) * 64, :].T.astype(BF16)
        m_s[...] = jnp.full(m_s.shape, NEG, F32)
        l_s[...] = jnp.zeros(l_s.shape, F32)
        acc_s[...] = jnp.zeros(acc_s.shape, F32)

    row = qi * tq + lax.broadcasted_iota(jnp.int32, (tq, tk), 0)
    col = kj * tk + lax.broadcasted_iota(jnp.int32, (tq, tk), 1)
    mask = col <= row
    if selected:
        n_blk = sel_ref.shape[-1]
        blk_of_col = kj * (tk // NSA_BLOCK) + (lax.broadcasted_iota(jnp.int32, (n_blk, tk), 1) >> BLOCK_SHIFT)
        expand = jnp.where(lax.broadcasted_iota(jnp.int32, (n_blk, tk), 0) == blk_of_col, 1.0, 0.0).astype(BF16)
        picked = jnp.dot(sel_ref[...].astype(BF16), expand, preferred_element_type=F32)
        mask = mask & (picked > 0.5)
    else:
        mask = mask & (col > row - NSA_WINDOW)
    keep = jnp.where(mask, 1.0, 0.0)
    k_t = k_ref[...].astype(BF16)
    v_t = v_ref[...].astype(BF16)
    s = jnp.dot(q_s[...], k_t, preferred_element_type=F32) * scale
    s = jnp.where(jnp.concatenate([keep] * NSA_HPG, axis=0) > 0.5, s, NEG)
    _online_update(s, v_t, m_s, l_s, acc_s)

    @pl.when(last_ref[p_id] == 1)
    def _():
        gates = gate_ref[...]
        for h in range(NSA_HPG):
            c0 = gate_base + h
            gcol = jnp.where(g == 0, gates[:, c0:c0 + 1], gates[:, c0 + NSA_HPG:c0 + NSA_HPG + 1])
            o_ref[:, h * 64:(h + 1) * 64] = acc_s[h * tq:(h + 1) * tq, :] / l_s[h * tq:(h + 1) * tq, 0:1] * gcol


def _nsa_band(pt, gates, sel, tq, selected):
    nb, _, nt = pt.shape
    tk = tq
    nq = nt // tq
    qi, kj, first = _tri_pairs(nq, None if selected else -(-NSA_WINDOW // tk))
    last = jnp.concatenate([first[1:], jnp.ones((1,), jnp.int32)])
    if selected:
        rk, rv, gate_base = R_NSA + 256, R_NSA + 384, 8
    else:
        rk, rv, gate_base = R_WIN, R_WIN + 128, 16
    idx = lambda f: (lambda b, g, p, qi, kj, fi, la: f(b, g, p, qi, kj))
    in_specs = [
        pl.BlockSpec((None, 256, tq), idx(lambda b, g, p, qi, kj: (b, R_NQ // 256 + g, qi[p]))),
        pl.BlockSpec((None, HEAD_DIM, tk), idx(lambda b, g, p, qi, kj: (b, rk // 64 + g, kj[p]))),
        pl.BlockSpec((None, HEAD_DIM, tk), idx(lambda b, g, p, qi, kj: (b, rv // 64 + g, kj[p]))),
        pl.BlockSpec((None, tq, LANES), idx(lambda b, g, p, qi, kj: (b, qi[p], 0))),
    ]
    args = [pt, pt, pt, gates]
    if selected:
        in_specs.append(pl.BlockSpec((None, None, tq, sel.shape[-1]),
                                     idx(lambda b, g, p, qi, kj: (b, g, qi[p], 0))))
        args.append(sel)
    grid_spec = pltpu.PrefetchScalarGridSpec(
        num_scalar_prefetch=4,
        grid=(nb, NSA_KV_GROUPS, qi.shape[0]),
        in_specs=in_specs,
        out_specs=pl.BlockSpec((None, tq, 256), idx(lambda b, g, p, qi, kj: (b, qi[p], g))),
        scratch_shapes=[
            pltpu.VMEM((NSA_HPG * tq, HEAD_DIM), BF16),
            pltpu.VMEM((NSA_HPG * tq, LANES), F32),
            pltpu.VMEM((NSA_HPG * tq, LANES), F32),
            pltpu.VMEM((NSA_HPG * tq, HEAD_DIM), F32),
        ],
    )
    return pl.pallas_call(
        functools.partial(_nsa_band_kernel, tq=tq, tk=tk, selected=selected, gate_base=gate_base),
        grid_spec=grid_spec,
        out_shape=jax.ShapeDtypeStruct((nb, nt, 512), F32),
        compiler_params=_cparams(("parallel", "parallel", "arbitrary")),
        name="nsa_sel" if selected else "nsa_win",
    )(qi, kj, first, last, *args)


def _flash_step(s, v_t, m_ref, l_ref, acc_ref):
    m_prev = m_ref[...]
    m_new = jnp.maximum(m_prev, jnp.max(s, axis=0, keepdims=True))
    alpha = jnp.exp(m_prev - m_new)
    p = jnp.exp(s - m_new)
    l_ref[...] = alpha * l_ref[...] + jnp.sum(p, axis=0, keepdims=True)
    m_ref[...] = m_new
    acc_ref[...] = alpha * acc_ref[...] + jnp.dot(v_t, p.astype(BF16), preferred_element_type=F32)


def _kq(k_t, q_t):
    return lax.dot_general(k_t, q_t, (((0,), (0,)), ((), ())), preferred_element_type=F32)


def _causal_keep(qi, kj, tq, tk):
    key = kj * tk + lax.broadcasted_iota(jnp.int32, (tk, tq), 0)
    qry = qi * tq + lax.broadcasted_iota(jnp.int32, (tk, tq), 1)
    return key, qry


def _split3(x):
    x1 = x.astype(BF16)
    r1 = x - x1.astype(F32)
    x2 = r1.astype(BF16)
    x3 = (r1 - x2.astype(F32)).astype(BF16)
    return x1, x2, x3


def _fox_kernel_t(qi_ref, kj_ref, first_ref, q_ref, k_ref, v_ref, c_ref, o_ref, q_s, m_s, l_s, acc_s, *, tq, tk):
    p_id = pl.program_id(1)
    qi = qi_ref[p_id]
    kj = kj_ref[p_id]

    @pl.when(first_ref[p_id] == 1)
    def _():
        extra = jnp.where(lax.broadcasted_iota(jnp.int32, (16, tq), 0) < 3, -1.0, 0.0).astype(BF16)
        for h in range(FOX_HEADS):
            q_s[h, 0:64, :] = (q_ref[h * 64:(h + 1) * 64, :] * (HEAD_DIM ** -0.5)).astype(BF16)
            q_s[h, 64:80, :] = extra
        m_s[...] = jnp.full(m_s.shape, NEG, F32)
        l_s[...] = jnp.zeros(l_s.shape, F32)
        acc_s[...] = jnp.zeros(acc_s.shape, F32)

    key, qry = _causal_keep(qi, kj, tq, tk)
    mask = key <= qry
    sub16 = lax.broadcasted_iota(jnp.int32, (16, tk), 0)
    for h in range(FOX_HEADS):
        c1, c2, c3 = [jnp.broadcast_to(t.astype(F32), (16, tk)) for t in _split3(c_ref[h:h + 1, :])]
        extra = jnp.where(sub16 == 0, c1, jnp.where(sub16 == 1, c2, jnp.where(sub16 == 2, c3, 0.0)))
        k_aug = jnp.concatenate([k_ref[h * 64:(h + 1) * 64, :].astype(BF16), extra.astype(BF16)], axis=0)
        s = jnp.where(mask, _kq(k_aug, q_s[h]), NEG)
        _flash_step(s, v_ref[h * 64:(h + 1) * 64, :].astype(BF16), m_s.at[h], l_s.at[h], acc_s.at[h])

    @pl.when(kj == qi)
    def _():
        for h in range(FOX_HEADS):
            o_ref[:, h * 64:(h + 1) * 64] = (acc_s[h] / l_s[h]).T


def _fox_attention_t(pt, c, tq):
    nb, _, nt = pt.shape
    tk = tq
    qi, kj, first = _tri_pairs(nt // tq)
    blk = 256
    grid_spec = pltpu.PrefetchScalarGridSpec(
        num_scalar_prefetch=3,
        grid=(nb, qi.shape[0]),
        in_specs=[
            pl.BlockSpec((None, blk, tq), lambda b, p, qi, kj, f: (b, R_FQ // blk, qi[p])),
            pl.BlockSpec((None, blk, tk), lambda b, p, qi, kj, f: (b, R_FK // blk, kj[p])),
            pl.BlockSpec((None, blk, tk), lambda b, p, qi, kj, f: (b, R_FV // blk, kj[p])),
            pl.BlockSpec((None, FOX_HEADS, tk), lambda b, p, qi, kj, f: (b, 0, kj[p])),
        ],
        out_specs=pl.BlockSpec((None, tq, blk), lambda b, p, qi, kj, f: (b, qi[p], 0)),
        scratch_shapes=[
            pltpu.VMEM((FOX_HEADS, HEAD_DIM + 16, tq), BF16),
            pltpu.VMEM((FOX_HEADS, 1, tq), F32),
            pltpu.VMEM((FOX_HEADS, 1, tq), F32),
            pltpu.VMEM((FOX_HEADS, HEAD_DIM, tq), F32),
        ],
    )
    return pl.pallas_call(
        functools.partial(_fox_kernel_t, tq=tq, tk=tk),
        grid_spec=grid_spec,
        out_shape=jax.ShapeDtypeStruct((nb, nt, blk), F32),
        compiler_params=_cparams(("parallel", "arbitrary")),
        name="fox_attn",
    )(qi, kj, first, pt, pt, pt, c)


def _diff_kernel_t(qi_ref, kj_ref, first_ref, lam_ref, q_ref, kv_ref, g_ref, o_ref, q_s, m_s, l_s, acc_s,
                   *, tq, tk, out_scale):
    p_id = pl.program_id(1)
    qi = qi_ref[p_id]
    kj = kj_ref[p_id]
    scale = DIFF_QK_DIM ** -0.5

    @pl.when(first_ref[p_id] == 1)
    def _():
        q_s[...] = q_ref[...].astype(BF16)
        m_s[...] = jnp.full(m_s.shape, NEG, F32)
        l_s[...] = jnp.zeros(l_s.shape, F32)
        acc_s[...] = jnp.zeros(acc_s.shape, F32)

    key, qry = _causal_keep(qi, kj, tq, tk)
    mask = key <= qry
    for h in range(DIFF_HEADS):
        v_t = kv_ref[h * 128 + 64:h * 128 + 128, :].astype(BF16)
        for m in range(2):
            i = 2 * h + m
            k_t = kv_ref[h * 128 + m * 32:h * 128 + (m + 1) * 32, :].astype(BF16)
            s = jnp.where(mask, _kq(k_t, q_s[i * 32:(i + 1) * 32, :]) * scale, NEG)
            _flash_step(s, v_t, m_s.at[i], l_s.at[i], acc_s.at[i])

    @pl.when(kj == qi)
    def _():
        lam = lam_ref[0]
        for h in range(DIFF_HEADS):
            o = acc_s[2 * h] / l_s[2 * h] - lam * (acc_s[2 * h + 1] / l_s[2 * h + 1])
            o = o * lax.rsqrt(jnp.mean(o * o, axis=0, keepdims=True) + RMS_EPS) * g_ref[...] * out_scale
            o_ref[:, h * 64:(h + 1) * 64] = o.T


def _diff_attention_t(pt, lam, g_col, out_scale, tq):
    nb, _, nt = pt.shape
    tk = tq
    qi, kj, first = _tri_pairs(nt // tq)
    grid_spec = pltpu.PrefetchScalarGridSpec(
        num_scalar_prefetch=4,
        grid=(nb, qi.shape[0]),
        in_specs=[
            pl.BlockSpec((None, 256, tq), lambda b, p, qi, kj, f, lam: (b, R_DQ // 256, qi[p])),
            pl.BlockSpec((None, 512, tk), lambda b, p, qi, kj, f, lam: (b, R_DIFF // 512, kj[p])),
            pl.BlockSpec((HEAD_DIM, 1), lambda b, p, qi, kj, f, lam: (0, 0)),
        ],
        out_specs=pl.BlockSpec((None, tq, 256), lambda b, p, qi, kj, f, lam: (b, qi[p], 0)),
        scratch_shapes=[
            pltpu.VMEM((256, tq), BF16),
            pltpu.VMEM((2 * DIFF_HEADS, 1, tq), F32),
            pltpu.VMEM((2 * DIFF_HEADS, 1, tq), F32),
            pltpu.VMEM((2 * DIFF_HEADS, HEAD_DIM, tq), F32),
        ],
    )
    return pl.pallas_call(
        functools.partial(_diff_kernel_t, tq=tq, tk=tk, out_scale=out_scale),
        grid_spec=grid_spec,
        out_shape=jax.ShapeDtypeStruct((nb, nt, 256), F32),
        compiler_params=_cparams(("parallel", "arbitrary")),
        name="diff_attn",
    )(qi, kj, first, lam, pt, pt, g_col)


def _stack_heads(q_ref, tq):
    return jnp.concatenate([q_ref[h * 64:(h + 1) * 64, :] for h in range(NSA_HPG)], axis=1)


def _select_mask_t(imp, qpos, n_blk):
    jb = lax.broadcasted_iota(jnp.int32, imp.shape, 0)
    cur = qpos >> BLOCK_SHIFT
    forced = (jb == 0) | (jb == cur) | (jb == cur - 1)
    score = jnp.where(jb <= cur, jnp.where(forced, FORCE_SCORE, imp), NEG)
    rank = jnp.zeros(imp.shape, F32)
    for j in range(n_blk):
        row = score[j:j + 1, :]
        ahead = (row > score) | ((row == score) & (jb > j))
        rank = rank + jnp.where(ahead, 1.0, 0.0)
    return jnp.where(rank < float(NSA_TOP_N), 1.0, 0.0)


def _nsa_cmp_kernel_t(q_ref, kc_ref, vct_ref, gate_ref, o_ref, sel_ref, *, tq, n_cmp):
    qi = pl.program_id(2)
    g = pl.program_id(1)
    q_t = (_stack_heads(q_ref, tq) * (HEAD_DIM ** -0.5)).astype(BF16)
    lc = jnp.dot(kc_ref[...].astype(BF16), q_t, preferred_element_type=F32)
    qpos = qi * tq + lax.broadcasted_iota(jnp.int32, (1, tq), 1)
    blk = lax.broadcasted_iota(jnp.int32, (n_cmp, tq), 0)
    keep = jnp.where(blk < ((qpos + 1) >> BLOCK_SHIFT), 1.0, 0.0)
    keep4 = jnp.concatenate([keep] * NSA_HPG, axis=1)
    lc = jnp.where(keep4 > 0.5, lc, NEG)
    p = jnp.exp(lc - jnp.max(lc, axis=0, keepdims=True)) * keep4
    pc = p / jnp.maximum(jnp.sum(p, axis=0, keepdims=True), 1e-30)
    oc = jnp.dot(vct_ref[...].astype(BF16), pc.astype(BF16), preferred_element_type=F32)
    imp = pc[:, 0:tq]
    for h in range(1, NSA_HPG):
        imp = imp + pc[:, h * tq:(h + 1) * tq]
    sel_ref[...] = _select_mask_t(imp, qpos, n_cmp)
    gates = gate_ref[...]
    for h in range(NSA_HPG):
        grow = jnp.where(g == 0, gates[h:h + 1, :], gates[NSA_HPG + h:NSA_HPG + h + 1, :])
        o_ref[:, h * 64:(h + 1) * 64] = (oc[:, h * tq:(h + 1) * tq] * grow).T


def _nsa_cmp_t(pt, kc, vct, tq):
    nb, _, nt = pt.shape
    n_cmp = kc.shape[2]
    return pl.pallas_call(
        functools.partial(_nsa_cmp_kernel_t, tq=tq, n_cmp=n_cmp),
        grid=(nb, NSA_KV_GROUPS, nt // tq),
        in_specs=[
            pl.BlockSpec((None, 256, tq), lambda b, g, i: (b, R_NQ // 256 + g, i)),
            pl.BlockSpec((None, None, n_cmp, HEAD_DIM), lambda b, g, i: (b, g, 0, 0)),
            pl.BlockSpec((None, None, HEAD_DIM, n_cmp), lambda b, g, i: (b, g, 0, 0)),
            pl.BlockSpec((None, 32, tq), lambda b, g, i: (b, R_GATE // 32, i)),
        ],
        out_specs=[
            pl.BlockSpec((None, tq, 256), lambda b, g, i: (b, i, g)),
            pl.BlockSpec((None, None, n_cmp, tq), lambda b, g, i: (b, g, 0, i)),
        ],
        out_shape=[jax.ShapeDtypeStruct((nb, nt, 512), F32),
                   jax.ShapeDtypeStruct((nb, NSA_KV_GROUPS, n_cmp, nt), F32)],
        compiler_params=_cparams(("parallel", "parallel", "parallel")),
        name="nsa_cmp",
    )(pt, kc, vct, pt)


def _nsa_band_kernel_t(qi_ref, kj_ref, first_ref, last_ref, q_ref, k_ref, v_ref, gate_ref, *rest,
                       tq, tk, selected, gate_base):
    if selected:
        sel_ref, o_ref, q_s, m_s, l_s, acc_s = rest
    else:
        o_ref, q_s, m_s, l_s, acc_s = rest
    p_id = pl.program_id(2)
    g = pl.program_id(1)
    qi = qi_ref[p_id]
    kj = kj_ref[p_id]

    @pl.when(first_ref[p_id] == 1)
    def _():
        q_s[...] = (_stack_heads(q_ref, tq) * (HEAD_DIM ** -0.5)).astype(BF16)
        m_s[...] = jnp.full(m_s.shape, NEG, F32)
        l_s[...] = jnp.zeros(l_s.shape, F32)
        acc_s[...] = jnp.zeros(acc_s.shape, F32)

    key, qry = _causal_keep(qi, kj, tq, tk)
    mask = key <= qry
    if selected:
        n_blk = sel_ref.shape[0]
        blk_of_key = kj * (tk // NSA_BLOCK) + (lax.broadcasted_iota(jnp.int32, (tk, n_blk), 0) >> BLOCK_SHIFT)
        expand = jnp.where(lax.broadcasted_iota(jnp.int32, (tk, n_blk), 1) == blk_of_key, 1.0, 0.0).astype(BF16)
        picked = jnp.dot(expand, sel_ref[...].astype(BF16), preferred_element_type=F32)
        mask = mask & (picked > 0.5)
    else:
        mask = mask & (key > qry - NSA_WINDOW)
    keep = jnp.where(mask, 1.0, 0.0)
    s = _kq(k_ref[...].astype(BF16), q_s[...])
    s = jnp.where(jnp.concatenate([keep] * NSA_HPG, axis=1) > 0.5, s, NEG)
    _flash_step(s, v_ref[...].astype(BF16), m_s, l_s, acc_s)

    @pl.when(last_ref[p_id] == 1)
    def _():
        gates = gate_ref[...]
        o = acc_s[...] / l_s[...]
        for h in range(NSA_HPG):
            r0 = gate_base + h
            grow = jnp.where(g == 0, gates[r0:r0 + 1, :], gates[r0 + NSA_HPG:r0 + NSA_HPG + 1, :])
            o_ref[:, h * 64:(h + 1) * 64] = (o[:, h * tq:(h + 1) * tq] * grow).T


def _nsa_band_t(pt, sel, tq, selected):
    nb, _, nt = pt.shape
    tk = tq
    nq = nt // tq
    qi, kj, first = _tri_pairs(nq, None if selected else -(-NSA_WINDOW // tk))
    last = jnp.concatenate([first[1:], jnp.ones((1,), jnp.int32)])
    if selected:
        rk, rv, gate_base = R_NSA + 256, R_NSA + 384, 8
    else:
        rk, rv, gate_base = R_WIN, R_WIN + 128, 16
    idx = lambda f: (lambda b, g, p, qi, kj, fi, la: f(b, g, p, qi, kj))
    in_specs = [
        pl.BlockSpec((None, 256, tq), idx(lambda b, g, p, qi, kj: (b, R_NQ // 256 + g, qi[p]))),
        pl.BlockSpec((None, HEAD_DIM, tk), idx(lambda b, g, p, qi, kj: (b, rk // 64 + g, kj[p]))),
        pl.BlockSpec((None, HEAD_DIM, tk), idx(lambda b, g, p, qi, kj: (b, rv // 64 + g, kj[p]))),
        pl.BlockSpec((None, 32, tq), idx(lambda b, g, p, qi, kj: (b, R_GATE // 32, qi[p]))),
    ]
    args = [pt, pt, pt, pt]
    if selected:
        in_specs.append(pl.BlockSpec((None, None, sel.shape[2], tq),
                                     idx(lambda b, g, p, qi, kj: (b, g, 0, qi[p]))))
        args.append(sel)
    grid_spec = pltpu.PrefetchScalarGridSpec(
        num_scalar_prefetch=4,
        grid=(nb, NSA_KV_GROUPS, qi.shape[0]),
        in_specs=in_specs,
        out_specs=pl.BlockSpec((None, tq, 256), idx(lambda b, g, p, qi, kj: (b, qi[p], g))),
        scratch_shapes=[
            pltpu.VMEM((HEAD_DIM, NSA_HPG * tq), BF16),
            pltpu.VMEM((1, NSA_HPG * tq), F32),
            pltpu.VMEM((1, NSA_HPG * tq), F32),
            pltpu.VMEM((HEAD_DIM, NSA_HPG * tq), F32),
        ],
    )
    return pl.pallas_call(
        functools.partial(_nsa_band_kernel_t, tq=tq, tk=tk, selected=selected, gate_base=gate_base),
        grid_spec=grid_spec,
        out_shape=jax.ShapeDtypeStruct((nb, nt, 512), F32),
        compiler_params=_cparams(("parallel", "parallel", "arbitrary")),
        name="nsa_sel" if selected else "nsa_win",
    )(qi, kj, first, last, *args)


def _layer_norm_rows(x, g, b):
    mu = jnp.mean(x, axis=-1, keepdims=True)
    xc = x - mu
    var = jnp.mean(xc * xc, axis=-1, keepdims=True)
    return xc * lax.rsqrt(var + LN_EPS) * g + b


def _outproj_kernel(fox_ref, n1_ref, n2_ref, n3_ref, diff_ref, h_ref, w_ref, g_ref, b_ref, wr_ref, br_ref,
                    o_ref, lg_ref):
    nsa = (n1_ref[...] + n2_ref[...] + n3_ref[...]).astype(BF16)
    y = jnp.dot(fox_ref[...].astype(BF16), w_ref[0:256, :], preferred_element_type=F32)
    y = y + jnp.dot(nsa, w_ref[256:768, :], preferred_element_type=F32)
    y = y + jnp.dot(diff_ref[...].astype(BF16), w_ref[768:1024, :], preferred_element_type=F32)
    h1 = _layer_norm_rows(DEEPNORM_ALPHA * h_ref[...] + y, g_ref[...], b_ref[...])
    o_ref[...] = h1
    lg_ref[...] = jnp.dot(h1.astype(BF16), wr_ref[...], preferred_element_type=F32) + br_ref[...]


def _outproj(fox, n1, n2, n3, diff, h, w, g, b, wr, br, tm):
    m = h.shape[0]
    row = lambda i: (i, 0)
    const = lambda i: (0, 0)
    return pl.pallas_call(
        _outproj_kernel,
        grid=(m // tm,),
        in_specs=[
            pl.BlockSpec((tm, 256), row), pl.BlockSpec((tm, 512), row), pl.BlockSpec((tm, 512), row),
            pl.BlockSpec((tm, 512), row), pl.BlockSpec((tm, 256), row), pl.BlockSpec((tm, D_MODEL), row),
            pl.BlockSpec((D_MODEL, D_MODEL), const), pl.BlockSpec((1, D_MODEL), const),
            pl.BlockSpec((1, D_MODEL), const), pl.BlockSpec((D_MODEL, LANES), const),
            pl.BlockSpec((1, LANES), const),
        ],
        out_specs=[pl.BlockSpec((tm, D_MODEL), row), pl.BlockSpec((tm, LANES), row)],
        out_shape=[jax.ShapeDtypeStruct((m, D_MODEL), F32), jax.ShapeDtypeStruct((m, LANES), F32)],
        compiler_params=_cparams(("parallel",)),
        name="outproj_ln",
    )(fox, n1, n2, n3, diff, h, w, g, b, wr, br)


def _moe_kernel(te_ref, nv_ref, x_ref, wg_ref, wl_ref, bg_ref, bl_ref, wd_ref, bd_ref, o_ref):
    i = pl.program_id(0)

    @pl.when(i < nv_ref[0])
    def _():
        x = x_ref[...]
        hg = jnp.dot(x, wg_ref[...], preferred_element_type=F32) + bg_ref[...]
        hl = jnp.dot(x, wl_ref[...], preferred_element_type=F32) + bl_ref[...]
        glu = jnp.minimum(hg, SWIGLU_LIMIT)
        lin = jnp.clip(hl, -SWIGLU_LIMIT, SWIGLU_LIMIT)
        act = glu * (1.0 / (1.0 + jnp.exp(-SWIGLU_ALPHA * glu))) * (lin + 1.0)
        o_ref[...] = jnp.dot(act.astype(BF16), wd_ref[...], preferred_element_type=F32) + bd_ref[...]

    @pl.when(i >= nv_ref[0])
    def _():
        o_ref[...] = jnp.zeros(o_ref.shape, F32)


def _moe_experts(tile_e, n_valid, xs, wg, wl, bg, bl, wd, bd, tm):
    n_rows = xs.shape[0]
    ex = lambda i, te, nv: (te[i], 0, 0)
    grid_spec = pltpu.PrefetchScalarGridSpec(
        num_scalar_prefetch=2,
        grid=(n_rows // tm,),
        in_specs=[
            pl.BlockSpec((tm, D_MODEL), lambda i, te, nv: (i, 0)),
            pl.BlockSpec((None, D_MODEL, D_MODEL), ex),
            pl.BlockSpec((None, D_MODEL, D_MODEL), ex),
            pl.BlockSpec((None, 1, D_MODEL), ex),
            pl.BlockSpec((None, 1, D_MODEL), ex),
            pl.BlockSpec((None, D_MODEL, D_MODEL), ex),
            pl.BlockSpec((None, 1, D_MODEL), ex),
        ],
        out_specs=pl.BlockSpec((tm, D_MODEL), lambda i, te, nv: (i, 0)),
    )
    return pl.pallas_call(
        _moe_kernel,
        grid_spec=grid_spec,
        out_shape=jax.ShapeDtypeStruct((n_rows, D_MODEL), F32),
        compiler_params=_cparams(("arbitrary",)),
        name="moe_experts",
    )(tile_e, n_valid, xs, wg, wl, bg, bl, wd, bd)


def _combine_kernel(h_ref, y_ref, gate_ref, g_ref, b_ref, o_ref):
    gates = gate_ref[...]
    f = y_ref[0] * gates[:, 0:1]
    for k in range(1, TOP_K):
        f = f + y_ref[k] * gates[:, k:k + 1]
    o_ref[...] = _layer_norm_rows(DEEPNORM_ALPHA * h_ref[...] + f, g_ref[...], b_ref[...])


def _combine_ln(h1, yg, gates, g, b, tm):
    m = h1.shape[0]
    row = lambda i: (i, 0)
    const = lambda i: (0, 0)
    return pl.pallas_call(
        _combine_kernel,
        grid=(m // tm,),
        in_specs=[
            pl.BlockSpec((tm, D_MODEL), row),
            pl.BlockSpec((TOP_K, tm, D_MODEL), lambda i: (0, i, 0)),
            pl.BlockSpec((tm, TOP_K), row),
            pl.BlockSpec((1, D_MODEL), const), pl.BlockSpec((1, D_MODEL), const),
        ],
        out_specs=pl.BlockSpec((tm, D_MODEL), row),
        out_shape=jax.ShapeDtypeStruct((m, D_MODEL), F32),
        compiler_params=_cparams(("parallel",)),
        name="moe_combine_ln",
    )(h1, yg, gates, g, b)


def _moe_block(h1, logits, wts, ln_g, ln_b, tm_moe, tm_row):
    wg, wl, bg, bl, wd, bd = wts
    m = h1.shape[0]
    top_val, top_idx = lax.top_k(logits, TOP_K)
    gate = jax.nn.softmax(top_val, axis=-1)
    n = m * TOP_K
    flat_e = top_idx.reshape(n)
    order = jnp.argsort(flat_e)
    sorted_e = flat_e[order]
    counts = jnp.bincount(flat_e, length=N_EXPERTS)
    padded = (counts + tm_moe - 1) // tm_moe * tm_moe
    starts = jnp.cumsum(counts) - counts
    pends = jnp.cumsum(padded)
    dest = ((pends - padded)[sorted_e] + jnp.arange(n) - starts[sorted_e]).astype(jnp.int32)
    n_rows = -(-n // tm_moe) * tm_moe + N_EXPERTS * tm_moe
    n_tiles = n_rows // tm_moe
    row_tok = jnp.full((n_rows,), m, jnp.int32).at[dest].set((order // TOP_K).astype(jnp.int32))
    tile_e = jnp.minimum(jnp.searchsorted(pends, jnp.arange(n_tiles) * tm_moe, side='right'),
                         N_EXPERTS - 1).astype(jnp.int32)
    n_valid = (pends[-1] // tm_moe).astype(jnp.int32).reshape(1)
    pos = jnp.zeros((n,), jnp.int32).at[order].set(dest).reshape(m, TOP_K)
    xs = jnp.concatenate([h1.astype(BF16), jnp.zeros((1, D_MODEL), BF16)], 0)[row_tok]
    out = _moe_experts(tile_e, n_valid, xs, wg, wl, bg, bl, wd, bd, tm_moe)
    yg = out[pos.T]
    return _combine_ln(h1, yg, gate, ln_g, ln_b, tm_row)


def _sublane_allsum(x):
    x = x + pltpu.roll(x, 4, 0)
    x = x + pltpu.roll(x, 2, 0)
    return x + pltpu.roll(x, 1, 0)


def _lane_suffix_scan(x):
    lane = lax.broadcasted_iota(jnp.int32, x.shape, 1)
    k = 1
    while k < LANES:
        x = x + jnp.where(lane + k < LANES, pltpu.roll(x, LANES - k, 1), 0.0)
        k *= 2
    return x


def _stream_update(logits, m_ref, l_ref, valid=None):
    m_prev = m_ref[...]
    m_new = jnp.maximum(m_prev, logits)
    alpha = jnp.exp(m_prev - m_new)
    p = jnp.exp(logits - m_new)
    if valid is not None:
        p = jnp.where(valid, p, 0.0)
    l_ref[...] = l_ref[...] * alpha + p
    m_ref[...] = m_new
    return alpha, p


def _lane_merge(m_ref, l_ref):
    m = m_ref[...]
    w = jnp.exp(m - jnp.max(m, axis=1, keepdims=True))
    tot = jnp.sum(l_ref[...] * w, axis=1, keepdims=True)
    return w, 1.0 / tot


FOX_WIN = 520
FOX_VROW0 = 256


def _fox_sample_kernel(pt_ref, *refs, kpages, r0):
    pages = refs[:2 * kpages]
    qc_ref, new_ref, o_ref, qcb_s, m_s, l_s, s_s, acc_s = refs[2 * kpages:]
    j = pl.program_id(1)
    sub = lax.broadcasted_iota(jnp.int32, (SUBLANES, LANES), 0)
    lane = lax.broadcasted_iota(jnp.int32, (SUBLANES, LANES), 1)
    lf_rows = (sub >= r0) & (sub < r0 + FOX_HEADS)
    n_kv = FOX_VROW0 // SUBLANES + 1

    def page(main, tail, valid):
        t = main(0) * qcb_s[0:8, :]
        for i in range(1, n_kv):
            t = t + main(i) * qcb_s[8 * i:8 * i + 8, :]
        qk = t + pltpu.roll(t, 4, 0)
        lf = jnp.where(lf_rows, tail, 0.0)
        lf = lf + pltpu.roll(lf, 4, 0)
        if valid is not None:
            lf = jnp.where(valid, lf, 0.0)
        incl = _lane_suffix_scan(lf)
        logits = qk + s_s[...] + (incl - lf)
        if valid is not None:
            logits = jnp.where(valid, logits, NEG)
        alpha, p = _stream_update(logits, m_s, l_s, valid)
        n_main = (FOX_WIN - SUBLANES - FOX_VROW0) // SUBLANES
        for i in range(n_main):
            r = 8 * i
            acc_s[r:r + 8, :] = acc_s[r:r + 8, :] * alpha + main(FOX_VROW0 // SUBLANES + i) * p
        r = 8 * n_main
        acc_s[r:r + 8, :] = acc_s[r:r + 8, :] * alpha + tail * p
        s_s[...] = s_s[...] + incl[:, 0:1]

    @pl.when(j == 0)
    def _():
        qcb_s[...] = jnp.broadcast_to(qc_ref[...], qcb_s.shape)
        m_s[...] = jnp.full(m_s.shape, NEG, F32)
        l_s[...] = jnp.zeros(l_s.shape, F32)
        s_s[...] = jnp.zeros(s_s.shape, F32)
        acc_s[...] = jnp.zeros(acc_s.shape, F32)
        page(lambda i: jnp.broadcast_to(new_ref[8 * i:8 * i + 8, :], (SUBLANES, LANES)),
             jnp.broadcast_to(new_ref[FOX_WIN - 8:FOX_WIN, :], (SUBLANES, LANES)), lane == 0)

    for k in range(kpages):
        main_ref, tail_ref = pages[2 * k], pages[2 * k + 1]
        page(lambda i, ref=main_ref: ref[8 * i:8 * i + 8, :], tail_ref[...], None)

    @pl.when(j == pl.num_programs(1) - 1)
    def _():
        w, inv = _lane_merge(m_s, l_s)
        for i in range((FOX_WIN - FOX_VROW0) // SUBLANES):
            r = 8 * i
            o_ref[r:r + 8, :] = jnp.sum(acc_s[r:r + 8, :] * w, axis=1, keepdims=True) * inv


def _fox_sample_attn(cache_fm, page_table, layer, qcol, newcol, kpages):
    nb, n_pages = page_table.shape
    r0 = 4 * layer
    n_chunks = n_pages // kpages

    def page_idx(b, j, pt, k):
        return pt[b, n_pages - 1 - (j * kpages + k)]

    in_specs = []
    for k in range(kpages):
        in_specs.append(pl.BlockSpec((None, 512, LANES), lambda b, j, pt, k=k: (page_idx(b, j, pt, k), layer, 0)))
        in_specs.append(pl.BlockSpec((None, 8, LANES), lambda b, j, pt, k=k: (page_idx(b, j, pt, k), 64 * (layer + 1), 0)))
    in_specs += [pl.BlockSpec((None, FOX_WIN, 1), lambda b, j, pt: (b, 0, 0)),
                 pl.BlockSpec((None, FOX_WIN, 1), lambda b, j, pt: (b, 0, 0))]
    n_out = FOX_WIN - FOX_VROW0
    grid_spec = pltpu.PrefetchScalarGridSpec(
        num_scalar_prefetch=1,
        grid=(nb, n_chunks),
        in_specs=in_specs,
        out_specs=pl.BlockSpec((None, n_out, 1), lambda b, j, pt: (b, 0, 0)),
        scratch_shapes=[
            pltpu.VMEM((FOX_WIN, LANES), F32),
            pltpu.VMEM((SUBLANES, LANES), F32), pltpu.VMEM((SUBLANES, LANES), F32), pltpu.VMEM((SUBLANES, LANES), F32),
            pltpu.VMEM((n_out, LANES), F32),
        ],
    )
    return pl.pallas_call(
        functools.partial(_fox_sample_kernel, kpages=kpages, r0=r0),
        grid_spec=grid_spec,
        out_shape=jax.ShapeDtypeStruct((nb, n_out, 1), F32),
        compiler_params=_cparams(("parallel", "arbitrary")),
        name="fox_sample",
    )(page_table, *([cache_fm] * (2 * kpages)), qcol, newcol)


def _bcast_row(x, s):
    return jnp.broadcast_to(x[s:s + 1, :], x.shape)


def _pack_streams(ts):
    sub = lax.broadcasted_iota(jnp.int32, ts[0].shape, 0)
    out = ts[0]
    for s in range(1, len(ts)):
        out = jnp.where(sub == s, ts[s], out)
    return out


def _kv_page_update(x, qcb_s, k_rows, v_rows, m_s, l_s, acc_s, valid):
    ts = []
    for s, (g0, cnt, c0) in enumerate(k_rows):
        t = x(g0) * qcb_s[c0:c0 + 8, :]
        for i in range(1, cnt):
            t = t + x(g0 + i) * qcb_s[c0 + 8 * i:c0 + 8 * i + 8, :]
        ts.append(_sublane_allsum(t))
    logits = _pack_streams(ts)
    if valid is not None:
        logits = jnp.where(valid, logits, NEG)
    alpha, p = _stream_update(logits, m_s, l_s, valid)
    for s, g0 in enumerate(v_rows):
        ab, pb = _bcast_row(alpha, s), _bcast_row(p, s)
        for i in range(8):
            r = s * 64 + 8 * i
            acc_s[r:r + 8, :] = acc_s[r:r + 8, :] * ab + x(g0 + i) * pb


def _merged_rows(acc_s, w, inv, s):
    wb = _bcast_row(w, s)
    return [jnp.sum(acc_s[s * 64 + 8 * i:s * 64 + 8 * i + 8, :] * wb, axis=1, keepdims=True) * inv[s:s + 1, :]
            for i in range(8)]


_DIFF_K_ROWS = tuple((16 * (s // 2) + 4 * (s % 2), 4, 128 * (s // 2) + 32 * (s % 2)) for s in range(8))
_DIFF_V_ROWS = tuple(16 * (s // 2) + 8 for s in range(8))


def _diff_sample_kernel(pt_ref, lam_ref, *refs, kpages, out_scale):
    pages = refs[:kpages]
    qc_ref, new_ref, g_ref, o_ref, qcb_s, m_s, l_s, acc_s = refs[kpages:]
    j = pl.program_id(1)
    lane = lax.broadcasted_iota(jnp.int32, (SUBLANES, LANES), 1)

    @pl.when(j == 0)
    def _():
        qcb_s[...] = jnp.broadcast_to(qc_ref[...], qcb_s.shape)
        m_s[...] = jnp.full(m_s.shape, NEG, F32)
        l_s[...] = jnp.zeros(l_s.shape, F32)
        acc_s[...] = jnp.zeros(acc_s.shape, F32)
        _kv_page_update(lambda i: jnp.broadcast_to(new_ref[8 * i:8 * i + 8, :], (SUBLANES, LANES)),
                        qcb_s, _DIFF_K_ROWS, _DIFF_V_ROWS, m_s, l_s, acc_s, lane == 0)

    for k in range(kpages):
        _kv_page_update(lambda i, ref=pages[k]: ref[8 * i:8 * i + 8, :],
                        qcb_s, _DIFF_K_ROWS, _DIFF_V_ROWS, m_s, l_s, acc_s, None)

    @pl.when(j == pl.num_programs(1) - 1)
    def _():
        w, inv = _lane_merge(m_s, l_s)
        lam = lam_ref[0]
        for h in range(DIFF_HEADS):
            o1 = _merged_rows(acc_s, w, inv, 2 * h)
            o2 = _merged_rows(acc_s, w, inv, 2 * h + 1)
            o = [a - lam * b for a, b in zip(o1, o2)]
            ss = o[0] * o[0]
            for i in range(1, 8):
                ss = ss + o[i] * o[i]
            r = lax.rsqrt(_sublane_allsum(ss) * (1.0 / HEAD_DIM) + RMS_EPS)
            for i in range(8):
                o_ref[h * 64 + 8 * i:h * 64 + 8 * i + 8, :] = o[i] * r * g_ref[8 * i:8 * i + 8, :] * out_scale


def _diff_sample_attn(cache_fm, page_table, layer, lam, qcol, newcol, gcol, out_scale, kpages):
    nb, n_pages = page_table.shape
    in_specs = [pl.BlockSpec((None, 512, LANES), lambda b, j, pt, lam, k=k: (pt[b, j * kpages + k], layer, 0))
                for k in range(kpages)]
    in_specs += [pl.BlockSpec((None, 512, 1), lambda b, j, pt, lam: (b, 0, 0)),
                 pl.BlockSpec((None, 512, 1), lambda b, j, pt, lam: (b, 0, 0)),
                 pl.BlockSpec((HEAD_DIM, 1), lambda b, j, pt, lam: (0, 0))]
    grid_spec = pltpu.PrefetchScalarGridSpec(
        num_scalar_prefetch=2,
        grid=(nb, n_pages // kpages),
        in_specs=in_specs,
        out_specs=pl.BlockSpec((None, 256, 1), lambda b, j, pt, lam: (b, 0, 0)),
        scratch_shapes=[pltpu.VMEM((512, LANES), F32), pltpu.VMEM((SUBLANES, LANES), F32),
                        pltpu.VMEM((SUBLANES, LANES), F32), pltpu.VMEM((512, LANES), F32)],
    )
    return pl.pallas_call(
        functools.partial(_diff_sample_kernel, kpages=kpages, out_scale=out_scale),
        grid_spec=grid_spec,
        out_shape=jax.ShapeDtypeStruct((nb, 256, 1), F32),
        compiler_params=_cparams(("parallel", "arbitrary")),
        name="diff_sample",
    )(page_table, lam, *([cache_fm] * kpages), qcol, newcol, gcol)


def _nsa_compress_sample_kernel(pt_ref, *refs, kpages, n_pages):
    pages = refs[:kpages]
    w2_ref, pe2_ref, o_ref, buf_s = refs[kpages:]
    j = pl.program_id(1)
    for k in range(kpages):
        start = pl.multiple_of((j * kpages + k) * 256, 256)
        buf_s[pl.ds(start, 256), :] = pages[k][...]

    @pl.when(j == pl.num_programs(1) - 1)
    def _():
        for t in range(2):
            acc = jnp.zeros((2 * n_pages, LANES), F32)
            for d in range(HEAD_DIM):
                x = jnp.concatenate(
                    [buf_s[pl.ds(t * 128 + g * 64 + d, n_pages, stride=256), :] for g in range(NSA_KV_GROUPS)], axis=0)
                x = (x + pe2_ref[t, d:d + 1, :]).astype(BF16)
                acc = acc + jnp.dot(x, w2_ref[t, d], preferred_element_type=F32)
            o_ref[t] = acc


def _nsa_compress_sample(cache_fm, page_table, layer, w2, pe2, kpages):
    nb, n_pages = page_table.shape
    in_specs = [pl.BlockSpec((None, 256, LANES), lambda b, j, pt, k=k: (pt[b, j * kpages + k], 2 * layer, 0))
                for k in range(kpages)]
    in_specs += [pl.BlockSpec((2, HEAD_DIM, LANES, LANES), lambda b, j, pt: (0, 0, 0, 0)),
                 pl.BlockSpec((2, HEAD_DIM, LANES), lambda b, j, pt: (0, 0, 0))]
    grid_spec = pltpu.PrefetchScalarGridSpec(
        num_scalar_prefetch=1,
        grid=(nb, n_pages // kpages),
        in_specs=in_specs,
        out_specs=pl.BlockSpec((None, 2, 2 * n_pages, LANES), lambda b, j, pt: (b, 0, 0, 0)),
        scratch_shapes=[pltpu.VMEM((n_pages * 256, LANES), F32)],
    )
    return pl.pallas_call(
        functools.partial(_nsa_compress_sample_kernel, kpages=kpages, n_pages=n_pages),
        grid_spec=grid_spec,
        out_shape=jax.ShapeDtypeStruct((nb, 2, 2 * n_pages, LANES), F32),
        compiler_params=_cparams(("parallel", "arbitrary")),
        name="nsa_compress_sample",
    )(page_table, *([cache_fm] * kpages), w2, pe2)


_NSA_K_ROWS = tuple((8 * (s // NSA_HPG), 8, 64 * s) for s in range(NSA_HEADS))
_NSA_V_ROWS = tuple(16 + 8 * (s // NSA_HPG) for s in range(NSA_HEADS))


def _col_to_row(col):
    n = col.shape[0]
    eye = lax.broadcasted_iota(jnp.int32, (n, n), 0) == lax.broadcasted_iota(jnp.int32, (n, n), 1)
    return jnp.sum(jnp.where(eye, col, 0.0), axis=0, keepdims=True)


def _nsa_sample_kernel(pt_ref, *refs, kpages, n_pages):
    pages = refs[:kpages]
    (kv_ref, qh_ref, qc_ref, newsel_ref, newwin_ref, win_ref, g0_ref, g12_ref,
     oc_ref, osw_ref, wout_ref,
     qcb_s, sel_s, m_s, l_s, acc_s, mw_s, lw_s, accw_s) = refs[kpages:]
    j = pl.program_id(1)
    lane = lax.broadcasted_iota(jnp.int32, (SUBLANES, LANES), 1)
    scale = HEAD_DIM ** -0.5
    n_blk = 2 * n_pages

    @pl.when(j == 0)
    def _():
        qcb_s[...] = jnp.broadcast_to(qc_ref[...], qcb_s.shape)
        for ref in (m_s, mw_s):
            ref[...] = jnp.full(ref.shape, NEG, F32)
        for ref in (l_s, acc_s, lw_s, accw_s):
            ref[...] = jnp.zeros(ref.shape, F32)
        imp = [jnp.zeros((n_pages, 1), F32), jnp.zeros((n_pages, 1), F32)]
        for g in range(NSA_KV_GROUPS):
            kc = kv_ref[0, g * n_pages:(g + 1) * n_pages, :].astype(BF16)
            vc = kv_ref[1, g * n_pages:(g + 1) * n_pages, :].astype(BF16)
            lc = [jnp.dot(kc, qh_ref[g, hf].astype(BF16), preferred_element_type=F32) * scale for hf in range(2)]
            mx = jnp.maximum(jnp.max(lc[0], axis=0, keepdims=True), jnp.max(lc[1], axis=0, keepdims=True))
            pe = [jnp.exp(c - mx) for c in lc]
            den = jnp.maximum(jnp.sum(pe[0], axis=0, keepdims=True) + jnp.sum(pe[1], axis=0, keepdims=True), 1e-30)
            pc = [e / den for e in pe]
            res = [lax.dot_general(c.astype(BF16), vc, (((0,), (0,)), ((), ())), preferred_element_type=F32)
                   for c in pc]
            oc = res[0][:, 0:HEAD_DIM] + res[1][:, HEAD_DIM:2 * HEAD_DIM]
            oc_ref[g * NSA_HPG:(g + 1) * NSA_HPG, :] = oc * g0_ref[g * NSA_HPG:(g + 1) * NSA_HPG, :]
            for hf in range(2):
                imp[hf] = imp[hf] + jnp.sum(pc[hf], axis=1, keepdims=True)
            pidx = lax.broadcasted_iota(jnp.int32, (n_pages, 1), 0)
            score = [jnp.where(pidx == 0, FORCE_SCORE, imp[0]), jnp.where(pidx == n_pages - 1, FORCE_SCORE, imp[1])]
            rows = [_col_to_row(c) for c in score]
            prow = lax.broadcasted_iota(jnp.int32, (n_pages, n_pages), 1)
            pcol = lax.broadcasted_iota(jnp.int32, (n_pages, n_pages), 0)
            sel = []
            for hf in range(2):
                cnt = jnp.zeros((n_pages, 1), F32)
                for hf2 in range(2):
                    before = (2 * prow + hf2) < (2 * pcol + hf)
                    ahead = (rows[hf2] > score[hf]) | ((rows[hf2] == score[hf]) & before)
                    cnt = cnt + jnp.sum(jnp.where(ahead, 1.0, 0.0), axis=1, keepdims=True)
                cnt = cnt + jnp.where(score[hf] < FORCE_SCORE, 1.0, 0.0)
                sel.append(jnp.where(cnt < float(NSA_TOP_N), 1.0, 0.0))
            lane_p = lax.broadcasted_iota(jnp.int32, (n_pages, LANES), 1)
            sel_s[g] = jnp.where(lane_p < NSA_BLOCK, sel[0], sel[1])
            imp = [jnp.zeros((n_pages, 1), F32), jnp.zeros((n_pages, 1), F32)]
        _kv_page_update(lambda i: jnp.broadcast_to(newsel_ref[8 * i:8 * i + 8, :], (SUBLANES, LANES)),
                        qcb_s, _NSA_K_ROWS, _NSA_V_ROWS, m_s, l_s, acc_s, lane == 0)
        _kv_page_update(lambda i: jnp.broadcast_to(newwin_ref[8 * i:8 * i + 8, :], (SUBLANES, LANES)),
                        qcb_s, _NSA_K_ROWS, _NSA_V_ROWS, mw_s, lw_s, accw_s, lane == 0)
        for c in range(NSA_WINDOW // LANES):
            valid = (lane >= 1) if c == 0 else None
            _kv_page_update(lambda i, c=c: win_ref[8 * i:8 * i + 8, c * LANES:(c + 1) * LANES],
                            qcb_s, _NSA_K_ROWS, _NSA_V_ROWS, mw_s, lw_s, accw_s, valid)
        lane_w = lax.broadcasted_iota(jnp.int32, (256, NSA_WINDOW), 1)
        shifted = pltpu.roll(win_ref[...], NSA_WINDOW - 1, 1)
        wout_ref[...] = jnp.where(lane_w == NSA_WINDOW - 1, newwin_ref[...], shifted)

    sub = lax.broadcasted_iota(jnp.int32, (SUBLANES, LANES), 0)
    for k in range(kpages):
        p_idx = j * kpages + k
        picked = jnp.where(sub < NSA_HPG, jnp.broadcast_to(sel_s[0, pl.ds(p_idx, 1), :], (SUBLANES, LANES)),
                           jnp.broadcast_to(sel_s[1, pl.ds(p_idx, 1), :], (SUBLANES, LANES)))
        _kv_page_update(lambda i, ref=pages[k]: ref[8 * i:8 * i + 8, :],
                        qcb_s, _NSA_K_ROWS, _NSA_V_ROWS, m_s, l_s, acc_s, picked > 0.5)

    @pl.when(j == pl.num_programs(1) - 1)
    def _():
        w, inv = _lane_merge(m_s, l_s)
        ww, invw = _lane_merge(mw_s, lw_s)
        for s in range(NSA_HEADS):
            o_sel = _merged_rows(acc_s, w, inv, s)
            o_win = _merged_rows(accw_s, ww, invw, s)
            for i in range(8):
                r = s * 64 + 8 * i
                osw_ref[r:r + 8, :] = o_sel[i] * g12_ref[0, r:r + 8, :] + o_win[i] * g12_ref[1, r:r + 8, :]


def _nsa_sample_attn(cache_fm, win_fm, page_table, layer, kv, qh, qcol, newsel, newwin, g0, g12, kpages):
    nb, n_pages = page_table.shape
    c3 = lambda b, j, pt: (b, 0, 0)
    c4 = lambda b, j, pt: (b, 0, 0, 0)
    in_specs = [pl.BlockSpec((None, 256, LANES), lambda b, j, pt, k=k: (pt[b, j * kpages + k], 2 * layer + 1, 0))
                for k in range(kpages)]
    in_specs += [
        pl.BlockSpec((None, 2, 2 * n_pages, LANES), c4),
        pl.BlockSpec((None, NSA_KV_GROUPS, 2, LANES, NSA_HPG), lambda b, j, pt: (b, 0, 0, 0, 0)),
        pl.BlockSpec((None, 512, 1), c3),
        pl.BlockSpec((None, 256, 1), c3),
        pl.BlockSpec((None, 256, 1), c3),
        pl.BlockSpec((None, None, 256, NSA_WINDOW), lambda b, j, pt: (b, layer, 0, 0)),
        pl.BlockSpec((None, NSA_HEADS, HEAD_DIM), c3),
        pl.BlockSpec((None, 2, 512, 1), c4),
    ]
    grid_spec = pltpu.PrefetchScalarGridSpec(
        num_scalar_prefetch=1,
        grid=(nb, n_pages // kpages),
        in_specs=in_specs,
        out_specs=[pl.BlockSpec((None, NSA_HEADS, HEAD_DIM), c3),
                   pl.BlockSpec((None, 512, 1), c3),
                   pl.BlockSpec((None, 256, NSA_WINDOW), c3)],
        scratch_shapes=[
            pltpu.VMEM((512, LANES), F32), pltpu.VMEM((NSA_KV_GROUPS, n_pages, LANES), F32),
            pltpu.VMEM((SUBLANES, LANES), F32), pltpu.VMEM((SUBLANES, LANES), F32), pltpu.VMEM((512, LANES), F32),
            pltpu.VMEM((SUBLANES, LANES), F32), pltpu.VMEM((SUBLANES, LANES), F32), pltpu.VMEM((512, LANES), F32),
        ],
    )
    return pl.pallas_call(
        functools.partial(_nsa_sample_kernel, kpages=kpages, n_pages=n_pages),
        grid_spec=grid_spec,
        out_shape=[jax.ShapeDtypeStruct((nb, NSA_HEADS, HEAD_DIM), F32),
                   jax.ShapeDtypeStruct((nb, 512, 1), F32),
                   jax.ShapeDtypeStruct((nb, 256, NSA_WINDOW), F32)],
        compiler_params=_cparams(("parallel", "arbitrary")),
        name="nsa_sample",
    )(page_table, *([cache_fm] * kpages), kv, qh, qcol, newsel, newwin, win_fm, g0, g12)


SampleProj = collections.namedtuple(
    "SampleProj", ["fq", "fox_rows", "nq", "ng", "nsa_rows", "win_rows", "dq", "diff_rows"])


def _fox_cache_view(c):
    return c.transpose(0, 1, 4, 3, 2).reshape(c.shape[0], 2 * 516, PAGE_SIZE)


def _kv_cache_view(c):
    return c.transpose(0, 1, 3, 4, 5, 2).reshape(c.shape[0], 2 * 512, PAGE_SIZE)


def _win_state_view(s):
    return s.transpose(0, 1, 3, 4, 5, 2).reshape(s.shape[0], DEPTH, 256, NSA_WINDOW)


def _fox_sample(cache_fm, page_table, layer, sp, kpages):
    nb = sp.fq.shape[0]
    r0 = 4 * layer
    qk = (sp.fq.reshape(nb, FOX_HEADS, HEAD_DIM) * (HEAD_DIM ** -0.5)).transpose(0, 2, 1).reshape(nb, 256)
    qcol = jnp.zeros((nb, FOX_WIN), F32).at[:, r0:r0 + 256].set(qk)
    rows = sp.fox_rows.reshape(nb, FOX_HEADS, 129).transpose(0, 2, 1).reshape(nb, 516)
    newcol = jnp.zeros((nb, FOX_WIN), F32).at[:, r0:r0 + 516].set(rows)
    o = _fox_sample_attn(cache_fm, page_table, layer, qcol[..., None], newcol[..., None], kpages)[..., 0]
    return o[:, r0:r0 + 256].reshape(nb, HEAD_DIM, FOX_HEADS).transpose(0, 2, 1).reshape(nb, 256)


def _diff_sample(cache_fm, page_table, layer, sp, lam, lam_init, g, kpages):
    nb = sp.dq.shape[0]
    q = sp.dq.reshape(nb, DIFF_HEADS, HEAD_DIM) * (DIFF_QK_DIM ** -0.5)
    qcol = jnp.concatenate([q, jnp.zeros_like(q)], axis=-1).reshape(nb, 512, 1)
    newcol = sp.diff_rows.reshape(nb, 512, 1)
    o = _diff_sample_attn(cache_fm, page_table, layer, lam.reshape(1).astype(F32), qcol, newcol,
                          g.reshape(HEAD_DIM, 1), 1.0 - lam_init, kpages)
    return o.reshape(nb, 256)


def _nsa_sample(cache_fm, win_fm, page_table, layer, sp, cmp_pos, cmp_w, kpages):
    nb = sp.nq.shape[0]
    w = cmp_w.transpose(0, 2, 1, 3)
    z = jnp.zeros_like(w)
    w2 = jnp.concatenate([jnp.concatenate([w, z], -1), jnp.concatenate([z, w], -1)], -2).astype(BF16)
    pe = cmp_pos.transpose(0, 2, 1)
    pe2 = jnp.concatenate([pe, pe], -1)
    kv = _nsa_compress_sample(cache_fm, page_table, layer, w2, pe2, kpages)
    q = sp.nq.reshape(nb, NSA_KV_GROUPS, NSA_HPG, HEAD_DIM)
    qt = q.transpose(0, 1, 3, 2)
    zq = jnp.zeros_like(qt)
    qh = jnp.stack([jnp.concatenate([qt, zq], 2), jnp.concatenate([zq, qt], 2)], 2)
    qcol = (sp.nq.reshape(nb, 512) * (HEAD_DIM ** -0.5))[..., None]
    newsel = sp.nsa_rows.reshape(nb, 512)[:, 256:, None]
    newwin = sp.win_rows.reshape(nb, 256, 1)
    gates = sp.ng.reshape(nb, 3, NSA_HEADS)
    g0 = jnp.broadcast_to(gates[:, 0, :, None], (nb, NSA_HEADS, HEAD_DIM))
    g12 = jnp.broadcast_to(gates[:, 1:, :, None], (nb, 2, NSA_HEADS, HEAD_DIM)).reshape(nb, 2, 512, 1)
    oc, osw, wnew = _nsa_sample_attn(cache_fm, win_fm, page_table, layer,
                                     kv, qh, qcol, newsel, newwin, g0, g12, kpages)
    win_new = wnew.reshape(nb, 2, NSA_KV_GROUPS, HEAD_DIM, NSA_WINDOW).transpose(0, 4, 1, 2, 3)
    return oc.reshape(nb, 512) + osw.reshape(nb, 512), win_new


def _diff_lambda(lp, layer):
    lam_init = 0.8 - 0.6 * math.exp(-0.3 * layer)
    lp = lp.astype(F32)
    lam = jnp.exp(jnp.sum(lp[0] * lp[1])) - jnp.exp(jnp.sum(lp[2] * lp[3])) + lam_init
    return lam, lam_init


def _pad_rows(a, n):
    return jnp.concatenate([a, jnp.zeros((n - a.shape[0],) + a.shape[1:], a.dtype)], 0)


def kernel(x_prompt, x_sample, cache_fox, cache_nsa, cache_diff, state_nsa_win, page_table,
           w_in, b_forget, b_nsa_gate, nsa_cmp_pos, nsa_cmp_w, diff_lambda, diff_subln_g,
           w_out, ln_attn_g, ln_attn_b, w_router, b_router, w_up, b_up, w_down, b_down,
           ln_ffn_g, ln_ffn_b):
    nb, seq, _ = x_prompt.shape
    n_dec, n_new, _ = x_sample.shape
    past = page_table.shape[1] * cache_fox.shape[2]
    tabs_p = _rope_tables(jnp.arange(seq, dtype=F32))
    n_dec_pad = LANES
    tabs_s = _rope_tables(jnp.full((n_dec_pad,), past, F32))
    perm = jnp.asarray(np.maximum(_PERM, 0), jnp.int32)
    valid = jnp.asarray((_PERM >= 0).astype(np.float32))[:, None]

    assert n_new == 1 and past % NSA_BLOCK == 0 and state_nsa_win.shape[2] == NSA_WINDOW
    fox_fm = _fox_cache_view(cache_fox)
    nsa_fm = _kv_cache_view(cache_nsa)
    diff_fm = _kv_cache_view(cache_diff)
    win_fm = _win_state_view(state_nsa_win)

    hp = x_prompt
    hs = x_sample.reshape(n_dec, D_MODEL)
    m_p = nb * seq
    m_all = m_p + n_dec
    m_pad = -(-m_all // 512) * 512
    outs = {k: [] for k in ('fox_p', 'fox_s', 'nsa_p', 'nsa_s', 'diff_p', 'diff_s', 'win_p', 'win_s')}
    for l in range(DEPTH):
        lam, lam_init = _diff_lambda(diff_lambda[l], l)
        lam_arr = lam.reshape(1).astype(F32)
        wt = (jnp.take(w_in[l], perm, axis=1).T * valid).astype(BF16)
        wg_tok = jnp.pad(w_in[l][:, C_NG:C_NG + 24], ((0, 0), (0, LANES - 24))).astype(BF16)
        bgate_col = jnp.pad(b_nsa_gate[l], (0, 8)).reshape(32, 1)
        bgate_row = jnp.pad(b_nsa_gate[l], (0, LANES - 24)).reshape(1, LANES)
        bforget_col = jnp.pad(b_forget[l], (0, 4)).reshape(8, 1)
        w_out_b = w_out[l].astype(BF16)
        wr = jnp.pad(w_router[l], ((0, 0), (0, LANES - N_EXPERTS))).astype(BF16)
        br = jnp.pad(b_router[l], (0, LANES - N_EXPERTS)).reshape(1, LANES)
        moe_w = (w_up[l][:, :, 0::2].astype(BF16), w_up[l][:, :, 1::2].astype(BF16),
                 b_up[l][:, None, 0::2], b_up[l][:, None, 1::2],
                 w_down[l].astype(BF16), b_down[l][:, None, :])
        g1, b1 = ln_attn_g[l].reshape(1, -1), ln_attn_b[l].reshape(1, -1)
        g2, b2 = ln_ffn_g[l].reshape(1, -1), ln_ffn_b[l].reshape(1, -1)

        pt, gates = _inproj(hp, wt, wg_tok, bgate_col, bforget_col, bgate_row, tabs_p, 256)
        outs['fox_p'].append(pt[:, R_FOX:R_FOX + 516].reshape(nb, 129, FOX_HEADS, seq).transpose(0, 3, 2, 1))
        outs['nsa_p'].append(pt[:, R_NSA:R_NSA + 512].reshape(nb, 4, NSA_KV_GROUPS, HEAD_DIM, seq)
                             .transpose(0, 4, 1, 2, 3))
        outs['diff_p'].append(pt[:, R_DIFF:R_DIFF + 512].reshape(nb, DIFF_HEADS, 2, HEAD_DIM, seq)
                              .transpose(0, 4, 1, 2, 3))
        outs['win_p'].append(pt[:, R_WIN:R_WIN + 256, seq - NSA_WINDOW:]
                             .reshape(nb, 2, NSA_KV_GROUPS, HEAD_DIM, NSA_WINDOW).transpose(0, 4, 1, 2, 3))
        c = jnp.cumsum(pt[:, R_LOGF:R_LOGF + FOX_HEADS, :], axis=-1)
        o_fox = _fox_attention_t(pt, c, 512)
        o_diff = _diff_attention_t(pt, lam_arr, diff_subln_g[l].reshape(HEAD_DIM, 1), 1.0 - lam_init, 512)
        n_cmp = seq // NSA_BLOCK
        kcv = pt[:, R_NSA:R_NSA + 256].reshape(nb, 2, NSA_KV_GROUPS, HEAD_DIM, n_cmp, NSA_BLOCK)
        xb = kcv.transpose(1, 0, 2, 4, 5, 3).reshape(2, nb * NSA_KV_GROUPS * n_cmp, NSA_BLOCK * HEAD_DIM)
        cmp = _compress(xb, nsa_cmp_pos[l].reshape(2, 1, NSA_BLOCK * HEAD_DIM),
                        nsa_cmp_w[l].reshape(2, NSA_BLOCK * HEAD_DIM, HEAD_DIM), 128)
        cmp = cmp.reshape(2, nb, NSA_KV_GROUPS, n_cmp, HEAD_DIM)
        o_cmp, sel = _nsa_cmp_t(pt, cmp[0], cmp[1].transpose(0, 1, 3, 2), 256)
        o_sel = _nsa_band_t(pt, sel, 256, True)
        o_win = _nsa_band_t(pt, None, 256, False)

        xs_pad = _pad_rows(hs, n_dec_pad)[None]
        pt_s, _ = _inproj(xs_pad, wt, wg_tok, bgate_col, bforget_col, bgate_row, tabs_s, n_dec_pad)
        ps = pt_s[0, :, :n_dec].T
        sp = SampleProj(
            fq=ps[:, R_FQ:R_FQ + 256].reshape(n_dec, 1, FOX_HEADS, HEAD_DIM),
            fox_rows=ps[:, R_FOX:R_FOX + 516].reshape(n_dec, 1, 129, FOX_HEADS).transpose(0, 1, 3, 2),
            nq=ps[:, R_NQ:R_NQ + 512].reshape(n_dec, 1, NSA_HEADS, HEAD_DIM),
            ng=ps[:, R_GATE:R_GATE + 24].reshape(n_dec, 1, 3, NSA_HEADS),
            nsa_rows=ps[:, R_NSA:R_NSA + 512].reshape(n_dec, 1, 4, NSA_KV_GROUPS, HEAD_DIM),
            win_rows=ps[:, R_WIN:R_WIN + 256].reshape(n_dec, 1, 2, NSA_KV_GROUPS, HEAD_DIM),
            dq=ps[:, R_DQ:R_DQ + 256].reshape(n_dec, 1, DIFF_HEADS, 2, DIFF_QK_DIM),
            diff_rows=ps[:, R_DIFF:R_DIFF + 512].reshape(n_dec, 1, DIFF_HEADS, 2, HEAD_DIM))
        s_fox = _fox_sample(fox_fm, page_table, l, sp, SAMPLE_PAGES_PER_STEP)
        s_nsa, win_new = _nsa_sample(nsa_fm, win_fm, page_table, l, sp, nsa_cmp_pos[l], nsa_cmp_w[l],
                                     SAMPLE_PAGES_PER_STEP)
        s_diff = _diff_sample(diff_fm, page_table, l, sp, lam, lam_init, diff_subln_g[l], SAMPLE_PAGES_PER_STEP)
        outs['fox_s'].append(sp.fox_rows)
        outs['nsa_s'].append(sp.nsa_rows)
        outs['diff_s'].append(sp.diff_rows)
        outs['win_s'].append(win_new)

        def rows(a_p, a_s):
            return _pad_rows(jnp.concatenate([a_p.reshape(m_p, -1), a_s], 0), m_pad)

        zero_s = jnp.zeros((n_dec, 512), F32)
        h_all = rows(hp, hs)
        h1, logits = _outproj(rows(o_fox, s_fox), rows(o_cmp, s_nsa), rows(o_sel, zero_s), rows(o_win, zero_s),
                              rows(o_diff, s_diff), h_all, w_out_b, g1, b1, wr, br, 512)
        h2 = _moe_block(h1, logits[:, :N_EXPERTS], moe_w, g2, b2, 512, 512)
        hp = h2[:m_p].reshape(nb, seq, D_MODEL)
        hs = h2[m_p:m_all]

    st = lambda k: jnp.stack(outs[k], axis=1)
    return (hp, hs.reshape(n_dec, n_new, D_MODEL),
            st('fox_p'), st('fox_s'), st('nsa_p'), st('nsa_s'),
            st('diff_p'), st('diff_s'), st('win_p'), st('win_s'))
```

```python
import collections
import functools
import math

import numpy as np
import jax
import jax.numpy as jnp
from jax import lax
from jax.experimental import pallas as pl
from jax.experimental.pallas import tpu as pltpu

F32 = jnp.float32
BF16 = jnp.bfloat16

D_MODEL = 1024
DEPTH = 2
PAGE_SIZE = 128
HEAD_DIM = 64
FOX_HEADS = 4
NSA_HEADS = 8
NSA_KV_GROUPS = 2
NSA_HPG = NSA_HEADS // NSA_KV_GROUPS
DIFF_HEADS = 4
DIFF_QK_DIM = HEAD_DIM // 2
NSA_BLOCK = 64
BLOCK_SHIFT = 6
NSA_TOP_N = 16
NSA_WINDOW = 512
ROPE_THETA = 500000.0
N_EXPERTS = 32
TOP_K = 4
SWIGLU_LIMIT = 7.0
SWIGLU_ALPHA = 1.702
LN_EPS = 1e-5
RMS_EPS = 1e-5
DEEPNORM_ALPHA = (2 * DEPTH) ** 0.25
NEG = -1e30
FORCE_SCORE = 1e4

SUBLANES = 8
LANES = 128
VMEM_LIMIT_BYTES = 48 * 1024 * 1024
SAMPLE_PAGES_PER_STEP = 16
NSA_SEL_TK = 1024

C_FQ, C_FK, C_FV, C_FF = 0, 256, 512, 768
C_NQ, C_NKC, C_NKW, C_NVW, C_NG = 772, 1284, 1796, 1924, 2052
C_DQ, C_DK, C_DV = 2076, 2332, 2588
N_IN = 2844

R_NQ = 0
R_NSA = 512
R_DIFF = 1024
R_FQ = 1536
R_FK = 1792
R_FV = 2048
R_WIN = 2304
R_DQ = 2560
R_GATE = 2816
R_FOX = 2848
R_LOGF = R_FOX + 512
NP_ROWS = 3368


def _build_perm():
    perm = np.full((NP_ROWS,), -1, np.int64)
    perm[R_NQ:R_NQ + 512] = C_NQ + np.arange(512)
    perm[R_NSA:R_NSA + 512] = C_NKC + np.arange(512)
    for h in range(DIFF_HEADS):
        perm[R_DIFF + h * 128:R_DIFF + h * 128 + 64] = C_DK + h * 64 + np.arange(64)
        perm[R_DIFF + h * 128 + 64:R_DIFF + h * 128 + 128] = C_DV + h * 64 + np.arange(64)
    perm[R_FQ:R_FQ + 256] = C_FQ + np.arange(256)
    perm[R_FK:R_FK + 256] = C_FK + np.arange(256)
    perm[R_FV:R_FV + 256] = C_FV + np.arange(256)
    perm[R_WIN:R_WIN + 256] = C_NKW + np.arange(256)
    perm[R_DQ:R_DQ + 256] = C_DQ + np.arange(256)
    perm[R_GATE:R_GATE + 24] = C_NG + np.arange(24)
    for c in range(64):
        for h in range(FOX_HEADS):
            perm[R_FOX + c * 4 + h] = C_FK + h * 64 + c
            perm[R_FOX + (64 + c) * 4 + h] = C_FV + h * 64 + c
    for h in range(FOX_HEADS):
        perm[R_FOX + 512 + h] = C_FF + h
    return perm


_PERM = _build_perm()
_ROT64_ROWS = tuple([R_NQ + h * 64 for h in range(NSA_HEADS)]
                    + [R_NSA + g * 64 for g in range(NSA_KV_GROUPS)]
                    + [R_NSA + 256 + g * 64 for g in range(NSA_KV_GROUPS)]
                    + [R_WIN + g * 64 for g in range(NSA_KV_GROUPS)])
_ROT32_ROWS = tuple([R_DQ + i * 32 for i in range(2 * DIFF_HEADS)]
                    + [R_DIFF + h * 128 + m * 32 for h in range(DIFF_HEADS) for m in range(2)])


def _cparams(sem):
    return pltpu.CompilerParams(dimension_semantics=sem, vmem_limit_bytes=VMEM_LIMIT_BYTES)


def _inproj_kernel(x_ref, w_ref, wg_ref, bgate_ref, bforget_ref, bgrow_ref, cn_ref, sn_ref, cd_ref, sd_ref,
                   o_ref, g_ref):
    x = x_ref[...].astype(BF16)
    o_ref[...] = lax.dot_general(w_ref[...], x, (((1,), (1,)), ((), ())), preferred_element_type=F32)
    cn = cn_ref[...]
    sn = sn_ref[...]
    for r in _ROT64_ROWS:
        x1 = o_ref[r:r + 8, :]
        x2 = o_ref[r + 8:r + 16, :]
        o_ref[r:r + 8, :] = x1 * cn - x2 * sn
        o_ref[r + 8:r + 16, :] = x2 * cn + x1 * sn
    cd = cd_ref[...]
    sd = sd_ref[...]
    for r in _ROT32_ROWS:
        v = o_ref[r:r + 8, :]
        o_ref[r:r + 8, :] = v * cd + pltpu.roll(v, 4, 0) * sd
    z = o_ref[R_GATE:R_GATE + 32, :] + bgate_ref[...]
    o_ref[R_GATE:R_GATE + 32, :] = 1.0 / (1.0 + jnp.exp(-z))
    z = o_ref[R_LOGF:R_LOGF + 8, :] + bforget_ref[...]
    o_ref[R_LOGF:R_LOGF + 8, :] = jnp.minimum(z, 0.0) - jnp.log1p(jnp.exp(-jnp.abs(z)))
    zg = jnp.dot(x, wg_ref[...], preferred_element_type=F32) + bgrow_ref[...]
    g_ref[...] = 1.0 / (1.0 + jnp.exp(-zg))


def _inproj(x, wt, wg, bgate_col, bforget_col, bgate_row, tabs, tn):
    nb, nt, k = x.shape
    cn, sn, cd, sd = tabs
    const = lambda b, i: (0, 0)
    tab = lambda b, i: (0, i)
    return pl.pallas_call(
        _inproj_kernel,
        grid=(nb, nt // tn),
        in_specs=[
            pl.BlockSpec((None, tn, k), lambda b, i: (b, i, 0)),
            pl.BlockSpec((NP_ROWS, k), const),
            pl.BlockSpec((k, LANES), const),
            pl.BlockSpec((32, 1), const),
            pl.BlockSpec((8, 1), const),
            pl.BlockSpec((1, LANES), const),
            pl.BlockSpec((8, tn), tab),
            pl.BlockSpec((8, tn), tab),
            pl.BlockSpec((8, tn), tab),
            pl.BlockSpec((8, tn), tab),
        ],
        out_specs=[
            pl.BlockSpec((None, NP_ROWS, tn), lambda b, i: (b, 0, i)),
            pl.BlockSpec((None, tn, LANES), lambda b, i: (b, i, 0)),
        ],
        out_shape=[jax.ShapeDtypeStruct((nb, NP_ROWS, nt), F32),
                   jax.ShapeDtypeStruct((nb, nt, LANES), F32)],
        compiler_params=_cparams(("parallel", "parallel")),
        name="inproj",
    )(x, wt, wg, bgate_col, bforget_col, bgate_row, cn, sn, cd, sd)


def _rope_tables(pos):
    inv8 = ROPE_THETA ** (-jnp.arange(8, dtype=F32) / 8)
    ang8 = pos[:, None] * inv8[None, :]
    cn, sn = jnp.cos(ang8).T, jnp.sin(ang8).T
    inv4 = ROPE_THETA ** (-jnp.arange(4, dtype=F32) / 4)
    ang4 = pos[:, None] * inv4[None, :]
    c4, s4 = jnp.cos(ang4).T, jnp.sin(ang4).T
    cd = jnp.concatenate([c4, c4], 0)
    sd = jnp.concatenate([-s4, s4], 0)
    return cn, sn, cd, sd


def _tri_pairs(nq, lo_tiles=None):
    qi, kj = [], []
    for i in range(nq):
        j0 = 0 if lo_tiles is None else max(0, i - lo_tiles)
        for j in range(j0, i + 1):
            qi.append(i)
            kj.append(j)
    first = [1 if (p == 0 or qi[p] != qi[p - 1]) else 0 for p in range(len(qi))]
    return (jnp.asarray(qi, jnp.int32), jnp.asarray(kj, jnp.int32), jnp.asarray(first, jnp.int32))


def _online_update(s, v_t, m_ref, l_ref, acc_ref):
    m_prev = m_ref[:, 0:1]
    m_new = jnp.maximum(m_prev, jnp.max(s, axis=1, keepdims=True))
    alpha = jnp.exp(m_prev - m_new)
    p = jnp.exp(s - m_new)
    l_ref[...] = jnp.broadcast_to(alpha * l_ref[:, 0:1] + jnp.sum(p, axis=1, keepdims=True), l_ref.shape)
    m_ref[...] = jnp.broadcast_to(m_new, m_ref.shape)
    pv = lax.dot_general(p.astype(BF16), v_t, (((1,), (1,)), ((), ())), preferred_element_type=F32)
    acc_ref[...] = alpha * acc_ref[...] + pv


def _fox_kernel(qi_ref, kj_ref, first_ref, q_ref, k_ref, v_ref, c_ref, o_ref, q_s, m_s, l_s, acc_s, *, tq, tk):
    p_id = pl.program_id(1)
    qi = qi_ref[p_id]
    kj = kj_ref[p_id]
    scale = HEAD_DIM ** -0.5

    @pl.when(first_ref[p_id] == 1)
    def _():
        for h in range(FOX_HEADS):
            q_s[h] = q_ref[h * 64:(h + 1) * 64, :].T.astype(BF16)
        m_s[...] = jnp.full(m_s.shape, NEG, F32)
        l_s[...] = jnp.zeros(l_s.shape, F32)
        acc_s[...] = jnp.zeros(acc_s.shape, F32)

    row = qi * tq + lax.broadcasted_iota(jnp.int32, (tq, tk), 0)
    col = kj * tk + lax.broadcasted_iota(jnp.int32, (tq, tk), 1)
    mask = col <= row
    for h in range(FOX_HEADS):
        k_t = k_ref[h * 64:(h + 1) * 64, :].astype(BF16)
        v_t = v_ref[h * 64:(h + 1) * 64, :].astype(BF16)
        s = jnp.dot(q_s[h], k_t, preferred_element_type=F32) * scale - c_ref[h:h + 1, :]
        s = jnp.where(mask, s, NEG)
        _online_update(s, v_t, m_s.at[h], l_s.at[h], acc_s.at[h])

    @pl.when(kj == qi)
    def _():
        for h in range(FOX_HEADS):
            o_ref[:, h * 64:(h + 1) * 64] = acc_s[h] / l_s[h][:, 0:1]


def _fox_attention(pt, c, tq):
    nb, _, nt = pt.shape
    tk = tq
    qi, kj, first = _tri_pairs(nt // tq)
    blk = 256
    grid_spec = pltpu.PrefetchScalarGridSpec(
        num_scalar_prefetch=3,
        grid=(nb, qi.shape[0]),
        in_specs=[
            pl.BlockSpec((None, blk, tq), lambda b, p, qi, kj, f: (b, R_FQ // blk, qi[p])),
            pl.BlockSpec((None, blk, tk), lambda b, p, qi, kj, f: (b, R_FK // blk, kj[p])),
            pl.BlockSpec((None, blk, tk), lambda b, p, qi, kj, f: (b, R_FV // blk, kj[p])),
            pl.BlockSpec((None, FOX_HEADS, tk), lambda b, p, qi, kj, f: (b, 0, kj[p])),
        ],
        out_specs=pl.BlockSpec((None, tq, blk), lambda b, p, qi, kj, f: (b, qi[p], 0)),
        scratch_shapes=[
            pltpu.VMEM((FOX_HEADS, tq, HEAD_DIM), BF16),
            pltpu.VMEM((FOX_HEADS, tq, LANES), F32),
            pltpu.VMEM((FOX_HEADS, tq, LANES), F32),
            pltpu.VMEM((FOX_HEADS, tq, HEAD_DIM), F32),
        ],
    )
    return pl.pallas_call(
        functools.partial(_fox_kernel, tq=tq, tk=tk),
        grid_spec=grid_spec,
        out_shape=jax.ShapeDtypeStruct((nb, nt, blk), F32),
        compiler_params=_cparams(("parallel", "arbitrary")),
        name="fox_attn",
    )(qi, kj, first, pt, pt, pt, c)


def _diff_kernel(qi_ref, kj_ref, first_ref, lam_ref, q_ref, kv_ref, g_ref, o_ref, q_s, m_s, l_s, acc_s,
                 *, tq, tk, out_scale):
    p_id = pl.program_id(1)
    qi = qi_ref[p_id]
    kj = kj_ref[p_id]
    scale = DIFF_QK_DIM ** -0.5

    @pl.when(first_ref[p_id] == 1)
    def _():
        for i in range(2 * DIFF_HEADS):
            q_s[i] = q_ref[i * 32:(i + 1) * 32, :].T.astype(BF16)
        m_s[...] = jnp.full(m_s.shape, NEG, F32)
        l_s[...] = jnp.zeros(l_s.shape, F32)
        acc_s[...] = jnp.zeros(acc_s.shape, F32)

    row = qi * tq + lax.broadcasted_iota(jnp.int32, (tq, tk), 0)
    col = kj * tk + lax.broadcasted_iota(jnp.int32, (tq, tk), 1)
    mask = col <= row
    for h in range(DIFF_HEADS):
        v_t = kv_ref[h * 128 + 64:h * 128 + 128, :].astype(BF16)
        for m in range(2):
            i = 2 * h + m
            k_t = kv_ref[h * 128 + m * 32:h * 128 + (m + 1) * 32, :].astype(BF16)
            s = jnp.dot(q_s[i], k_t, preferred_element_type=F32) * scale
            s = jnp.where(mask, s, NEG)
            _online_update(s, v_t, m_s.at[i], l_s.at[i], acc_s.at[i])

    @pl.when(kj == qi)
    def _():
        lam = lam_ref[0]
        for h in range(DIFF_HEADS):
            o1 = acc_s[2 * h] / l_s[2 * h][:, 0:1]
            o2 = acc_s[2 * h + 1] / l_s[2 * h + 1][:, 0:1]
            o = o1 - lam * o2
            o = o * lax.rsqrt(jnp.mean(o * o, axis=-1, keepdims=True) + RMS_EPS) * g_ref[...] * out_scale
            o_ref[:, h * 64:(h + 1) * 64] = o


def _diff_attention(pt, lam, g_row, out_scale, tq):
    nb, _, nt = pt.shape
    tk = tq
    qi, kj, first = _tri_pairs(nt // tq)
    grid_spec = pltpu.PrefetchScalarGridSpec(
        num_scalar_prefetch=4,
        grid=(nb, qi.shape[0]),
        in_specs=[
            pl.BlockSpec((None, 256, tq), lambda b, p, qi, kj, f, lam: (b, R_DQ // 256, qi[p])),
            pl.BlockSpec((None, 512, tk), lambda b, p, qi, kj, f, lam: (b, R_DIFF // 512, kj[p])),
            pl.BlockSpec((1, HEAD_DIM), lambda b, p, qi, kj, f, lam: (0, 0)),
        ],
        out_specs=pl.BlockSpec((None, tq, 256), lambda b, p, qi, kj, f, lam: (b, qi[p], 0)),
        scratch_shapes=[
            pltpu.VMEM((2 * DIFF_HEADS, tq, DIFF_QK_DIM), BF16),
            pltpu.VMEM((2 * DIFF_HEADS, tq, LANES), F32),
            pltpu.VMEM((2 * DIFF_HEADS, tq, LANES), F32),
            pltpu.VMEM((2 * DIFF_HEADS, tq, HEAD_DIM), F32),
        ],
    )
    return pl.pallas_call(
        functools.partial(_diff_kernel, tq=tq, tk=tk, out_scale=out_scale),
        grid_spec=grid_spec,
        out_shape=jax.ShapeDtypeStruct((nb, nt, 256), F32),
        compiler_params=_cparams(("parallel", "arbitrary")),
        name="diff_attn",
    )(qi, kj, first, lam, pt, pt, g_row)


def _compress_kernel(x_ref, pe_ref, w_ref, o_ref):
    x = (x_ref[...] + pe_ref[...]).astype(BF16)
    o_ref[...] = jnp.dot(x, w_ref[...].astype(BF16), preferred_element_type=F32)


def _compress(xb, pe, w, tm):
    _, m, kk = xb.shape
    return pl.pallas_call(
        _compress_kernel,
        grid=(2, m // tm),
        in_specs=[
            pl.BlockSpec((None, tm, kk), lambda t, i: (t, i, 0)),
            pl.BlockSpec((None, 1, kk), lambda t, i: (t, 0, 0)),
            pl.BlockSpec((None, kk, HEAD_DIM), lambda t, i: (t, 0, 0)),
        ],
        out_specs=pl.BlockSpec((None, tm, HEAD_DIM), lambda t, i: (t, i, 0)),
        out_shape=jax.ShapeDtypeStruct((2, m, HEAD_DIM), F32),
        compiler_params=_cparams(("parallel", "parallel")),
        name="nsa_compress",
    )(xb, pe, w)


def _select_mask(imp, qpos, n_blk):
    jb = lax.broadcasted_iota(jnp.int32, imp.shape, 1)
    cur = qpos >> BLOCK_SHIFT
    forced = (jb == 0) | (jb == cur) | (jb == cur - 1)
    score = jnp.where(jb <= cur, jnp.where(forced, FORCE_SCORE, imp), NEG)
    rank = jnp.zeros(imp.shape, F32)
    for j in range(n_blk):
        col = score[:, j:j + 1]
        ahead = (col > score) | ((col == score) & (jb > j))
        rank = rank + jnp.where(ahead, 1.0, 0.0)
    return jnp.where(rank < float(NSA_TOP_N), 1.0, 0.0)


def _nsa_cmp_kernel(q_ref, kc_ref, vc_ref, gate_ref, o_ref, sel_ref, *, tq, n_cmp):
    qi = pl.program_id(2)
    g = pl.program_id(1)
    scale = HEAD_DIM ** -0.5
    kc = kc_ref[...].astype(BF16)
    vc = vc_ref[...].astype(BF16)
    qpos = qi * tq + lax.broadcasted_iota(jnp.int32, (tq, 1), 0)
    nb_iota = lax.broadcasted_iota(jnp.int32, (tq, n_cmp), 1)
    mask = nb_iota < ((qpos + 1) >> BLOCK_SHIFT)
    maskf = jnp.where(mask, 1.0, 0.0)
    imp = jnp.zeros((tq, n_cmp), F32)
    gates = gate_ref[...]
    for h in range(NSA_HPG):
        qh = q_ref[h * 64:(h + 1) * 64, :].T.astype(BF16)
        lc = lax.dot_general(qh, kc, (((1,), (1,)), ((), ())), preferred_element_type=F32) * scale
        lc = jnp.where(mask, lc, NEG)
        m = jnp.max(lc, axis=1, keepdims=True)
        p = jnp.exp(lc - m) * maskf
        pc = p / jnp.maximum(jnp.sum(p, axis=1, keepdims=True), 1e-30)
        imp = imp + pc
        oc = jnp.dot(pc.astype(BF16), vc, preferred_element_type=F32)
        gcol = jnp.where(g == 0, gates[:, h:h + 1], gates[:, NSA_HPG + h:NSA_HPG + h + 1])
        o_ref[:, h * 64:(h + 1) * 64] = oc * gcol
    sel_ref[...] = _select_mask(imp, qpos, n_cmp)


def _nsa_cmp(pt, gates, kc, vc, tq):
    nb, _, nt = pt.shape
    n_cmp = kc.shape[2]
    return pl.pallas_call(
        functools.partial(_nsa_cmp_kernel, tq=tq, n_cmp=n_cmp),
        grid=(nb, NSA_KV_GROUPS, nt // tq),
        in_specs=[
            pl.BlockSpec((None, 256, tq), lambda b, g, i: (b, R_NQ // 256 + g, i)),
            pl.BlockSpec((None, None, n_cmp, HEAD_DIM), lambda b, g, i: (b, g, 0, 0)),
            pl.BlockSpec((None, None, n_cmp, HEAD_DIM), lambda b, g, i: (b, g, 0, 0)),
            pl.BlockSpec((None, tq, LANES), lambda b, g, i: (b, i, 0)),
        ],
        out_specs=[
            pl.BlockSpec((None, tq, 256), lambda b, g, i: (b, i, g)),
            pl.BlockSpec((None, None, tq, n_cmp), lambda b, g, i: (b, g, i, 0)),
        ],
        out_shape=[jax.ShapeDtypeStruct((nb, nt, 512), F32),
                   jax.ShapeDtypeStruct((nb, NSA_KV_GROUPS, nt, n_cmp), F32)],
        compiler_params=_cparams(("parallel", "parallel", "parallel")),
        name="nsa_cmp",
    )(pt, kc, vc, gates)


def _nsa_band_kernel(qi_ref, kj_ref, first_ref, last_ref, q_ref, k_ref, v_ref, gate_ref, *rest,
                     tq, tk, selected, gate_base):
    if selected:
        sel_ref, o_ref, q_s, m_s, l_s, acc_s = rest
    else:
        o_ref, q_s, m_s, l_s, acc_s = rest
    p_id = pl.program_id(2)
    g = pl.program_id(1)
    qi = qi_ref[p_id]
    kj = kj_ref[p_id]
    scale = HEAD_DIM ** -0.5

    @pl.when(first_ref[p_id] == 1)
    def _():
        for h in range(NSA_HPG):
            q_s[h * tq:(h + 1) * tq, :] = q_ref[h * 64:(h + 1) * 64, :].T.astype(BF16)
        m_s[...] = jnp.full(m_s.shape, NEG, F32)
        l_s[...] = jnp.zeros(l_s.shape, F32)
        acc_s[...] = jnp.zeros(acc_s.shape, F32)

    row = qi * tq + lax.broadcasted_iota(jnp.int32, (tq, tk), 0)
    col = kj * tk + lax.broadcasted_iota(jnp.int32, (tq, tk), 1)
    mask = col <= row
    if selected:
        n_blk = sel_ref.shape[-1]
        blk_of_col = kj * (tk // NSA_BLOCK) + (lax.broadcasted_iota(jnp.int32, (n_blk, tk), 1) >> BLOCK_SHIFT)
        expand = jnp.where(lax.broadcasted_iota(jnp.int32, (n_blk, tk), 0) == blk_of_col, 1.0, 0.0).astype(BF16)
        picked = jnp.dot(sel_ref[...].astype(BF16), expand, preferred_element_type=F32)
        mask = mask & (picked > 0.5)
    else:
        mask = mask & (col > row - NSA_WINDOW)
    keep = jnp.where(mask, 1.0, 0.0)
    k_t = k_ref[...].astype(BF16)
    v_t = v_ref[...].astype(BF16)
    s = jnp.dot(q_s[...], k_t, preferred_element_type=F32) * scale
    s = jnp.where(jnp.concatenate([keep] * NSA_HPG, axis=0) > 0.5, s, NEG)
    _online_update(s, v_t, m_s, l_s, acc_s)

    @pl.when(last_ref[p_id] == 1)
    def _():
        gates = gate_ref[...]
        for h in range(NSA_HPG):
            c0 = gate_base + h
            gcol = jnp.where(g == 0, gates[:, c0:c0 + 1], gates[:, c0 + NSA_HPG:c0 + NSA_HPG + 1])
            o_ref[:, h * 64:(h + 1) * 64] = acc_s[h * tq:(h + 1) * tq, :] / l_s[h * tq:(h + 1) * tq, 0:1] * gcol


def _nsa_band(pt, gates, sel, tq, selected):
    nb, _, nt = pt.shape
    tk = tq
    nq = nt // tq
    qi, kj, first = _tri_pairs(nq, None if selected else -(-NSA_WINDOW // tk))
    last = jnp.concatenate([first[1:], jnp.ones((1,), jnp.int32)])
    if selected:
        rk, rv, gate_base = R_NSA + 256, R_NSA + 384, 8
    else:
        rk, rv, gate_base = R_WIN, R_WIN + 128, 16
    idx = lambda f: (lambda b, g, p, qi, kj, fi, la: f(b, g, p, qi, kj))
    in_specs = [
        pl.BlockSpec((None, 256, tq), idx(lambda b, g, p, qi, kj: (b, R_NQ // 256 + g, qi[p]))),
        pl.BlockSpec((None, HEAD_DIM, tk), idx(lambda b, g, p, qi, kj: (b, rk // 64 + g, kj[p]))),
        pl.BlockSpec((None, HEAD_DIM, tk), idx(lambda b, g, p, qi, kj: (b, rv // 64 + g, kj[p]))),
        pl.BlockSpec((None, tq, LANES), idx(lambda b, g, p, qi, kj: (b, qi[p], 0))),
    ]
    args = [pt, pt, pt, gates]
    if selected:
        in_specs.append(pl.BlockSpec((None, None, tq, sel.shape[-1]),
                                     idx(lambda b, g, p, qi, kj: (b, g, qi[p], 0))))
        args.append(sel)
    grid_spec = pltpu.PrefetchScalarGridSpec(
        num_scalar_prefetch=4,
        grid=(nb, NSA_KV_GROUPS, qi.shape[0]),
        in_specs=in_specs,
        out_specs=pl.BlockSpec((None, tq, 256), idx(lambda b, g, p, qi, kj: (b, qi[p], g))),
        scratch_shapes=[
            pltpu.VMEM((NSA_HPG * tq, HEAD_DIM), BF16),
            pltpu.VMEM((NSA_HPG * tq, LANES), F32),
            pltpu.VMEM((NSA_HPG * tq, LANES), F32),
            pltpu.VMEM((NSA_HPG * tq, HEAD_DIM), F32),
        ],
    )
    return pl.pallas_call(
        functools.partial(_nsa_band_kernel, tq=tq, tk=tk, selected=selected, gate_base=gate_base),
        grid_spec=grid_spec,
        out_shape=jax.ShapeDtypeStruct((nb, nt, 512), F32),
        compiler_params=_cparams(("parallel", "parallel", "arbitrary")),
        name="nsa_sel" if selected else "nsa_win",
    )(qi, kj, first, last, *args)


def _flash_step(s, v_t, m_ref, l_ref, acc_ref):
    m_prev = m_ref[...]
    m_new = jnp.maximum(m_prev, jnp.max(s, axis=0, keepdims=True))
    alpha = jnp.exp(m_prev - m_new)
    p = jnp.exp(s - m_new)
    l_ref[...] = alpha * l_ref[...] + jnp.sum(p, axis=0, keepdims=True)
    m_ref[...] = m_new
    acc_ref[...] = alpha * acc_ref[...] + jnp.dot(v_t, p.astype(BF16), preferred_element_type=F32)


def _kq(k_t, q_t):
    return lax.dot_general(k_t, q_t, (((0,), (0,)), ((), ())), preferred_element_type=F32)


def _causal_keep(qi, kj, tq, tk):
    key = kj * tk + lax.broadcasted_iota(jnp.int32, (tk, tq), 0)
    qry = qi * tq + lax.broadcasted_iota(jnp.int32, (tk, tq), 1)
    return key, qry


def _split3(x):
    x1 = x.astype(BF16)
    r1 = x - x1.astype(F32)
    x2 = r1.astype(BF16)
    x3 = (r1 - x2.astype(F32)).astype(BF16)
    return x1, x2, x3


def _fox_kernel_t(qi_ref, kj_ref, first_ref, q_ref, k_ref, v_ref, c_ref, o_ref, q_s, m_s, l_s, acc_s, *, tq, tk):
    p_id = pl.program_id(1)
    qi = qi_ref[p_id]
    kj = kj_ref[p_id]

    @pl.when(first_ref[p_id] == 1)
    def _():
        extra = jnp.where(lax.broadcasted_iota(jnp.int32, (16, tq), 0) < 3, -1.0, 0.0).astype(BF16)
        for h in range(FOX_HEADS):
            q_s[h, 0:64, :] = (q_ref[h * 64:(h + 1) * 64, :] * (HEAD_DIM ** -0.5)).astype(BF16)
            q_s[h, 64:80, :] = extra
        m_s[...] = jnp.full(m_s.shape, NEG, F32)
        l_s[...] = jnp.zeros(l_s.shape, F32)
        acc_s[...] = jnp.zeros(acc_s.shape, F32)

    key, qry = _causal_keep(qi, kj, tq, tk)
    mask = key <= qry
    sub16 = lax.broadcasted_iota(jnp.int32, (16, tk), 0)
    for h in range(FOX_HEADS):
        c1, c2, c3 = [jnp.broadcast_to(t.astype(F32), (16, tk)) for t in _split3(c_ref[h:h + 1, :])]
        extra = jnp.where(sub16 == 0, c1, jnp.where(sub16 == 1, c2, jnp.where(sub16 == 2, c3, 0.0)))
        k_aug = jnp.concatenate([k_ref[h * 64:(h + 1) * 64, :].astype(BF16), extra.astype(BF16)], axis=0)
        s = jnp.where(mask, _kq(k_aug, q_s[h]), NEG)
        _flash_step(s, v_ref[h * 64:(h + 1) * 64, :].astype(BF16), m_s.at[h], l_s.at[h], acc_s.at[h])

    @pl.when(kj == qi)
    def _():
        for h in range(FOX_HEADS):
            o_ref[:, h * 64:(h + 1) * 64] = (acc_s[h] / l_s[h]).T


def _fox_attention_t(pt, c, tq):
    nb, _, nt = pt.shape
    tk = tq
    qi, kj, first = _tri_pairs(nt // tq)
    blk = 256
    grid_spec = pltpu.PrefetchScalarGridSpec(
        num_scalar_prefetch=3,
        grid=(nb, qi.shape[0]),
        in_specs=[
            pl.BlockSpec((None, blk, tq), lambda b, p, qi, kj, f: (b, R_FQ // blk, qi[p])),
            pl.BlockSpec((None, blk, tk), lambda b, p, qi, kj, f: (b, R_FK // blk, kj[p])),
            pl.BlockSpec((None, blk, tk), lambda b, p, qi, kj, f: (b, R_FV // blk, kj[p])),
            pl.BlockSpec((None, FOX_HEADS, tk), lambda b, p, qi, kj, f: (b, 0, kj[p])),
        ],
        out_specs=pl.BlockSpec((None, tq, blk), lambda b, p, qi, kj, f: (b, qi[p], 0)),
        scratch_shapes=[
            pltpu.VMEM((FOX_HEADS, HEAD_DIM + 16, tq), BF16),
            pltpu.VMEM((FOX_HEADS, 1, tq), F32),
            pltpu.VMEM((FOX_HEADS, 1, tq), F32),
            pltpu.VMEM((FOX_HEADS, HEAD_DIM, tq), F32),
        ],
    )
    return pl.pallas_call(
        functools.partial(_fox_kernel_t, tq=tq, tk=tk),
        grid_spec=grid_spec,
        out_shape=jax.ShapeDtypeStruct((nb, nt, blk), F32),
        compiler_params=_cparams(("parallel", "arbitrary")),
        name="fox_attn",
    )(qi, kj, first, pt, pt, pt, c)


def _diff_kernel_t(qi_ref, kj_ref, first_ref, lam_ref, q_ref, kv_ref, g_ref, o_ref, q_s, m_s, l_s, acc_s,
                   *, tq, tk, out_scale):
    p_id = pl.program_id(1)
    qi = qi_ref[p_id]
    kj = kj_ref[p_id]
    scale = DIFF_QK_DIM ** -0.5

    @pl.when(first_ref[p_id] == 1)
    def _():
        q_s[...] = q_ref[...].astype(BF16)
        m_s[...] = jnp.full(m_s.shape, NEG, F32)
        l_s[...] = jnp.zeros(l_s.shape, F32)
        acc_s[...] = jnp.zeros(acc_s.shape, F32)

    key, qry = _causal_keep(qi, kj, tq, tk)
    mask = key <= qry
    for h in range(DIFF_HEADS):
        v_t = kv_ref[h * 128 + 64:h * 128 + 128, :].astype(BF16)
        for m in range(2):
            i = 2 * h + m
            k_t = kv_ref[h * 128 + m * 32:h * 128 + (m + 1) * 32, :].astype(BF16)
            s = jnp.where(mask, _kq(k_t, q_s[i * 32:(i + 1) * 32, :]) * scale, NEG)
            _flash_step(s, v_t, m_s.at[i], l_s.at[i], acc_s.at[i])

    @pl.when(kj == qi)
    def _():
        lam = lam_ref[0]
        for h in range(DIFF_HEADS):
            o = acc_s[2 * h] / l_s[2 * h] - lam * (acc_s[2 * h + 1] / l_s[2 * h + 1])
            o = o * lax.rsqrt(jnp.mean(o * o, axis=0, keepdims=True) + RMS_EPS) * g_ref[...] * out_scale
            o_ref[:, h * 64:(h + 1) * 64] = o.T


def _diff_attention_t(pt, lam, g_col, out_scale, tq):
    nb, _, nt = pt.shape
    tk = tq
    qi, kj, first = _tri_pairs(nt // tq)
    grid_spec = pltpu.PrefetchScalarGridSpec(
        num_scalar_prefetch=4,
        grid=(nb, qi.shape[0]),
        in_specs=[
            pl.BlockSpec((None, 256, tq), lambda b, p, qi, kj, f, lam: (b, R_DQ // 256, qi[p])),
            pl.BlockSpec((None, 512, tk), lambda b, p, qi, kj, f, lam: (b, R_DIFF // 512, kj[p])),
            pl.BlockSpec((HEAD_DIM, 1), lambda b, p, qi, kj, f, lam: (0, 0)),
        ],
        out_specs=pl.BlockSpec((None, tq, 256), lambda b, p, qi, kj, f, lam: (b, qi[p], 0)),
        scratch_shapes=[
            pltpu.VMEM((256, tq), BF16),
            pltpu.VMEM((2 * DIFF_HEADS, 1, tq), F32),
            pltpu.VMEM((2 * DIFF_HEADS, 1, tq), F32),
            pltpu.VMEM((2 * DIFF_HEADS, HEAD_DIM, tq), F32),
        ],
    )
    return pl.pallas_call(
        functools.partial(_diff_kernel_t, tq=tq, tk=tk, out_scale=out_scale),
        grid_spec=grid_spec,
        out_shape=jax.ShapeDtypeStruct((nb, nt, 256), F32),
        compiler_params=_cparams(("parallel", "arbitrary")),
        name="diff_attn",
    )(qi, kj, first, lam, pt, pt, g_col)


def _stack_heads(q_ref, tq):
    return jnp.concatenate([q_ref[h * 64:(h + 1) * 64, :] for h in range(NSA_HPG)], axis=1)


def _select_mask_t(imp, qpos, n_blk):
    jb = lax.broadcasted_iota(jnp.int32, imp.shape, 0)
    cur = qpos >> BLOCK_SHIFT
    forced = (jb == 0) | (jb == cur) | (jb == cur - 1)
    score = jnp.where(jb <= cur, jnp.where(forced, FORCE_SCORE, imp), NEG)
    rank = jnp.zeros(imp.shape, F32)
    for j in range(n_blk):
        row = score[j:j + 1, :]
        ahead = (row > score) | ((row == score) & (jb > j))
        rank = rank + jnp.where(ahead, 1.0, 0.0)
    return jnp.where(rank < float(NSA_TOP_N), 1.0, 0.0)


def _nsa_cmp_kernel_t(q_ref, kc_ref, vct_ref, gate_ref, o_ref, sel_ref, *, tq, n_cmp):
    qi = pl.program_id(2)
    g = pl.program_id(1)
    q_t = (_stack_heads(q_ref, tq) * (HEAD_DIM ** -0.5)).astype(BF16)
    lc = jnp.dot(kc_ref[...].astype(BF16), q_t, preferred_element_type=F32)
    qpos = qi * tq + lax.broadcasted_iota(jnp.int32, (1, tq), 1)
    blk = lax.broadcasted_iota(jnp.int32, (n_cmp, tq), 0)
    keep = jnp.where(blk < ((qpos + 1) >> BLOCK_SHIFT), 1.0, 0.0)
    keep4 = jnp.concatenate([keep] * NSA_HPG, axis=1)
    lc = jnp.where(keep4 > 0.5, lc, NEG)
    p = jnp.exp(lc - jnp.max(lc, axis=0, keepdims=True)) * keep4
    pc = p / jnp.maximum(jnp.sum(p, axis=0, keepdims=True), 1e-30)
    oc = jnp.dot(vct_ref[...].astype(BF16), pc.astype(BF16), preferred_element_type=F32)
    imp = pc[:, 0:tq]
    for h in range(1, NSA_HPG):
        imp = imp + pc[:, h * tq:(h + 1) * tq]
    sel_ref[...] = _select_mask_t(imp, qpos, n_cmp)
    gates = gate_ref[...]
    for h in range(NSA_HPG):
        grow = jnp.where(g == 0, gates[h:h + 1, :], gates[NSA_HPG + h:NSA_HPG + h + 1, :])
        o_ref[:, h * 64:(h + 1) * 64] = (oc[:, h * tq:(h + 1) * tq] * grow).T


def _nsa_cmp_t(pt, kc, vct, tq):
    nb, _, nt = pt.shape
    n_cmp = kc.shape[2]
    return pl.pallas_call(
        functools.partial(_nsa_cmp_kernel_t, tq=tq, n_cmp=n_cmp),
        grid=(nb, NSA_KV_GROUPS, nt // tq),
        in_specs=[
            pl.BlockSpec((None, 256, tq), lambda b, g, i: (b, R_NQ // 256 + g, i)),
            pl.BlockSpec((None, None, n_cmp, HEAD_DIM), lambda b, g, i: (b, g, 0, 0)),
            pl.BlockSpec((None, None, HEAD_DIM, n_cmp), lambda b, g, i: (b, g, 0, 0)),
            pl.BlockSpec((None, 32, tq), lambda b, g, i: (b, R_GATE // 32, i)),
        ],
        out_specs=[
            pl.BlockSpec((None, tq, 256), lambda b, g, i: (b, i, g)),
            pl.BlockSpec((None, None, n_cmp, tq), lambda b, g, i: (b, g, 0, i)),
        ],
        out_shape=[jax.ShapeDtypeStruct((nb, nt, 512), F32),
                   jax.ShapeDtypeStruct((nb, NSA_KV_GROUPS, n_cmp, nt), F32)],
        compiler_params=_cparams(("parallel", "parallel", "parallel")),
        name="nsa_cmp",
    )(pt, kc, vct, pt)


def _nsa_band_kernel_t(qi_ref, kj_ref, first_ref, last_ref, q_ref, k_ref, v_ref, gate_ref, *rest,
                       tq, tk, selected, gate_base):
    if selected:
        sel_ref, o_ref, q_s, m_s, l_s, acc_s = rest
    else:
        o_ref, q_s, m_s, l_s, acc_s = rest
    p_id = pl.program_id(2)
    g = pl.program_id(1)
    qi = qi_ref[p_id]
    kj = kj_ref[p_id]

    @pl.when(first_ref[p_id] == 1)
    def _():
        q_s[...] = (_stack_heads(q_ref, tq) * (HEAD_DIM ** -0.5)).astype(BF16)
        m_s[...] = jnp.full(m_s.shape, NEG, F32)
        l_s[...] = jnp.zeros(l_s.shape, F32)
        acc_s[...] = jnp.zeros(acc_s.shape, F32)

    key, qry = _causal_keep(qi, kj, tq, tk)
    mask = key <= qry
    if selected:
        n_blk = sel_ref.shape[0]
        blk_of_key = kj * (tk // NSA_BLOCK) + (lax.broadcasted_iota(jnp.int32, (tk, n_blk), 0) >> BLOCK_SHIFT)
        expand = jnp.where(lax.broadcasted_iota(jnp.int32, (tk, n_blk), 1) == blk_of_key, 1.0, 0.0).astype(BF16)
        picked = jnp.dot(expand, sel_ref[...].astype(BF16), preferred_element_type=F32)
        mask = mask & (picked > 0.5)
    else:
        mask = mask & (key > qry - NSA_WINDOW)
    keep = jnp.where(mask, 1.0, 0.0)
    s = _kq(k_ref[...].astype(BF16), q_s[...])
    s = jnp.where(jnp.concatenate([keep] * NSA_HPG, axis=1) > 0.5, s, NEG)
    _flash_step(s, v_ref[...].astype(BF16), m_s, l_s, acc_s)

    @pl.when(last_ref[p_id] == 1)
    def _():
        gates = gate_ref[...]
        o = acc_s[...] / l_s[...]
        for h in range(NSA_HPG):
            r0 = gate_base + h
            grow = jnp.where(g == 0, gates[r0:r0 + 1, :], gates[r0 + NSA_HPG:r0 + NSA_HPG + 1, :])
            o_ref[:, h * 64:(h + 1) * 64] = (o[:, h * tq:(h + 1) * tq] * grow).T


def _band_pairs(nt, tq, tk, window):
    qi, kj = [], []
    for i in range(nt // tq):
        lo_key = 0 if window is None else max(0, i * tq - (window - 1))
        for j in range(lo_key // tk, ((i + 1) * tq - 1) // tk + 1):
            qi.append(i)
            kj.append(j)
    first = [1 if (p == 0 or qi[p] != qi[p - 1]) else 0 for p in range(len(qi))]
    last = first[1:] + [1]
    return tuple(jnp.asarray(a, jnp.int32) for a in (qi, kj, first, last))


def _nsa_band_t(pt, sel, tq, tk, selected):
    nb, _, nt = pt.shape
    qi, kj, first, last = _band_pairs(nt, tq, tk, None if selected else NSA_WINDOW)
    if selected:
        rk, rv, gate_base = R_NSA + 256, R_NSA + 384, 8
    else:
        rk, rv, gate_base = R_WIN, R_WIN + 128, 16
    idx = lambda f: (lambda b, g, p, qi, kj, fi, la: f(b, g, p, qi, kj))
    in_specs = [
        pl.BlockSpec((None, 256, tq), idx(lambda b, g, p, qi, kj: (b, R_NQ // 256 + g, qi[p]))),
        pl.BlockSpec((None, HEAD_DIM, tk), idx(lambda b, g, p, qi, kj: (b, rk // 64 + g, kj[p]))),
        pl.BlockSpec((None, HEAD_DIM, tk), idx(lambda b, g, p, qi, kj: (b, rv // 64 + g, kj[p]))),
        pl.BlockSpec((None, 32, tq), idx(lambda b, g, p, qi, kj: (b, R_GATE // 32, qi[p]))),
    ]
    args = [pt, pt, pt, pt]
    if selected:
        in_specs.append(pl.BlockSpec((None, None, sel.shape[2], tq),
                                     idx(lambda b, g, p, qi, kj: (b, g, 0, qi[p]))))
        args.append(sel)
    grid_spec = pltpu.PrefetchScalarGridSpec(
        num_scalar_prefetch=4,
        grid=(nb, NSA_KV_GROUPS, qi.shape[0]),
        in_specs=in_specs,
        out_specs=pl.BlockSpec((None, tq, 256), idx(lambda b, g, p, qi, kj: (b, qi[p], g))),
        scratch_shapes=[
            pltpu.VMEM((HEAD_DIM, NSA_HPG * tq), BF16),
            pltpu.VMEM((1, NSA_HPG * tq), F32),
            pltpu.VMEM((1, NSA_HPG * tq), F32),
            pltpu.VMEM((HEAD_DIM, NSA_HPG * tq), F32),
        ],
    )
    return pl.pallas_call(
        functools.partial(_nsa_band_kernel_t, tq=tq, tk=tk, selected=selected, gate_base=gate_base),
        grid_spec=grid_spec,
        out_shape=jax.ShapeDtypeStruct((nb, nt, 512), F32),
        compiler_params=_cparams(("parallel", "parallel", "arbitrary")),
        name="nsa_sel" if selected else "nsa_win",
    )(qi, kj, first, last, *args)


def _layer_norm_rows(x, g, b):
    mu = jnp.mean(x, axis=-1, keepdims=True)
    xc = x - mu
    var = jnp.mean(xc * xc, axis=-1, keepdims=True)
    return xc * lax.rsqrt(var + LN_EPS) * g + b


def _outproj_kernel(fox_ref, n1_ref, n2_ref, n3_ref, diff_ref, h_ref, w_ref, g_ref, b_ref, wr_ref, br_ref,
                    o_ref, lg_ref):
    nsa = (n1_ref[...] + n2_ref[...] + n3_ref[...]).astype(BF16)
    y = jnp.dot(fox_ref[...].astype(BF16), w_ref[0:256, :], preferred_element_type=F32)
    y = y + jnp.dot(nsa, w_ref[256:768, :], preferred_element_type=F32)
    y = y + jnp.dot(diff_ref[...].astype(BF16), w_ref[768:1024, :], preferred_element_type=F32)
    h1 = _layer_norm_rows(DEEPNORM_ALPHA * h_ref[...] + y, g_ref[...], b_ref[...])
    o_ref[...] = h1
    lg_ref[...] = jnp.dot(h1.astype(BF16), wr_ref[...], preferred_element_type=F32) + br_ref[...]


def _outproj(fox, n1, n2, n3, diff, h, w, g, b, wr, br, tm):
    m = h.shape[0]
    row = lambda i: (i, 0)
    const = lambda i: (0, 0)
    return pl.pallas_call(
        _outproj_kernel,
        grid=(m // tm,),
        in_specs=[
            pl.BlockSpec((tm, 256), row), pl.BlockSpec((tm, 512), row), pl.BlockSpec((tm, 512), row),
            pl.BlockSpec((tm, 512), row), pl.BlockSpec((tm, 256), row), pl.BlockSpec((tm, D_MODEL), row),
            pl.BlockSpec((D_MODEL, D_MODEL), const), pl.BlockSpec((1, D_MODEL), const),
            pl.BlockSpec((1, D_MODEL), const), pl.BlockSpec((D_MODEL, LANES), const),
            pl.BlockSpec((1, LANES), const),
        ],
        out_specs=[pl.BlockSpec((tm, D_MODEL), row), pl.BlockSpec((tm, LANES), row)],
        out_shape=[jax.ShapeDtypeStruct((m, D_MODEL), F32), jax.ShapeDtypeStruct((m, LANES), F32)],
        compiler_params=_cparams(("parallel",)),
        name="outproj_ln",
    )(fox, n1, n2, n3, diff, h, w, g, b, wr, br)


PERM_CHUNK = 256


def _deinterleave_kernel(w_ref, o_ref):
    half = w_ref.shape[1] // 2
    hc = PERM_CHUNK // 2
    src = lax.broadcasted_iota(jnp.int32, (PERM_CHUNK, PERM_CHUNK), 0)
    dst = lax.broadcasted_iota(jnp.int32, (PERM_CHUNK, PERM_CHUNK), 1)
    want = jnp.where(dst < hc, 2 * dst, 2 * (dst - hc) + 1)
    perm = jnp.where(src == want, 1.0, 0.0).astype(BF16)
    for c in range(w_ref.shape[1] // PERM_CHUNK):
        y = jnp.dot(w_ref[:, c * PERM_CHUNK:(c + 1) * PERM_CHUNK].astype(BF16), perm, preferred_element_type=F32)
        o_ref[:, c * hc:(c + 1) * hc] = y[:, :hc].astype(BF16)
        o_ref[:, half + c * hc:half + (c + 1) * hc] = y[:, hc:].astype(BF16)


def _deinterleave(w, tr):
    ne, k, n2 = w.shape
    return pl.pallas_call(
        _deinterleave_kernel,
        grid=(ne, k // tr),
        in_specs=[pl.BlockSpec((None, tr, n2), lambda e, i: (e, i, 0))],
        out_specs=pl.BlockSpec((None, tr, n2), lambda e, i: (e, i, 0)),
        out_shape=jax.ShapeDtypeStruct((ne, k, n2), BF16),
        compiler_params=_cparams(("parallel", "parallel")),
        name="moe_deinterleave",
    )(w)


def _moe_kernel(te_ref, nv_ref, x_ref, wu_ref, bg_ref, bl_ref, wd_ref, bd_ref, o_ref):
    i = pl.program_id(0)

    @pl.when(i < nv_ref[0])
    def _():
        hcat = jnp.dot(x_ref[...].astype(BF16), wu_ref[...], preferred_element_type=F32)
        hg = hcat[:, :D_MODEL] + bg_ref[...]
        hl = hcat[:, D_MODEL:] + bl_ref[...]
        glu = jnp.minimum(hg, SWIGLU_LIMIT)
        lin = jnp.clip(hl, -SWIGLU_LIMIT, SWIGLU_LIMIT)
        act = glu * (1.0 / (1.0 + jnp.exp(-SWIGLU_ALPHA * glu))) * (lin + 1.0)
        o_ref[...] = jnp.dot(act.astype(BF16), wd_ref[...], preferred_element_type=F32) + bd_ref[...]

    @pl.when(i >= nv_ref[0])
    def _():
        o_ref[...] = jnp.zeros(o_ref.shape, F32)


def _moe_experts(tile_e, n_valid, xs, wu, bg, bl, wd, bd, tm):
    n_rows = xs.shape[0]
    ex = lambda i, te, nv: (te[i], 0, 0)
    grid_spec = pltpu.PrefetchScalarGridSpec(
        num_scalar_prefetch=2,
        grid=(n_rows // tm,),
        in_specs=[
            pl.BlockSpec((tm, D_MODEL), lambda i, te, nv: (i, 0)),
            pl.BlockSpec((None, D_MODEL, 2 * D_MODEL), ex),
            pl.BlockSpec((None, 1, D_MODEL), ex),
            pl.BlockSpec((None, 1, D_MODEL), ex),
            pl.BlockSpec((None, D_MODEL, D_MODEL), ex),
            pl.BlockSpec((None, 1, D_MODEL), ex),
        ],
        out_specs=pl.BlockSpec((tm, D_MODEL), lambda i, te, nv: (i, 0)),
    )
    return pl.pallas_call(
        _moe_kernel,
        grid_spec=grid_spec,
        out_shape=jax.ShapeDtypeStruct((n_rows, D_MODEL), F32),
        compiler_params=_cparams(("arbitrary",)),
        name="moe_experts",
    )(tile_e, n_valid, xs, wu, bg, bl, wd, bd)


def _combine_kernel(h_ref, y_ref, gate_ref, g_ref, b_ref, o_ref):
    gates = gate_ref[...]
    f = y_ref[0] * gates[:, 0:1]
    for k in range(1, TOP_K):
        f = f + y_ref[k] * gates[:, k:k + 1]
    o_ref[...] = _layer_norm_rows(DEEPNORM_ALPHA * h_ref[...] + f, g_ref[...], b_ref[...])


def _combine_ln(h1, yg, gates, g, b, tm):
    m = h1.shape[0]
    row = lambda i: (i, 0)
    const = lambda i: (0, 0)
    return pl.pallas_call(
        _combine_kernel,
        grid=(m // tm,),
        in_specs=[
            pl.BlockSpec((tm, D_MODEL), row),
            pl.BlockSpec((TOP_K, tm, D_MODEL), lambda i: (0, i, 0)),
            pl.BlockSpec((tm, TOP_K), row),
            pl.BlockSpec((1, D_MODEL), const), pl.BlockSpec((1, D_MODEL), const),
        ],
        out_specs=pl.BlockSpec((tm, D_MODEL), row),
        out_shape=jax.ShapeDtypeStruct((m, D_MODEL), F32),
        compiler_params=_cparams(("parallel",)),
        name="moe_combine_ln",
    )(h1, yg, gates, g, b)


def _moe_block(h1, logits, wts, ln_g, ln_b, tm_moe, tm_row):
    wu, bg, bl, wd, bd = wts
    m = h1.shape[0]
    top_val, top_idx = lax.top_k(logits, TOP_K)
    gate = jax.nn.softmax(top_val, axis=-1)
    n = m * TOP_K
    flat_e = top_idx.reshape(n)
    order = jnp.argsort(flat_e)
    sorted_e = flat_e[order]
    counts = jnp.bincount(flat_e, length=N_EXPERTS)
    padded = (counts + tm_moe - 1) // tm_moe * tm_moe
    starts = jnp.cumsum(counts) - counts
    pends = jnp.cumsum(padded)
    dest = ((pends - padded)[sorted_e] + jnp.arange(n) - starts[sorted_e]).astype(jnp.int32)
    n_rows = -(-n // tm_moe) * tm_moe + N_EXPERTS * tm_moe
    n_tiles = n_rows // tm_moe
    row_tok = jnp.full((n_rows,), m, jnp.int32).at[dest].set((order // TOP_K).astype(jnp.int32))
    tile_e = jnp.minimum(jnp.searchsorted(pends, jnp.arange(n_tiles) * tm_moe, side='right'),
                         N_EXPERTS - 1).astype(jnp.int32)
    n_valid = (pends[-1] // tm_moe).astype(jnp.int32).reshape(1)
    pos = jnp.zeros((n,), jnp.int32).at[order].set(dest).reshape(m, TOP_K)
    xs = jnp.concatenate([h1, jnp.zeros((1, D_MODEL), F32)], 0)[row_tok]
    out = _moe_experts(tile_e, n_valid, xs, wu, bg, bl, wd, bd, tm_moe)
    yg = out[pos.T]
    return _combine_ln(h1, yg, gate, ln_g, ln_b, tm_row)


def _tree_sum(xs):
    xs = list(xs)
    while len(xs) > 1:
        xs = [xs[i] + xs[i + 1] for i in range(0, len(xs) - 1, 2)] + ([xs[-1]] if len(xs) % 2 else [])
    return xs[0]


def _sublane_allsum(x):
    x = x + pltpu.roll(x, 4, 0)
    x = x + pltpu.roll(x, 2, 0)
    return x + pltpu.roll(x, 1, 0)


def _lane_suffix_scan(x):
    lane = lax.broadcasted_iota(jnp.int32, x.shape, 1)
    k = 1
    while k < LANES:
        x = x + jnp.where(lane + k < LANES, pltpu.roll(x, LANES - k, 1), 0.0)
        k *= 2
    return x


def _stream_update(logits, m_ref, l_ref, valid=None):
    m_prev = m_ref[...]
    m_new = jnp.maximum(m_prev, logits)
    alpha = jnp.exp(m_prev - m_new)
    p = jnp.exp(logits - m_new)
    if valid is not None:
        p = jnp.where(valid, p, 0.0)
    l_ref[...] = l_ref[...] * alpha + p
    m_ref[...] = m_new
    return alpha, p


def _lane_merge(m_ref, l_ref):
    m = m_ref[...]
    w = jnp.exp(m - jnp.max(m, axis=1, keepdims=True))
    tot = jnp.sum(l_ref[...] * w, axis=1, keepdims=True)
    return w, 1.0 / tot


FOX_WIN = 520
FOX_VROW0 = 256


def _fox_sample_kernel(pt_ref, *refs, kpages, r0):
    pages = refs[:2 * kpages]
    qc_ref, new_ref, o_ref, qcb_s, m_s, l_s, s_s, acc_s, scan_s = refs[2 * kpages:]
    j = pl.program_id(1)
    sub = lax.broadcasted_iota(jnp.int32, (SUBLANES, LANES), 0)
    lane = lax.broadcasted_iota(jnp.int32, (SUBLANES, LANES), 1)
    lf_rows = (sub >= r0) & (sub < r0 + FOX_HEADS)
    n_kv = FOX_VROW0 // SUBLANES + 1

    def page(main, tail, valid):
        t = _tree_sum([main(i) * qcb_s[8 * i:8 * i + 8, :] for i in range(n_kv)])
        qk = t + pltpu.roll(t, 4, 0)
        lf = jnp.where(lf_rows, tail, 0.0)
        lf = lf + pltpu.roll(lf, 4, 0)
        if valid is not None:
            lf = jnp.where(valid, lf, 0.0)
        parts = jnp.concatenate([t.astype(F32) for t in _split3(lf)], axis=0).astype(BF16)
        scan = jnp.dot(parts, scan_s[...], preferred_element_type=F32)
        scan = scan[0:8, :] + scan[8:16, :] + scan[16:24, :]
        later, total = scan[:, 0:LANES], scan[:, LANES:2 * LANES]
        logits = qk + s_s[...] + later
        if valid is not None:
            logits = jnp.where(valid, logits, NEG)
        alpha, p = _stream_update(logits, m_s, l_s, valid)
        n_main = (FOX_WIN - SUBLANES - FOX_VROW0) // SUBLANES
        for i in range(n_main):
            r = 8 * i
            acc_s[r:r + 8, :] = acc_s[r:r + 8, :] * alpha + main(FOX_VROW0 // SUBLANES + i) * p
        r = 8 * n_main
        acc_s[r:r + 8, :] = acc_s[r:r + 8, :] * alpha + tail * p
        s_s[...] = s_s[...] + total

    @pl.when(j == 0)
    def _():
        src = lax.broadcasted_iota(jnp.int32, (LANES, 2 * LANES), 0)
        dst = lax.broadcasted_iota(jnp.int32, (LANES, 2 * LANES), 1)
        scan_s[...] = jnp.where((dst >= LANES) | (src > dst), 1.0, 0.0).astype(BF16)
        qcb_s[...] = jnp.broadcast_to(qc_ref[...], qcb_s.shape)
        m_s[...] = jnp.full(m_s.shape, NEG, F32)
        l_s[...] = jnp.zeros(l_s.shape, F32)
        s_s[...] = jnp.zeros(s_s.shape, F32)
        acc_s[...] = jnp.zeros(acc_s.shape, F32)
        page(lambda i: jnp.broadcast_to(new_ref[8 * i:8 * i + 8, :], (SUBLANES, LANES)),
             jnp.broadcast_to(new_ref[FOX_WIN - 8:FOX_WIN, :], (SUBLANES, LANES)), lane == 0)

    for k in range(kpages):
        main_ref, tail_ref = pages[2 * k], pages[2 * k + 1]
        page(lambda i, ref=main_ref: ref[8 * i:8 * i + 8, :], tail_ref[...], None)

    @pl.when(j == pl.num_programs(1) - 1)
    def _():
        w, inv = _lane_merge(m_s, l_s)
        for i in range((FOX_WIN - FOX_VROW0) // SUBLANES):
            r = 8 * i
            o_ref[r:r + 8, :] = jnp.sum(acc_s[r:r + 8, :] * w, axis=1, keepdims=True) * inv


def _fox_sample_attn(cache_fm, page_table, layer, qcol, newcol, kpages):
    nb, n_pages = page_table.shape
    r0 = 4 * layer
    n_chunks = n_pages // kpages

    def page_idx(b, j, pt, k):
        return pt[b, n_pages - 1 - (j * kpages + k)]

    in_specs = []
    for k in range(kpages):
        in_specs.append(pl.BlockSpec((None, 512, LANES), lambda b, j, pt, k=k: (page_idx(b, j, pt, k), layer, 0)))
        in_specs.append(pl.BlockSpec((None, 8, LANES), lambda b, j, pt, k=k: (page_idx(b, j, pt, k), 64 * (layer + 1), 0)))
    in_specs += [pl.BlockSpec((None, FOX_WIN, 1), lambda b, j, pt: (b, 0, 0)),
                 pl.BlockSpec((None, FOX_WIN, 1), lambda b, j, pt: (b, 0, 0))]
    n_out = FOX_WIN - FOX_VROW0
    grid_spec = pltpu.PrefetchScalarGridSpec(
        num_scalar_prefetch=1,
        grid=(nb, n_chunks),
        in_specs=in_specs,
        out_specs=pl.BlockSpec((None, n_out, 1), lambda b, j, pt: (b, 0, 0)),
        scratch_shapes=[
            pltpu.VMEM((FOX_WIN, LANES), F32),
            pltpu.VMEM((SUBLANES, LANES), F32), pltpu.VMEM((SUBLANES, LANES), F32), pltpu.VMEM((SUBLANES, LANES), F32),
            pltpu.VMEM((n_out, LANES), F32),
            pltpu.VMEM((LANES, 2 * LANES), BF16),
        ],
    )
    return pl.pallas_call(
        functools.partial(_fox_sample_kernel, kpages=kpages, r0=r0),
        grid_spec=grid_spec,
        out_shape=jax.ShapeDtypeStruct((nb, n_out, 1), F32),
        compiler_params=_cparams(("parallel", "arbitrary")),
        name="fox_sample",
    )(page_table, *([cache_fm] * (2 * kpages)), qcol, newcol)


def _bcast_row(x, s):
    return jnp.broadcast_to(x[s:s + 1, :], x.shape)


def _pack_streams(ts):
    sub = lax.broadcasted_iota(jnp.int32, ts[0].shape, 0)
    out = ts[0]
    for s in range(1, len(ts)):
        out = jnp.where(sub == s, ts[s], out)
    return out


def _kv_page_update(x, qcb_s, k_rows, v_rows, m_s, l_s, acc_s, valid):
    ts = []
    for s, (g0, cnt, c0) in enumerate(k_rows):
        t = _tree_sum([x(g0 + i) * qcb_s[c0 + 8 * i:c0 + 8 * i + 8, :] for i in range(cnt)])
        ts.append(_sublane_allsum(t))
    logits = _pack_streams(ts)
    if valid is not None:
        logits = jnp.where(valid, logits, NEG)
    alpha, p = _stream_update(logits, m_s, l_s, valid)
    for s, g0 in enumerate(v_rows):
        ab, pb = _bcast_row(alpha, s), _bcast_row(p, s)
        for i in range(8):
            r = s * 64 + 8 * i
            acc_s[r:r + 8, :] = acc_s[r:r + 8, :] * ab + x(g0 + i) * pb


def _merged_rows(acc_s, w, inv, s):
    wb = _bcast_row(w, s)
    return [jnp.sum(acc_s[s * 64 + 8 * i:s * 64 + 8 * i + 8, :] * wb, axis=1, keepdims=True) * inv[s:s + 1, :]
            for i in range(8)]


_DIFF_K_ROWS = tuple((16 * (s // 2) + 4 * (s % 2), 4, 128 * (s // 2) + 32 * (s % 2)) for s in range(8))
_DIFF_V_ROWS = tuple(16 * (s // 2) + 8 for s in range(8))


def _diff_sample_kernel(pt_ref, lam_ref, *refs, kpages, out_scale):
    pages = refs[:kpages]
    qc_ref, new_ref, g_ref, o_ref, qcb_s, m_s, l_s, acc_s = refs[kpages:]
    j = pl.program_id(1)
    lane = lax.broadcasted_iota(jnp.int32, (SUBLANES, LANES), 1)

    @pl.when(j == 0)
    def _():
        qcb_s[...] = jnp.broadcast_to(qc_ref[...], qcb_s.shape)
        m_s[...] = jnp.full(m_s.shape, NEG, F32)
        l_s[...] = jnp.zeros(l_s.shape, F32)
        acc_s[...] = jnp.zeros(acc_s.shape, F32)
        _kv_page_update(lambda i: jnp.broadcast_to(new_ref[8 * i:8 * i + 8, :], (SUBLANES, LANES)),
                        qcb_s, _DIFF_K_ROWS, _DIFF_V_ROWS, m_s, l_s, acc_s, lane == 0)

    for k in range(kpages):
        _kv_page_update(lambda i, ref=pages[k]: ref[8 * i:8 * i + 8, :],
                        qcb_s, _DIFF_K_ROWS, _DIFF_V_ROWS, m_s, l_s, acc_s, None)

    @pl.when(j == pl.num_programs(1) - 1)
    def _():
        w, inv = _lane_merge(m_s, l_s)
        lam = lam_ref[0]
        for h in range(DIFF_HEADS):
            o1 = _merged_rows(acc_s, w, inv, 2 * h)
            o2 = _merged_rows(acc_s, w, inv, 2 * h + 1)
            o = [a - lam * b for a, b in zip(o1, o2)]
            ss = o[0] * o[0]
            for i in range(1, 8):
                ss = ss + o[i] * o[i]
            r = lax.rsqrt(_sublane_allsum(ss) * (1.0 / HEAD_DIM) + RMS_EPS)
            for i in range(8):
                o_ref[h * 64 + 8 * i:h * 64 + 8 * i + 8, :] = o[i] * r * g_ref[8 * i:8 * i + 8, :] * out_scale


def _diff_sample_attn(cache_fm, page_table, layer, lam, qcol, newcol, gcol, out_scale, kpages):
    nb, n_pages = page_table.shape
    in_specs = [pl.BlockSpec((None, 512, LANES), lambda b, j, pt, lam, k=k: (pt[b, j * kpages + k], layer, 0))
                for k in range(kpages)]
    in_specs += [pl.BlockSpec((None, 512, 1), lambda b, j, pt, lam: (b, 0, 0)),
                 pl.BlockSpec((None, 512, 1), lambda b, j, pt, lam: (b, 0, 0)),
                 pl.BlockSpec((HEAD_DIM, 1), lambda b, j, pt, lam: (0, 0))]
    grid_spec = pltpu.PrefetchScalarGridSpec(
        num_scalar_prefetch=2,
        grid=(nb, n_pages // kpages),
        in_specs=in_specs,
        out_specs=pl.BlockSpec((None, 256, 1), lambda b, j, pt, lam: (b, 0, 0)),
        scratch_shapes=[pltpu.VMEM((512, LANES), F32), pltpu.VMEM((SUBLANES, LANES), F32),
                        pltpu.VMEM((SUBLANES, LANES), F32), pltpu.VMEM((512, LANES), F32)],
    )
    return pl.pallas_call(
        functools.partial(_diff_sample_kernel, kpages=kpages, out_scale=out_scale),
        grid_spec=grid_spec,
        out_shape=jax.ShapeDtypeStruct((nb, 256, 1), F32),
        compiler_params=_cparams(("parallel", "arbitrary")),
        name="diff_sample",
    )(page_table, lam, *([cache_fm] * kpages), qcol, newcol, gcol)


def _nsa_compress_sample_kernel(pt_ref, *refs, kpages, n_pages):
    pages = refs[:kpages]
    w2_ref, pe2_ref, o_ref, buf_s = refs[kpages:]
    j = pl.program_id(1)
    chunk = jnp.stack([pages[k][...] for k in range(kpages)], axis=0)
    buf_s[:, pl.ds(pl.multiple_of(j * kpages, kpages), kpages), :] = pltpu.einshape("prl->rpl", chunk)

    @pl.when(j == pl.num_programs(1) - 1)
    def _():
        for t in range(2):
            acc = jnp.zeros((2 * n_pages, LANES), F32)
            for d in range(HEAD_DIM):
                x = jnp.concatenate([buf_s[t * 128 + g * 64 + d] for g in range(NSA_KV_GROUPS)], axis=0)
                x = (x + pe2_ref[t, d:d + 1, :]).astype(BF16)
                acc = acc + jnp.dot(x, w2_ref[t, d], preferred_element_type=F32)
            o_ref[t] = acc


def _nsa_compress_sample(cache_fm, page_table, layer, w2, pe2, kpages):
    nb, n_pages = page_table.shape
    in_specs = [pl.BlockSpec((None, 256, LANES), lambda b, j, pt, k=k: (pt[b, j * kpages + k], 2 * layer, 0))
                for k in range(kpages)]
    in_specs += [pl.BlockSpec((2, HEAD_DIM, LANES, LANES), lambda b, j, pt: (0, 0, 0, 0)),
                 pl.BlockSpec((2, HEAD_DIM, LANES), lambda b, j, pt: (0, 0, 0))]
    grid_spec = pltpu.PrefetchScalarGridSpec(
        num_scalar_prefetch=1,
        grid=(nb, n_pages // kpages),
        in_specs=in_specs,
        out_specs=pl.BlockSpec((None, 2, 2 * n_pages, LANES), lambda b, j, pt: (b, 0, 0, 0)),
        scratch_shapes=[pltpu.VMEM((256, n_pages, LANES), F32)],
    )
    return pl.pallas_call(
        functools.partial(_nsa_compress_sample_kernel, kpages=kpages, n_pages=n_pages),
        grid_spec=grid_spec,
        out_shape=jax.ShapeDtypeStruct((nb, 2, 2 * n_pages, LANES), F32),
        compiler_params=_cparams(("parallel", "arbitrary")),
        name="nsa_compress_sample",
    )(page_table, *([cache_fm] * kpages), w2, pe2)


_NSA_K_ROWS = tuple((8 * (s // NSA_HPG), 8, 64 * s) for s in range(NSA_HEADS))
_NSA_V_ROWS = tuple(16 + 8 * (s // NSA_HPG) for s in range(NSA_HEADS))


def _col_to_row(col):
    n = col.shape[0]
    eye = lax.broadcasted_iota(jnp.int32, (n, n), 0) == lax.broadcasted_iota(jnp.int32, (n, n), 1)
    return jnp.sum(jnp.where(eye, col, 0.0), axis=0, keepdims=True)


def _nsa_sample_kernel(pt_ref, *refs, kpages, n_pages):
    pages = refs[:kpages]
    (kv_ref, qh_ref, qc_ref, newsel_ref, newwin_ref, win_ref, g0_ref, g12_ref,
     oc_ref, osw_ref, wout_ref,
     qcb_s, sel_s, m_s, l_s, acc_s, mw_s, lw_s, accw_s) = refs[kpages:]
    j = pl.program_id(1)
    lane = lax.broadcasted_iota(jnp.int32, (SUBLANES, LANES), 1)
    scale = HEAD_DIM ** -0.5
    n_blk = 2 * n_pages

    @pl.when(j == 0)
    def _():
        qcb_s[...] = jnp.broadcast_to(qc_ref[...], qcb_s.shape)
        for ref in (m_s, mw_s):
            ref[...] = jnp.full(ref.shape, NEG, F32)
        for ref in (l_s, acc_s, lw_s, accw_s):
            ref[...] = jnp.zeros(ref.shape, F32)
        imp = [jnp.zeros((n_pages, 1), F32), jnp.zeros((n_pages, 1), F32)]
        for g in range(NSA_KV_GROUPS):
            kc = kv_ref[0, g * n_pages:(g + 1) * n_pages, :].astype(BF16)
            vc = kv_ref[1, g * n_pages:(g + 1) * n_pages, :].astype(BF16)
            lc = [jnp.dot(kc, qh_ref[g, hf].astype(BF16), preferred_element_type=F32) * scale for hf in range(2)]
            mx = jnp.maximum(jnp.max(lc[0], axis=0, keepdims=True), jnp.max(lc[1], axis=0, keepdims=True))
            pe = [jnp.exp(c - mx) for c in lc]
            den = jnp.maximum(jnp.sum(pe[0], axis=0, keepdims=True) + jnp.sum(pe[1], axis=0, keepdims=True), 1e-30)
            pc = [e / den for e in pe]
            res = [lax.dot_general(c.astype(BF16), vc, (((0,), (0,)), ((), ())), preferred_element_type=F32)
                   for c in pc]
            oc = res[0][:, 0:HEAD_DIM] + res[1][:, HEAD_DIM:2 * HEAD_DIM]
            oc_ref[g * NSA_HPG:(g + 1) * NSA_HPG, :] = oc * g0_ref[g * NSA_HPG:(g + 1) * NSA_HPG, :]
            for hf in range(2):
                imp[hf] = imp[hf] + jnp.sum(pc[hf], axis=1, keepdims=True)
            pidx = lax.broadcasted_iota(jnp.int32, (n_pages, 1), 0)
            score = [jnp.where(pidx == 0, FORCE_SCORE, imp[0]), jnp.where(pidx == n_pages - 1, FORCE_SCORE, imp[1])]
            rows = [_col_to_row(c) for c in score]
            prow = lax.broadcasted_iota(jnp.int32, (n_pages, n_pages), 1)
            pcol = lax.broadcasted_iota(jnp.int32, (n_pages, n_pages), 0)
            sel = []
            for hf in range(2):
                cnt = jnp.zeros((n_pages, 1), F32)
                for hf2 in range(2):
                    before = (2 * prow + hf2) < (2 * pcol + hf)
                    ahead = (rows[hf2] > score[hf]) | ((rows[hf2] == score[hf]) & before)
                    cnt = cnt + jnp.sum(jnp.where(ahead, 1.0, 0.0), axis=1, keepdims=True)
                cnt = cnt + jnp.where(score[hf] < FORCE_SCORE, 1.0, 0.0)
                sel.append(jnp.where(cnt < float(NSA_TOP_N), 1.0, 0.0))
            lane_p = lax.broadcasted_iota(jnp.int32, (n_pages, LANES), 1)
            sel_s[g] = jnp.where(lane_p < NSA_BLOCK, sel[0], sel[1])
            imp = [jnp.zeros((n_pages, 1), F32), jnp.zeros((n_pages, 1), F32)]
        _kv_page_update(lambda i: jnp.broadcast_to(newsel_ref[8 * i:8 * i + 8, :], (SUBLANES, LANES)),
                        qcb_s, _NSA_K_ROWS, _NSA_V_ROWS, m_s, l_s, acc_s, lane == 0)
        _kv_page_update(lambda i: jnp.broadcast_to(newwin_ref[8 * i:8 * i + 8, :], (SUBLANES, LANES)),
                        qcb_s, _NSA_K_ROWS, _NSA_V_ROWS, mw_s, lw_s, accw_s, lane == 0)
        for c in range(NSA_WINDOW // LANES):
            valid = (lane >= 1) if c == 0 else None
            _kv_page_update(lambda i, c=c: win_ref[8 * i:8 * i + 8, c * LANES:(c + 1) * LANES],
                            qcb_s, _NSA_K_ROWS, _NSA_V_ROWS, mw_s, lw_s, accw_s, valid)
        lane_w = lax.broadcasted_iota(jnp.int32, (256, NSA_WINDOW), 1)
        shifted = pltpu.roll(win_ref[...], NSA_WINDOW - 1, 1)
        wout_ref[...] = jnp.where(lane_w == NSA_WINDOW - 1, newwin_ref[...], shifted)

    sub = lax.broadcasted_iota(jnp.int32, (SUBLANES, LANES), 0)
    for k in range(kpages):
        p_idx = j * kpages + k
        picked = jnp.where(sub < NSA_HPG, jnp.broadcast_to(sel_s[0, pl.ds(p_idx, 1), :], (SUBLANES, LANES)),
                           jnp.broadcast_to(sel_s[1, pl.ds(p_idx, 1), :], (SUBLANES, LANES)))
        _kv_page_update(lambda i, ref=pages[k]: ref[8 * i:8 * i + 8, :],
                        qcb_s, _NSA_K_ROWS, _NSA_V_ROWS, m_s, l_s, acc_s, picked > 0.5)

    @pl.when(j == pl.num_programs(1) - 1)
    def _():
        w, inv = _lane_merge(m_s, l_s)
        ww, invw = _lane_merge(mw_s, lw_s)
        for s in range(NSA_HEADS):
            o_sel = _merged_rows(acc_s, w, inv, s)
            o_win = _merged_rows(accw_s, ww, invw, s)
            for i in range(8):
                r = s * 64 + 8 * i
                osw_ref[r:r + 8, :] = o_sel[i] * g12_ref[0, r:r + 8, :] + o_win[i] * g12_ref[1, r:r + 8, :]


def _nsa_sample_attn(cache_fm, win_fm, page_table, layer, kv, qh, qcol, newsel, newwin, g0, g12, kpages):
    nb, n_pages = page_table.shape
    c3 = lambda b, j, pt: (b, 0, 0)
    c4 = lambda b, j, pt: (b, 0, 0, 0)
    in_specs = [pl.BlockSpec((None, 256, LANES), lambda b, j, pt, k=k: (pt[b, j * kpages + k], 2 * layer + 1, 0))
                for k in range(kpages)]
    in_specs += [
        pl.BlockSpec((None, 2, 2 * n_pages, LANES), c4),
        pl.BlockSpec((None, NSA_KV_GROUPS, 2, LANES, NSA_HPG), lambda b, j, pt: (b, 0, 0, 0, 0)),
        pl.BlockSpec((None, 512, 1), c3),
        pl.BlockSpec((None, 256, 1), c3),
        pl.BlockSpec((None, 256, 1), c3),
        pl.BlockSpec((None, None, 256, NSA_WINDOW), lambda b, j, pt: (b, layer, 0, 0)),
        pl.BlockSpec((None, NSA_HEADS, HEAD_DIM), c3),
        pl.BlockSpec((None, 2, 512, 1), c4),
    ]
    grid_spec = pltpu.PrefetchScalarGridSpec(
        num_scalar_prefetch=1,
        grid=(nb, n_pages // kpages),
        in_specs=in_specs,
        out_specs=[pl.BlockSpec((None, NSA_HEADS, HEAD_DIM), c3),
                   pl.BlockSpec((None, 512, 1), c3),
                   pl.BlockSpec((None, 256, NSA_WINDOW), c3)],
        scratch_shapes=[
            pltpu.VMEM((512, LANES), F32), pltpu.VMEM((NSA_KV_GROUPS, n_pages, LANES), F32),
            pltpu.VMEM((SUBLANES, LANES), F32), pltpu.VMEM((SUBLANES, LANES), F32), pltpu.VMEM((512, LANES), F32),
            pltpu.VMEM((SUBLANES, LANES), F32), pltpu.VMEM((SUBLANES, LANES), F32), pltpu.VMEM((512, LANES), F32),
        ],
    )
    return pl.pallas_call(
        functools.partial(_nsa_sample_kernel, kpages=kpages, n_pages=n_pages),
        grid_spec=grid_spec,
        out_shape=[jax.ShapeDtypeStruct((nb, NSA_HEADS, HEAD_DIM), F32),
                   jax.ShapeDtypeStruct((nb, 512, 1), F32),
                   jax.ShapeDtypeStruct((nb, 256, NSA_WINDOW), F32)],
        compiler_params=_cparams(("parallel", "arbitrary")),
        name="nsa_sample",
    )(page_table, *([cache_fm] * kpages), kv, qh, qcol, newsel, newwin, win_fm, g0, g12)


SampleProj = collections.namedtuple(
    "SampleProj", ["fq", "fox_rows", "nq", "ng", "nsa_rows", "win_rows", "dq", "diff_rows"])


def _fox_cache_view(c):
    return c.transpose(0, 1, 4, 3, 2).reshape(c.shape[0], 2 * 516, PAGE_SIZE)


def _kv_cache_view(c):
    return c.transpose(0, 1, 3, 4, 5, 2).reshape(c.shape[0], 2 * 512, PAGE_SIZE)


def _win_state_view(s):
    return s.transpose(0, 1, 3, 4, 5, 2).reshape(s.shape[0], DEPTH, 256, NSA_WINDOW)


def _fox_sample(cache_fm, page_table, layer, sp, kpages):
    nb = sp.fq.shape[0]
    r0 = 4 * layer
    qk = (sp.fq.reshape(nb, FOX_HEADS, HEAD_DIM) * (HEAD_DIM ** -0.5)).transpose(0, 2, 1).reshape(nb, 256)
    qcol = jnp.zeros((nb, FOX_WIN), F32).at[:, r0:r0 + 256].set(qk)
    rows = sp.fox_rows.reshape(nb, FOX_HEADS, 129).transpose(0, 2, 1).reshape(nb, 516)
    newcol = jnp.zeros((nb, FOX_WIN), F32).at[:, r0:r0 + 516].set(rows)
    o = _fox_sample_attn(cache_fm, page_table, layer, qcol[..., None], newcol[..., None], kpages)[..., 0]
    return o[:, r0:r0 + 256].reshape(nb, HEAD_DIM, FOX_HEADS).transpose(0, 2, 1).reshape(nb, 256)


def _diff_sample(cache_fm, page_table, layer, sp, lam, lam_init, g, kpages):
    nb = sp.dq.shape[0]
    q = sp.dq.reshape(nb, DIFF_HEADS, HEAD_DIM) * (DIFF_QK_DIM ** -0.5)
    qcol = jnp.concatenate([q, jnp.zeros_like(q)], axis=-1).reshape(nb, 512, 1)
    newcol = sp.diff_rows.reshape(nb, 512, 1)
    o = _diff_sample_attn(cache_fm, page_table, layer, lam.reshape(1).astype(F32), qcol, newcol,
                          g.reshape(HEAD_DIM, 1), 1.0 - lam_init, kpages)
    return o.reshape(nb, 256)


def _nsa_sample(cache_fm, win_fm, page_table, layer, sp, cmp_pos, cmp_w, kpages):
    nb = sp.nq.shape[0]
    w = cmp_w.transpose(0, 2, 1, 3)
    z = jnp.zeros_like(w)
    w2 = jnp.concatenate([jnp.concatenate([w, z], -1), jnp.concatenate([z, w], -1)], -2).astype(BF16)
    pe = cmp_pos.transpose(0, 2, 1)
    pe2 = jnp.concatenate([pe, pe], -1)
    kv = _nsa_compress_sample(cache_fm, page_table, layer, w2, pe2, kpages)
    q = sp.nq.reshape(nb, NSA_KV_GROUPS, NSA_HPG, HEAD_DIM)
    qt = q.transpose(0, 1, 3, 2)
    zq = jnp.zeros_like(qt)
    qh = jnp.stack([jnp.concatenate([qt, zq], 2), jnp.concatenate([zq, qt], 2)], 2)
    qcol = (sp.nq.reshape(nb, 512) * (HEAD_DIM ** -0.5))[..., None]
    newsel = sp.nsa_rows.reshape(nb, 512)[:, 256:, None]
    newwin = sp.win_rows.reshape(nb, 256, 1)
    gates = sp.ng.reshape(nb, 3, NSA_HEADS)
    g0 = jnp.broadcast_to(gates[:, 0, :, None], (nb, NSA_HEADS, HEAD_DIM))
    g12 = jnp.broadcast_to(gates[:, 1:, :, None], (nb, 2, NSA_HEADS, HEAD_DIM)).reshape(nb, 2, 512, 1)
    oc, osw, wnew = _nsa_sample_attn(cache_fm, win_fm, page_table, layer,
                                     kv, qh, qcol, newsel, newwin, g0, g12, kpages)
    win_new = wnew.reshape(nb, 2, NSA_KV_GROUPS, HEAD_DIM, NSA_WINDOW).transpose(0, 4, 1, 2, 3)
    return oc.reshape(nb, 512) + osw.reshape(nb, 512), win_new


def _diff_lambda(lp, layer):
    lam_init = 0.8 - 0.6 * math.exp(-0.3 * layer)
    lp = lp.astype(F32)
    lam = jnp.exp(jnp.sum(lp[0] * lp[1])) - jnp.exp(jnp.sum(lp[2] * lp[3])) + lam_init
    return lam, lam_init


def _pad_rows(a, n):
    return jnp.concatenate([a, jnp.zeros((n - a.shape[0],) + a.shape[1:], a.dtype)], 0)


def kernel(x_prompt, x_sample, cache_fox, cache_nsa, cache_diff, state_nsa_win, page_table,
           w_in, b_forget, b_nsa_gate, nsa_cmp_pos, nsa_cmp_w, diff_lambda, diff_subln_g,
           w_out, ln_attn_g, ln_attn_b, w_router, b_router, w_up, b_up, w_down, b_down,
           ln_ffn_g, ln_ffn_b):
    nb, seq, _ = x_prompt.shape
    n_dec, n_new, _ = x_sample.shape
    past = page_table.shape[1] * cache_fox.shape[2]
    tabs_p = _rope_tables(jnp.arange(seq, dtype=F32))
    n_dec_pad = LANES
    tabs_s = _rope_tables(jnp.full((n_dec_pad,), past, F32))
    perm = jnp.asarray(np.maximum(_PERM, 0), jnp.int32)
    valid = jnp.asarray((_PERM >= 0).astype(np.float32))[:, None]

    assert n_new == 1 and past % NSA_BLOCK == 0 and state_nsa_win.shape[2] == NSA_WINDOW
    fox_fm = _fox_cache_view(cache_fox)
    nsa_fm = _kv_cache_view(cache_nsa)
    diff_fm = _kv_cache_view(cache_diff)
    win_fm = _win_state_view(state_nsa_win)

    hp = x_prompt
    hs = x_sample.reshape(n_dec, D_MODEL)
    m_p = nb * seq
    m_all = m_p + n_dec
    m_pad = -(-m_all // 512) * 512
    outs = {k: [] for k in ('fox_p', 'fox_s', 'nsa_p', 'nsa_s', 'diff_p', 'diff_s', 'win_p', 'win_s')}
    for l in range(DEPTH):
        lam, lam_init = _diff_lambda(diff_lambda[l], l)
        lam_arr = lam.reshape(1).astype(F32)
        wt = (jnp.take(w_in[l], perm, axis=1).T * valid).astype(BF16)
        wg_tok = jnp.pad(w_in[l][:, C_NG:C_NG + 24], ((0, 0), (0, LANES - 24))).astype(BF16)
        bgate_col = jnp.pad(b_nsa_gate[l], (0, 8)).reshape(32, 1)
        bgate_row = jnp.pad(b_nsa_gate[l], (0, LANES - 24)).reshape(1, LANES)
        bforget_col = jnp.pad(b_forget[l], (0, 4)).reshape(8, 1)
        w_out_b = w_out[l].astype(BF16)
        wr = jnp.pad(w_router[l], ((0, 0), (0, LANES - N_EXPERTS))).astype(BF16)
        br = jnp.pad(b_router[l], (0, LANES - N_EXPERTS)).reshape(1, LANES)
        moe_w = (_deinterleave(w_up[l], 512), b_up[l][:, None, 0::2], b_up[l][:, None, 1::2],
                 w_down[l].astype(BF16), b_down[l][:, None, :])
        g1, b1 = ln_attn_g[l].reshape(1, -1), ln_attn_b[l].reshape(1, -1)
        g2, b2 = ln_ffn_g[l].reshape(1, -1), ln_ffn_b[l].reshape(1, -1)

        pt, gates = _inproj(hp, wt, wg_tok, bgate_col, bforget_col, bgate_row, tabs_p, 256)
        outs['fox_p'].append(pt[:, R_FOX:R_FOX + 516].reshape(nb, 129, FOX_HEADS, seq).transpose(0, 3, 2, 1))
        outs['nsa_p'].append(pt[:, R_NSA:R_NSA + 512].reshape(nb, 4, NSA_KV_GROUPS, HEAD_DIM, seq)
                             .transpose(0, 4, 1, 2, 3))
        outs['diff_p'].append(pt[:, R_DIFF:R_DIFF + 512].reshape(nb, DIFF_HEADS, 2, HEAD_DIM, seq)
                              .transpose(0, 4, 1, 2, 3))
        outs['win_p'].append(pt[:, R_WIN:R_WIN + 256, seq - NSA_WINDOW:]
                             .reshape(nb, 2, NSA_KV_GROUPS, HEAD_DIM, NSA_WINDOW).transpose(0, 4, 1, 2, 3))
        c = jnp.cumsum(pt[:, R_LOGF:R_LOGF + FOX_HEADS, :], axis=-1)
        o_fox = _fox_attention_t(pt, c, 512)
        o_diff = _diff_attention_t(pt, lam_arr, diff_subln_g[l].reshape(HEAD_DIM, 1), 1.0 - lam_init, 512)
        n_cmp = seq // NSA_BLOCK
        kcv = pt[:, R_NSA:R_NSA + 256].reshape(nb, 2, NSA_KV_GROUPS, HEAD_DIM, n_cmp, NSA_BLOCK)
        xb = kcv.transpose(1, 0, 2, 4, 5, 3).reshape(2, nb * NSA_KV_GROUPS * n_cmp, NSA_BLOCK * HEAD_DIM)
        cmp = _compress(xb, nsa_cmp_pos[l].reshape(2, 1, NSA_BLOCK * HEAD_DIM),
                        nsa_cmp_w[l].reshape(2, NSA_BLOCK * HEAD_DIM, HEAD_DIM), 128)
        cmp = cmp.reshape(2, nb, NSA_KV_GROUPS, n_cmp, HEAD_DIM)
        o_cmp, sel = _nsa_cmp_t(pt, cmp[0], cmp[1].transpose(0, 1, 3, 2), 256)
        o_sel = _nsa_band_t(pt, sel, 256, NSA_SEL_TK, True)
        o_win = _nsa_band_t(pt, None, 256, 256, False)

        xs_pad = _pad_rows(hs, n_dec_pad)[None]
        pt_s, _ = _inproj(xs_pad, wt, wg_tok, bgate_col, bforget_col, bgate_row, tabs_s, n_dec_pad)
        ps = pt_s[0, :, :n_dec].T
        sp = SampleProj(
            fq=ps[:, R_FQ:R_FQ + 256].reshape(n_dec, 1, FOX_HEADS, HEAD_DIM),
            fox_rows=ps[:, R_FOX:R_FOX + 516].reshape(n_dec, 1, 129, FOX_HEADS).transpose(0, 1, 3, 2),
            nq=ps[:, R_NQ:R_NQ + 512].reshape(n_dec, 1, NSA_HEADS, HEAD_DIM),
            ng=ps[:, R_GATE:R_GATE + 24].reshape(n_dec, 1, 3, NSA_HEADS),
            nsa_rows=ps[:, R_NSA:R_NSA + 512].reshape(n_dec, 1, 4, NSA_KV_GROUPS, HEAD_DIM),
            win_rows=ps[:, R_WIN:R_WIN + 256].reshape(n_dec, 1, 2, NSA_KV_GROUPS, HEAD_DIM),
            dq=ps[:, R_DQ:R_DQ + 256].reshape(n_dec, 1, DIFF_HEADS, 2, DIFF_QK_DIM),
            diff_rows=ps[:, R_DIFF:R_DIFF + 512].reshape(n_dec, 1, DIFF_HEADS, 2, HEAD_DIM))
        s_fox = _fox_sample(fox_fm, page_table, l, sp, SAMPLE_PAGES_PER_STEP)
        s_nsa, win_new = _nsa_sample(nsa_fm, win_fm, page_table, l, sp, nsa_cmp_pos[l], nsa_cmp_w[l],
                                     SAMPLE_PAGES_PER_STEP)
        s_diff = _diff_sample(diff_fm, page_table, l, sp, lam, lam_init, diff_subln_g[l], SAMPLE_PAGES_PER_STEP)
        outs['fox_s'].append(sp.fox_rows)
        outs['nsa_s'].append(sp.nsa_rows)
        outs['diff_s'].append(sp.diff_rows)
        outs['win_s'].append(win_new)

        def rows(a_p, a_s):
            return _pad_rows(jnp.concatenate([a_p.reshape(m_p, -1), a_s], 0), m_pad)

        zero_s = jnp.zeros((n_dec, 512), F32)
        h_all = rows(hp, hs)
        h1, logits = _outproj(rows(o_fox, s_fox), rows(o_cmp, s_nsa), rows(o_sel, zero_s), rows(o_win, zero_s),
                              rows(o_diff, s_diff), h_all, w_out_b, g1, b1, wr, br, 512)
        h2 = _moe_block(h1, logits[:, :N_EXPERTS], moe_w, g2, b2, 512, 512)
        hp = h2[:m_p].reshape(nb, seq, D_MODEL)
        hs = h2[m_p:m_all]

    st = lambda k: jnp.stack(outs[k], axis=1)
    return (hp, hs.reshape(n_dec, n_new, D_MODEL),
            st('fox_p'), st('fox_s'), st('nsa_p'), st('nsa_s'),
            st('diff_p'), st('diff_s'), st('win_p'), st('win_s'))
```

```python
import collections
import functools
import math

import numpy as np
import jax
import jax.numpy as jnp
from jax import lax
from jax.experimental import pallas as pl
from jax.experimental.pallas import tpu as pltpu

F32 = jnp.float32
BF16 = jnp.bfloat16

D_MODEL = 1024
DEPTH = 2
PAGE_SIZE = 128
HEAD_DIM = 64
FOX_HEADS = 4
NSA_HEADS = 8
NSA_KV_GROUPS = 2
NSA_HPG = NSA_HEADS // NSA_KV_GROUPS
DIFF_HEADS = 4
DIFF_QK_DIM = HEAD_DIM // 2
NSA_BLOCK = 64
BLOCK_SHIFT = 6
NSA_TOP_N = 16
NSA_WINDOW = 512
ROPE_THETA = 500000.0
N_EXPERTS = 32
TOP_K = 4
SWIGLU_LIMIT = 7.0
SWIGLU_ALPHA = 1.702
LN_EPS = 1e-5
RMS_EPS = 1e-5
DEEPNORM_ALPHA = (2 * DEPTH) ** 0.25
NEG = -1e30
FORCE_SCORE = 1e4

SUBLANES = 8
LANES = 128
VMEM_LIMIT_BYTES = 48 * 1024 * 1024
SAMPLE_PAGES_PER_STEP = 16
NSA_SEL_TK = 1024

C_FQ, C_FK, C_FV, C_FF = 0, 256, 512, 768
C_NQ, C_NKC, C_NKW, C_NVW, C_NG = 772, 1284, 1796, 1924, 2052
C_DQ, C_DK, C_DV = 2076, 2332, 2588
N_IN = 2844

R_NQ = 0
R_NSA = 512
R_DIFF = 1024
R_FQ = 1536
R_FK = 1792
R_FV = 2048
R_WIN = 2304
R_DQ = 2560
R_GATE = 2816
R_FOX = 2848
R_LOGF = R_FOX + 512
NP_ROWS = 3368


def _build_perm():
    perm = np.full((NP_ROWS,), -1, np.int64)
    perm[R_NQ:R_NQ + 512] = C_NQ + np.arange(512)
    perm[R_NSA:R_NSA + 512] = C_NKC + np.arange(512)
    for h in range(DIFF_HEADS):
        perm[R_DIFF + h * 128:R_DIFF + h * 128 + 64] = C_DK + h * 64 + np.arange(64)
        perm[R_DIFF + h * 128 + 64:R_DIFF + h * 128 + 128] = C_DV + h * 64 + np.arange(64)
    perm[R_FQ:R_FQ + 256] = C_FQ + np.arange(256)
    perm[R_FK:R_FK + 256] = C_FK + np.arange(256)
    perm[R_FV:R_FV + 256] = C_FV + np.arange(256)
    perm[R_WIN:R_WIN + 256] = C_NKW + np.arange(256)
    perm[R_DQ:R_DQ + 256] = C_DQ + np.arange(256)
    perm[R_GATE:R_GATE + 24] = C_NG + np.arange(24)
    for c in range(64):
        for h in range(FOX_HEADS):
            perm[R_FOX + c * 4 + h] = C_FK + h * 64 + c
            perm[R_FOX + (64 + c) * 4 + h] = C_FV + h * 64 + c
    for h in range(FOX_HEADS):
        perm[R_FOX + 512 + h] = C_FF + h
    return perm


_PERM = _build_perm()
_ROT64_ROWS = tuple([R_NQ + h * 64 for h in range(NSA_HEADS)]
                    + [R_NSA + g * 64 for g in range(NSA_KV_GROUPS)]
                    + [R_NSA + 256 + g * 64 for g in range(NSA_KV_GROUPS)]
                    + [R_WIN + g * 64 for g in range(NSA_KV_GROUPS)])
_ROT32_ROWS = tuple([R_DQ + i * 32 for i in range(2 * DIFF_HEADS)]
                    + [R_DIFF + h * 128 + m * 32 for h in range(DIFF_HEADS) for m in range(2)])


def _cparams(sem):
    return pltpu.CompilerParams(dimension_semantics=sem, vmem_limit_bytes=VMEM_LIMIT_BYTES)


def _inproj_kernel(x_ref, w_ref, wg_ref, bgate_ref, bforget_ref, bgrow_ref, cn_ref, sn_ref, cd_ref, sd_ref,
                   o_ref, g_ref):
    x = x_ref[...].astype(BF16)
    o_ref[...] = lax.dot_general(w_ref[...], x, (((1,), (1,)), ((), ())), preferred_element_type=F32)
    cn = cn_ref[...]
    sn = sn_ref[...]
    for r in _ROT64_ROWS:
        x1 = o_ref[r:r + 8, :]
        x2 = o_ref[r + 8:r + 16, :]
        o_ref[r:r + 8, :] = x1 * cn - x2 * sn
        o_ref[r + 8:r + 16, :] = x2 * cn + x1 * sn
    cd = cd_ref[...]
    sd = sd_ref[...]
    for r in _ROT32_ROWS:
        v = o_ref[r:r + 8, :]
        o_ref[r:r + 8, :] = v * cd + pltpu.roll(v, 4, 0) * sd
    z = o_ref[R_GATE:R_GATE + 32, :] + bgate_ref[...]
    o_ref[R_GATE:R_GATE + 32, :] = 1.0 / (1.0 + jnp.exp(-z))
    z = o_ref[R_LOGF:R_LOGF + 8, :] + bforget_ref[...]
    o_ref[R_LOGF:R_LOGF + 8, :] = jnp.minimum(z, 0.0) - jnp.log1p(jnp.exp(-jnp.abs(z)))
    zg = jnp.dot(x, wg_ref[...], preferred_element_type=F32) + bgrow_ref[...]
    g_ref[...] = 1.0 / (1.0 + jnp.exp(-zg))


def _inproj(x, wt, wg, bgate_col, bforget_col, bgate_row, tabs, tn):
    nb, nt, k = x.shape
    cn, sn, cd, sd = tabs
    const = lambda b, i: (0, 0)
    tab = lambda b, i: (0, i)
    return pl.pallas_call(
        _inproj_kernel,
        grid=(nb, nt // tn),
        in_specs=[
            pl.BlockSpec((None, tn, k), lambda b, i: (b, i, 0)),
            pl.BlockSpec((NP_ROWS, k), const),
            pl.BlockSpec((k, LANES), const),
            pl.BlockSpec((32, 1), const),
            pl.BlockSpec((8, 1), const),
            pl.BlockSpec((1, LANES), const),
            pl.BlockSpec((8, tn), tab),
            pl.BlockSpec((8, tn), tab),
            pl.BlockSpec((8, tn), tab),
            pl.BlockSpec((8, tn), tab),
        ],
        out_specs=[
            pl.BlockSpec((None, NP_ROWS, tn), lambda b, i: (b, 0, i)),
            pl.BlockSpec((None, tn, LANES), lambda b, i: (b, i, 0)),
        ],
        out_shape=[jax.ShapeDtypeStruct((nb, NP_ROWS, nt), F32),
                   jax.ShapeDtypeStruct((nb, nt, LANES), F32)],
        compiler_params=_cparams(("parallel", "parallel")),
        name="inproj",
    )(x, wt, wg, bgate_col, bforget_col, bgate_row, cn, sn, cd, sd)


def _rope_tables(pos):
    inv8 = ROPE_THETA ** (-jnp.arange(8, dtype=F32) / 8)
    ang8 = pos[:, None] * inv8[None, :]
    cn, sn = jnp.cos(ang8).T, jnp.sin(ang8).T
    inv4 = ROPE_THETA ** (-jnp.arange(4, dtype=F32) / 4)
    ang4 = pos[:, None] * inv4[None, :]
    c4, s4 = jnp.cos(ang4).T, jnp.sin(ang4).T
    cd = jnp.concatenate([c4, c4], 0)
    sd = jnp.concatenate([-s4, s4], 0)
    return cn, sn, cd, sd


def _tri_pairs(nq, lo_tiles=None):
    qi, kj = [], []
    for i in range(nq):
        j0 = 0 if lo_tiles is None else max(0, i - lo_tiles)
        for j in range(j0, i + 1):
            qi.append(i)
            kj.append(j)
    first = [1 if (p == 0 or qi[p] != qi[p - 1]) else 0 for p in range(len(qi))]
    return (jnp.asarray(qi, jnp.int32), jnp.asarray(kj, jnp.int32), jnp.asarray(first, jnp.int32))


def _online_update(s, v_t, m_ref, l_ref, acc_ref):
    m_prev = m_ref[:, 0:1]
    m_new = jnp.maximum(m_prev, jnp.max(s, axis=1, keepdims=True))
    alpha = jnp.exp(m_prev - m_new)
    p = jnp.exp(s - m_new)
    l_ref[...] = jnp.broadcast_to(alpha * l_ref[:, 0:1] + jnp.sum(p, axis=1, keepdims=True), l_ref.shape)
    m_ref[...] = jnp.broadcast_to(m_new, m_ref.shape)
    pv = lax.dot_general(p.astype(BF16), v_t, (((1,), (1,)), ((), ())), preferred_element_type=F32)
    acc_ref[...] = alpha * acc_ref[...] + pv


def _fox_kernel(qi_ref, kj_ref, first_ref, q_ref, k_ref, v_ref, c_ref, o_ref, q_s, m_s, l_s, acc_s, *, tq, tk):
    p_id = pl.program_id(1)
    qi = qi_ref[p_id]
    kj = kj_ref[p_id]
    scale = HEAD_DIM ** -0.5

    @pl.when(first_ref[p_id] == 1)
    def _():
        for h in range(FOX_HEADS):
            q_s[h] = q_ref[h * 64:(h + 1) * 64, :].T.astype(BF16)
        m_s[...] = jnp.full(m_s.shape, NEG, F32)
        l_s[...] = jnp.zeros(l_s.shape, F32)
        acc_s[...] = jnp.zeros(acc_s.shape, F32)

    row = qi * tq + lax.broadcasted_iota(jnp.int32, (tq, tk), 0)
    col = kj * tk + lax.broadcasted_iota(jnp.int32, (tq, tk), 1)
    mask = col <= row
    for h in range(FOX_HEADS):
        k_t = k_ref[h * 64:(h + 1) * 64, :].astype(BF16)
        v_t = v_ref[h * 64:(h + 1) * 64, :].astype(BF16)
        s = jnp.dot(q_s[h], k_t, preferred_element_type=F32) * scale - c_ref[h:h + 1, :]
        s = jnp.where(mask, s, NEG)
        _online_update(s, v_t, m_s.at[h], l_s.at[h], acc_s.at[h])

    @pl.when(kj == qi)
    def _():
        for h in range(FOX_HEADS):
            o_ref[:, h * 64:(h + 1) * 64] = acc_s[h] / l_s[h][:, 0:1]


def _fox_attention(pt, c, tq):
    nb, _, nt = pt.shape
    tk = tq
    qi, kj, first = _tri_pairs(nt // tq)
    blk = 256
    grid_spec = pltpu.PrefetchScalarGridSpec(
        num_scalar_prefetch=3,
        grid=(nb, qi.shape[0]),
        in_specs=[
            pl.BlockSpec((None, blk, tq), lambda b, p, qi, kj, f: (b, R_FQ // blk, qi[p])),
            pl.BlockSpec((None, blk, tk), lambda b, p, qi, kj, f: (b, R_FK // blk, kj[p])),
            pl.BlockSpec((None, blk, tk), lambda b, p, qi, kj, f: (b, R_FV // blk, kj[p])),
            pl.BlockSpec((None, FOX_HEADS, tk), lambda b, p, qi, kj, f: (b, 0, kj[p])),
        ],
        out_specs=pl.BlockSpec((None, tq, blk), lambda b, p, qi, kj, f: (b, qi[p], 0)),
        scratch_shapes=[
            pltpu.VMEM((FOX_HEADS, tq, HEAD_DIM), BF16),
            pltpu.VMEM((FOX_HEADS, tq, LANES), F32),
            pltpu.VMEM((FOX_HEADS, tq, LANES), F32),
            pltpu.VMEM((FOX_HEADS, tq, HEAD_DIM), F32),
        ],
    )
    return pl.pallas_call(
        functools.partial(_fox_kernel, tq=tq, tk=tk),
        grid_spec=grid_spec,
        out_shape=jax.ShapeDtypeStruct((nb, nt, blk), F32),
        compiler_params=_cparams(("parallel", "arbitrary")),
        name="fox_attn",
    )(qi, kj, first, pt, pt, pt, c)


def _diff_kernel(qi_ref, kj_ref, first_ref, lam_ref, q_ref, kv_ref, g_ref, o_ref, q_s, m_s, l_s, acc_s,
                 *, tq, tk, out_scale):
    p_id = pl.program_id(1)
    qi = qi_ref[p_id]
    kj = kj_ref[p_id]
    scale = DIFF_QK_DIM ** -0.5

    @pl.when(first_ref[p_id] == 1)
    def _():
        for i in range(2 * DIFF_HEADS):
            q_s[i] = q_ref[i * 32:(i + 1) * 32, :].T.astype(BF16)
        m_s[...] = jnp.full(m_s.shape, NEG, F32)
        l_s[...] = jnp.zeros(l_s.shape, F32)
        acc_s[...] = jnp.zeros(acc_s.shape, F32)

    row = qi * tq + lax.broadcasted_iota(jnp.int32, (tq, tk), 0)
    col = kj * tk + lax.broadcasted_iota(jnp.int32, (tq, tk), 1)
    mask = col <= row
    for h in range(DIFF_HEADS):
        v_t = kv_ref[h * 128 + 64:h * 128 + 128, :].astype(BF16)
        for m in range(2):
            i = 2 * h + m
            k_t = kv_ref[h * 128 + m * 32:h * 128 + (m + 1) * 32, :].astype(BF16)
            s = jnp.dot(q_s[i], k_t, preferred_element_type=F32) * scale
            s = jnp.where(mask, s, NEG)
            _online_update(s, v_t, m_s.at[i], l_s.at[i], acc_s.at[i])

    @pl.when(kj == qi)
    def _():
        lam = lam_ref[0]
        for h in range(DIFF_HEADS):
            o1 = acc_s[2 * h] / l_s[2 * h][:, 0:1]
            o2 = acc_s[2 * h + 1] / l_s[2 * h + 1][:, 0:1]
            o = o1 - lam * o2
            o = o * lax.rsqrt(jnp.mean(o * o, axis=-1, keepdims=True) + RMS_EPS) * g_ref[...] * out_scale
            o_ref[:, h * 64:(h + 1) * 64] = o


def _diff_attention(pt, lam, g_row, out_scale, tq):
    nb, _, nt = pt.shape
    tk = tq
    qi, kj, first = _tri_pairs(nt // tq)
    grid_spec = pltpu.PrefetchScalarGridSpec(
        num_scalar_prefetch=4,
        grid=(nb, qi.shape[0]),
        in_specs=[
            pl.BlockSpec((None, 256, tq), lambda b, p, qi, kj, f, lam: (b, R_DQ // 256, qi[p])),
            pl.BlockSpec((None, 512, tk), lambda b, p, qi, kj, f, lam: (b, R_DIFF // 512, kj[p])),
            pl.BlockSpec((1, HEAD_DIM), lambda b, p, qi, kj, f, lam: (0, 0)),
        ],
        out_specs=pl.BlockSpec((None, tq, 256), lambda b, p, qi, kj, f, lam: (b, qi[p], 0)),
        scratch_shapes=[
            pltpu.VMEM((2 * DIFF_HEADS, tq, DIFF_QK_DIM), BF16),
            pltpu.VMEM((2 * DIFF_HEADS, tq, LANES), F32),
            pltpu.VMEM((2 * DIFF_HEADS, tq, LANES), F32),
            pltpu.VMEM((2 * DIFF_HEADS, tq, HEAD_DIM), F32),
        ],
    )
    return pl.pallas_call(
        functools.partial(_diff_kernel, tq=tq, tk=tk, out_scale=out_scale),
        grid_spec=grid_spec,
        out_shape=jax.ShapeDtypeStruct((nb, nt, 256), F32),
        compiler_params=_cparams(("parallel", "arbitrary")),
        name="diff_attn",
    )(qi, kj, first, lam, pt, pt, g_row)


def _compress_kernel(x_ref, pe_ref, w_ref, o_ref):
    x = (x_ref[...] + pe_ref[...]).astype(BF16)
    o_ref[...] = jnp.dot(x, w_ref[...].astype(BF16), preferred_element_type=F32)


def _compress(xb, pe, w, tm):
    _, m, kk = xb.shape
    return pl.pallas_call(
        _compress_kernel,
        grid=(2, m // tm),
        in_specs=[
            pl.BlockSpec((None, tm, kk), lambda t, i: (t, i, 0)),
            pl.BlockSpec((None, 1, kk), lambda t, i: (t, 0, 0)),
            pl.BlockSpec((None, kk, HEAD_DIM), lambda t, i: (t, 0, 0)),
        ],
        out_specs=pl.BlockSpec((None, tm, HEAD_DIM), lambda t, i: (t, i, 0)),
        out_shape=jax.ShapeDtypeStruct((2, m, HEAD_DIM), F32),
        compiler_params=_cparams(("parallel", "parallel")),
        name="nsa_compress",
    )(xb, pe, w)


def _select_mask(imp, qpos, n_blk):
    jb = lax.broadcasted_iota(jnp.int32, imp.shape, 1)
    cur = qpos >> BLOCK_SHIFT
    forced = (jb == 0) | (jb == cur) | (jb == cur - 1)
    score = jnp.where(jb <= cur, jnp.where(forced, FORCE_SCORE, imp), NEG)
    rank = jnp.zeros(imp.shape, F32)
    for j in range(n_blk):
        col = score[:, j:j + 1]
        ahead = (col > score) | ((col == score) & (jb > j))
        rank = rank + jnp.where(ahead, 1.0, 0.0)
    return jnp.where(rank < float(NSA_TOP_N), 1.0, 0.0)


def _nsa_cmp_kernel(q_ref, kc_ref, vc_ref, gate_ref, o_ref, sel_ref, *, tq, n_cmp):
    qi = pl.program_id(2)
    g = pl.program_id(1)
    scale = HEAD_DIM ** -0.5
    kc = kc_ref[...].astype(BF16)
    vc = vc_ref[...].astype(BF16)
    qpos = qi * tq + lax.broadcasted_iota(jnp.int32, (tq, 1), 0)
    nb_iota = lax.broadcasted_iota(jnp.int32, (tq, n_cmp), 1)
    mask = nb_iota < ((qpos + 1) >> BLOCK_SHIFT)
    maskf = jnp.where(mask, 1.0, 0.0)
    imp = jnp.zeros((tq, n_cmp), F32)
    gates = gate_ref[...]
    for h in range(NSA_HPG):
        qh = q_ref[h * 64:(h + 1) * 64, :].T.astype(BF16)
        lc = lax.dot_general(qh, kc, (((1,), (1,)), ((), ())), preferred_element_type=F32) * scale
        lc = jnp.where(mask, lc, NEG)
        m = jnp.max(lc, axis=1, keepdims=True)
        p = jnp.exp(lc - m) * maskf
        pc = p / jnp.maximum(jnp.sum(p, axis=1, keepdims=True), 1e-30)
        imp = imp + pc
        oc = jnp.dot(pc.astype(BF16), vc, preferred_element_type=F32)
        gcol = jnp.where(g == 0, gates[:, h:h + 1], gates[:, NSA_HPG + h:NSA_HPG + h + 1])
        o_ref[:, h * 64:(h + 1) * 64] = oc * gcol
    sel_ref[...] = _select_mask(imp, qpos, n_cmp)


def _nsa_cmp(pt, gates, kc, vc, tq):
    nb, _, nt = pt.shape
    n_cmp = kc.shape[2]
    return pl.pallas_call(
        functools.partial(_nsa_cmp_kernel, tq=tq, n_cmp=n_cmp),
        grid=(nb, NSA_KV_GROUPS, nt // tq),
        in_specs=[
            pl.BlockSpec((None, 256, tq), lambda b, g, i: (b, R_NQ // 256 + g, i)),
            pl.BlockSpec((None, None, n_cmp, HEAD_DIM), lambda b, g, i: (b, g, 0, 0)),
            pl.BlockSpec((None, None, n_cmp, HEAD_DIM), lambda b, g, i: (b, g, 0, 0)),
            pl.BlockSpec((None, tq, LANES), lambda b, g, i: (b, i, 0)),
        ],
        out_specs=[
            pl.BlockSpec((None, tq, 256), lambda b, g, i: (b, i, g)),
            pl.BlockSpec((None, None, tq, n_cmp), lambda b, g, i: (b, g, i, 0)),
        ],
        out_shape=[jax.ShapeDtypeStruct((nb, nt, 512), F32),
                   jax.ShapeDtypeStruct((nb, NSA_KV_GROUPS, nt, n_cmp), F32)],
        compiler_params=_cparams(("parallel", "parallel", "parallel")),
        name="nsa_cmp",
    )(pt, kc, vc, gates)


def _nsa_band_kernel(qi_ref, kj_ref, first_ref, last_ref, q_ref, k_ref, v_ref, gate_ref, *rest,
                     tq, tk, selected, gate_base):
    if selected:
        sel_ref, o_ref, q_s, m_s, l_s, acc_s = rest
    else:
        o_ref, q_s, m_s, l_s, acc_s = rest
    p_id = pl.program_id(2)
    g = pl.program_id(1)
    qi = qi_ref[p_id]
    kj = kj_ref[p_id]
    scale = HEAD_DIM ** -0.5

    @pl.when(first_ref[p_id] == 1)
    def _():
        for h in range(NSA_HPG):
            q_s[h * tq:(h + 1) * tq, :] = q_ref[h * 64:(h + 1) * 64, :].T.astype(BF16)
        m_s[...] = jnp.full(m_s.shape, NEG, F32)
        l_s[...] = jnp.zeros(l_s.shape, F32)
        acc_s[...] = jnp.zeros(acc_s.shape, F32)

    row = qi * tq + lax.broadcasted_iota(jnp.int32, (tq, tk), 0)
    col = kj * tk + lax.broadcasted_iota(jnp.int32, (tq, tk), 1)
    mask = col <= row
    if selected:
        n_blk = sel_ref.shape[-1]
        blk_of_col = kj * (tk // NSA_BLOCK) + (lax.broadcasted_iota(jnp.int32, (n_blk, tk), 1) >> BLOCK_SHIFT)
        expand = jnp.where(lax.broadcasted_iota(jnp.int32, (n_blk, tk), 0) == blk_of_col, 1.0, 0.0).astype(BF16)
        picked = jnp.dot(sel_ref[...].astype(BF16), expand, preferred_element_type=F32)
        mask = mask & (picked > 0.5)
    else:
        mask = mask & (col > row - NSA_WINDOW)
    keep = jnp.where(mask, 1.0, 0.0)
    k_t = k_ref[...].astype(BF16)
    v_t = v_ref[...].astype(BF16)
    s = jnp.dot(q_s[...], k_t, preferred_element_type=F32) * scale
    s = jnp.where(jnp.concatenate([keep] * NSA_HPG, axis=0) > 0.5, s, NEG)
    _online_update(s, v_t, m_s, l_s, acc_s)

    @pl.when(last_ref[p_id] == 1)
    def _():
        gates = gate_ref[...]
        for h in range(NSA_HPG):
            c0 = gate_base + h
            gcol = jnp.where(g == 0, gates[:, c0:c0 + 1], gates[:, c0 + NSA_HPG:c0 + NSA_HPG + 1])
            o_ref[:, h * 64:(h + 1) * 64] = acc_s[h * tq:(h + 1) * tq, :] / l_s[h * tq:(h + 1) * tq, 0:1] * gcol


def _nsa_band(pt, gates, sel, tq, selected):
    nb, _, nt = pt.shape
    tk = tq
    nq = nt // tq
    qi, kj, first = _tri_pairs(nq, None if selected else -(-NSA_WINDOW // tk))
    last = jnp.concatenate([first[1:], jnp.ones((1,), jnp.int32)])
    if selected:
        rk, rv, gate_base = R_NSA + 256, R_NSA + 384, 8
    else:
        rk, rv, gate_base = R_WIN, R_WIN + 128, 16
    idx = lambda f: (lambda b, g, p, qi, kj, fi, la: f(b, g, p, qi, kj))
    in_specs = [
        pl.BlockSpec((None, 256, tq), idx(lambda b, g, p, qi, kj: (b, R_NQ // 256 + g, qi[p]))),
        pl.BlockSpec((None, HEAD_DIM, tk), idx(lambda b, g, p, qi, kj: (b, rk // 64 + g, kj[p]))),
        pl.BlockSpec((None, HEAD_DIM, tk), idx(lambda b, g, p, qi, kj: (b, rv // 64 + g, kj[p]))),
        pl.BlockSpec((None, tq, LANES), idx(lambda b, g, p, qi, kj: (b, qi[p], 0))),
    ]
    args = [pt, pt, pt, gates]
    if selected:
        in_specs.append(pl.BlockSpec((None, None, tq, sel.shape[-1]),
                                     idx(lambda b, g, p, qi, kj: (b, g, qi[p], 0))))
        args.append(sel)
    grid_spec = pltpu.PrefetchScalarGridSpec(
        num_scalar_prefetch=4,
        grid=(nb, NSA_KV_GROUPS, qi.shape[0]),
        in_specs=in_specs,
        out_specs=pl.BlockSpec((None, tq, 256), idx(lambda b, g, p, qi, kj: (b, qi[p], g))),
        scratch_shapes=[
            pltpu.VMEM((NSA_HPG * tq, HEAD_DIM), BF16),
            pltpu.VMEM((NSA_HPG * tq, LANES), F32),
            pltpu.VMEM((NSA_HPG * tq, LANES), F32),
            pltpu.VMEM((NSA_HPG * tq, HEAD_DIM), F32),
        ],
    )
    return pl.pallas_call(
        functools.partial(_nsa_band_kernel, tq=tq, tk=tk, selected=selected, gate_base=gate_base),
        grid_spec=grid_spec,
        out_shape=jax.ShapeDtypeStruct((nb, nt, 512), F32),
        compiler_params=_cparams(("parallel", "parallel", "arbitrary")),
        name="nsa_sel" if selected else "nsa_win",
    )(qi, kj, first, last, *args)


def _flash_step(s, v_t, m_ref, l_ref, acc_ref):
    m_prev = m_ref[...]
    m_new = jnp.maximum(m_prev, jnp.max(s, axis=0, keepdims=True))
    alpha = jnp.exp(m_prev - m_new)
    p = jnp.exp(s - m_new)
    l_ref[...] = alpha * l_ref[...] + jnp.sum(p, axis=0, keepdims=True)
    m_ref[...] = m_new
    acc_ref[...] = alpha * acc_ref[...] + jnp.dot(v_t, p.astype(BF16), preferred_element_type=F32)


def _kq(k_t, q_t):
    return lax.dot_general(k_t, q_t, (((0,), (0,)), ((), ())), preferred_element_type=F32)


def _causal_keep(qi, kj, tq, tk):
    key = kj * tk + lax.broadcasted_iota(jnp.int32, (tk, tq), 0)
    qry = qi * tq + lax.broadcasted_iota(jnp.int32, (tk, tq), 1)
    return key, qry


def _split3(x):
    x1 = x.astype(BF16)
    r1 = x - x1.astype(F32)
    x2 = r1.astype(BF16)
    x3 = (r1 - x2.astype(F32)).astype(BF16)
    return x1, x2, x3


def _fox_kernel_t(qi_ref, kj_ref, first_ref, q_ref, k_ref, v_ref, c_ref, o_ref, q_s, m_s, l_s, acc_s, *, tq, tk):
    p_id = pl.program_id(1)
    qi = qi_ref[p_id]
    kj = kj_ref[p_id]

    @pl.when(first_ref[p_id] == 1)
    def _():
        extra = jnp.where(lax.broadcasted_iota(jnp.int32, (16, tq), 0) < 3, -1.0, 0.0).astype(BF16)
        for h in range(FOX_HEADS):
            q_s[h, 0:64, :] = (q_ref[h * 64:(h + 1) * 64, :] * (HEAD_DIM ** -0.5)).astype(BF16)
            q_s[h, 64:80, :] = extra
        m_s[...] = jnp.full(m_s.shape, NEG, F32)
        l_s[...] = jnp.zeros(l_s.shape, F32)
        acc_s[...] = jnp.zeros(acc_s.shape, F32)

    def tile(masked):
        sub16 = lax.broadcasted_iota(jnp.int32, (16, tk), 0)
        if masked:
            key, qry = _causal_keep(qi, kj, tq, tk)
            mask = key <= qry
        for h in range(FOX_HEADS):
            c1, c2, c3 = [jnp.broadcast_to(t.astype(F32), (16, tk)) for t in _split3(c_ref[h:h + 1, :])]
            extra = jnp.where(sub16 == 0, c1, jnp.where(sub16 == 1, c2, jnp.where(sub16 == 2, c3, 0.0)))
            k_aug = jnp.concatenate([k_ref[h * 64:(h + 1) * 64, :].astype(BF16), extra.astype(BF16)], axis=0)
            s = _kq(k_aug, q_s[h])
            if masked:
                s = jnp.where(mask, s, NEG)
            _flash_step(s, v_ref[h * 64:(h + 1) * 64, :].astype(BF16), m_s.at[h], l_s.at[h], acc_s.at[h])

    pl.when(kj == qi)(lambda: tile(True))
    pl.when(kj != qi)(lambda: tile(False))

    @pl.when(kj == qi)
    def _():
        for h in range(FOX_HEADS):
            o_ref[:, h * 64:(h + 1) * 64] = (acc_s[h] / l_s[h]).T


def _fox_attention_t(pt, c, tq):
    nb, _, nt = pt.shape
    tk = tq
    qi, kj, first = _tri_pairs(nt // tq)
    blk = 256
    grid_spec = pltpu.PrefetchScalarGridSpec(
        num_scalar_prefetch=3,
        grid=(nb, qi.shape[0]),
        in_specs=[
            pl.BlockSpec((None, blk, tq), lambda b, p, qi, kj, f: (b, R_FQ // blk, qi[p])),
            pl.BlockSpec((None, blk, tk), lambda b, p, qi, kj, f: (b, R_FK // blk, kj[p])),
            pl.BlockSpec((None, blk, tk), lambda b, p, qi, kj, f: (b, R_FV // blk, kj[p])),
            pl.BlockSpec((None, FOX_HEADS, tk), lambda b, p, qi, kj, f: (b, 0, kj[p])),
        ],
        out_specs=pl.BlockSpec((None, tq, blk), lambda b, p, qi, kj, f: (b, qi[p], 0)),
        scratch_shapes=[
            pltpu.VMEM((FOX_HEADS, HEAD_DIM + 16, tq), BF16),
            pltpu.VMEM((FOX_HEADS, 1, tq), F32),
            pltpu.VMEM((FOX_HEADS, 1, tq), F32),
            pltpu.VMEM((FOX_HEADS, HEAD_DIM, tq), F32),
        ],
    )
    return pl.pallas_call(
        functools.partial(_fox_kernel_t, tq=tq, tk=tk),
        grid_spec=grid_spec,
        out_shape=jax.ShapeDtypeStruct((nb, nt, blk), F32),
        compiler_params=_cparams(("parallel", "arbitrary")),
        name="fox_attn",
    )(qi, kj, first, pt, pt, pt, c)


def _diff_kernel_t(qi_ref, kj_ref, first_ref, lam_ref, q_ref, kv_ref, g_ref, o_ref, q_s, m_s, l_s, acc_s,
                   *, tq, tk, out_scale):
    p_id = pl.program_id(1)
    qi = qi_ref[p_id]
    kj = kj_ref[p_id]
    scale = DIFF_QK_DIM ** -0.5

    @pl.when(first_ref[p_id] == 1)
    def _():
        q_s[...] = q_ref[...].astype(BF16)
        m_s[...] = jnp.full(m_s.shape, NEG, F32)
        l_s[...] = jnp.zeros(l_s.shape, F32)
        acc_s[...] = jnp.zeros(acc_s.shape, F32)

    def tile(masked):
        if masked:
            key, qry = _causal_keep(qi, kj, tq, tk)
            mask = key <= qry
        for h in range(DIFF_HEADS):
            v_t = kv_ref[h * 128 + 64:h * 128 + 128, :].astype(BF16)
            for m in range(2):
                i = 2 * h + m
                k_t = kv_ref[h * 128 + m * 32:h * 128 + (m + 1) * 32, :].astype(BF16)
                s = _kq(k_t, q_s[i * 32:(i + 1) * 32, :]) * scale
                if masked:
                    s = jnp.where(mask, s, NEG)
                _flash_step(s, v_t, m_s.at[i], l_s.at[i], acc_s.at[i])

    pl.when(kj == qi)(lambda: tile(True))
    pl.when(kj != qi)(lambda: tile(False))

    @pl.when(kj == qi)
    def _():
        lam = lam_ref[0]
        for h in range(DIFF_HEADS):
            o = acc_s[2 * h] / l_s[2 * h] - lam * (acc_s[2 * h + 1] / l_s[2 * h + 1])
            o = o * lax.rsqrt(jnp.mean(o * o, axis=0, keepdims=True) + RMS_EPS) * g_ref[...] * out_scale
            o_ref[:, h * 64:(h + 1) * 64] = o.T


def _diff_attention_t(pt, lam, g_col, out_scale, tq):
    nb, _, nt = pt.shape
    tk = tq
    qi, kj, first = _tri_pairs(nt // tq)
    grid_spec = pltpu.PrefetchScalarGridSpec(
        num_scalar_prefetch=4,
        grid=(nb, qi.shape[0]),
        in_specs=[
            pl.BlockSpec((None, 256, tq), lambda b, p, qi, kj, f, lam: (b, R_DQ // 256, qi[p])),
            pl.BlockSpec((None, 512, tk), lambda b, p, qi, kj, f, lam: (b, R_DIFF // 512, kj[p])),
            pl.BlockSpec((HEAD_DIM, 1), lambda b, p, qi, kj, f, lam: (0, 0)),
        ],
        out_specs=pl.BlockSpec((None, tq, 256), lambda b, p, qi, kj, f, lam: (b, qi[p], 0)),
        scratch_shapes=[
            pltpu.VMEM((256, tq), BF16),
            pltpu.VMEM((2 * DIFF_HEADS, 1, tq), F32),
            pltpu.VMEM((2 * DIFF_HEADS, 1, tq), F32),
            pltpu.VMEM((2 * DIFF_HEADS, HEAD_DIM, tq), F32),
        ],
    )
    return pl.pallas_call(
        functools.partial(_diff_kernel_t, tq=tq, tk=tk, out_scale=out_scale),
        grid_spec=grid_spec,
        out_shape=jax.ShapeDtypeStruct((nb, nt, 256), F32),
        compiler_params=_cparams(("parallel", "arbitrary")),
        name="diff_attn",
    )(qi, kj, first, lam, pt, pt, g_col)


def _stack_heads(q_ref, tq):
    return jnp.concatenate([q_ref[h * 64:(h + 1) * 64, :] for h in range(NSA_HPG)], axis=1)


def _select_mask_t(imp, qpos, n_blk):
    jb = lax.broadcasted_iota(jnp.int32, imp.shape, 0)
    cur = qpos >> BLOCK_SHIFT
    forced = (jb == 0) | (jb == cur) | (jb == cur - 1)
    score = jnp.where(jb <= cur, jnp.where(forced, FORCE_SCORE, imp), NEG)
    rank = jnp.zeros(imp.shape, F32)
    for j in range(n_blk):
        row = score[j:j + 1, :]
        ahead = (row > score) | ((row == score) & (jb > j))
        rank = rank + jnp.where(ahead, 1.0, 0.0)
    return jnp.where(rank < float(NSA_TOP_N), 1.0, 0.0)


def _nsa_cmp_kernel_t(q_ref, kc_ref, vct_ref, gate_ref, o_ref, sel_ref, *, tq, n_cmp):
    qi = pl.program_id(2)
    g = pl.program_id(1)
    q_t = (_stack_heads(q_ref, tq) * (HEAD_DIM ** -0.5)).astype(BF16)
    lc = jnp.dot(kc_ref[...].astype(BF16), q_t, preferred_element_type=F32)
    qpos = qi * tq + lax.broadcasted_iota(jnp.int32, (1, tq), 1)
    blk = lax.broadcasted_iota(jnp.int32, (n_cmp, tq), 0)
    keep = jnp.where(blk < ((qpos + 1) >> BLOCK_SHIFT), 1.0, 0.0)
    keep4 = jnp.concatenate([keep] * NSA_HPG, axis=1)
    lc = jnp.where(keep4 > 0.5, lc, NEG)
    p = jnp.exp(lc - jnp.max(lc, axis=0, keepdims=True)) * keep4
    pc = p / jnp.maximum(jnp.sum(p, axis=0, keepdims=True), 1e-30)
    oc = jnp.dot(vct_ref[...].astype(BF16), pc.astype(BF16), preferred_element_type=F32)
    imp = pc[:, 0:tq]
    for h in range(1, NSA_HPG):
        imp = imp + pc[:, h * tq:(h + 1) * tq]
    sel_ref[...] = _select_mask_t(imp, qpos, n_cmp)
    gates = gate_ref[...]
    for h in range(NSA_HPG):
        grow = jnp.where(g == 0, gates[h:h + 1, :], gates[NSA_HPG + h:NSA_HPG + h + 1, :])
        o_ref[:, h * 64:(h + 1) * 64] = (oc[:, h * tq:(h + 1) * tq] * grow).T


def _nsa_cmp_t(pt, kc, vct, tq):
    nb, _, nt = pt.shape
    n_cmp = kc.shape[2]
    return pl.pallas_call(
        functools.partial(_nsa_cmp_kernel_t, tq=tq, n_cmp=n_cmp),
        grid=(nb, NSA_KV_GROUPS, nt // tq),
        in_specs=[
            pl.BlockSpec((None, 256, tq), lambda b, g, i: (b, R_NQ // 256 + g, i)),
            pl.BlockSpec((None, None, n_cmp, HEAD_DIM), lambda b, g, i: (b, g, 0, 0)),
            pl.BlockSpec((None, None, HEAD_DIM, n_cmp), lambda b, g, i: (b, g, 0, 0)),
            pl.BlockSpec((None, 32, tq), lambda b, g, i: (b, R_GATE // 32, i)),
        ],
        out_specs=[
            pl.BlockSpec((None, tq, 256), lambda b, g, i: (b, i, g)),
            pl.BlockSpec((None, None, n_cmp, tq), lambda b, g, i: (b, g, 0, i)),
        ],
        out_shape=[jax.ShapeDtypeStruct((nb, nt, 512), F32),
                   jax.ShapeDtypeStruct((nb, NSA_KV_GROUPS, n_cmp, nt), F32)],
        compiler_params=_cparams(("parallel", "parallel", "parallel")),
        name="nsa_cmp",
    )(pt, kc, vct, pt)


def _nsa_band_kernel_t(qi_ref, kj_ref, first_ref, last_ref, q_ref, k_ref, v_ref, gate_ref, *rest,
                       tq, tk, selected, gate_base):
    if selected:
        sel_ref, o_ref, q_s, m_s, l_s, acc_s = rest
    else:
        o_ref, q_s, m_s, l_s, acc_s = rest
    p_id = pl.program_id(2)
    g = pl.program_id(1)
    qi = qi_ref[p_id]
    kj = kj_ref[p_id]

    @pl.when(first_ref[p_id] == 1)
    def _():
        q_s[...] = (_stack_heads(q_ref, tq) * (HEAD_DIM ** -0.5)).astype(BF16)
        m_s[...] = jnp.full(m_s.shape, NEG, F32)
        l_s[...] = jnp.zeros(l_s.shape, F32)
        acc_s[...] = jnp.zeros(acc_s.shape, F32)

    def tile(masked):
        s = _kq(k_ref[...].astype(BF16), q_s[...])
        if masked:
            key, qry = _causal_keep(qi, kj, tq, tk)
            mask = key <= qry
            if selected:
                n_blk = sel_ref.shape[0]
                blk_of_key = kj * (tk // NSA_BLOCK) + (lax.broadcasted_iota(jnp.int32, (tk, n_blk), 0) >> BLOCK_SHIFT)
                expand = jnp.where(lax.broadcasted_iota(jnp.int32, (tk, n_blk), 1) == blk_of_key, 1.0, 0.0)
                picked = jnp.dot(expand.astype(BF16), sel_ref[...].astype(BF16), preferred_element_type=F32)
                mask = mask & (picked > 0.5)
            else:
                mask = mask & (key > qry - NSA_WINDOW)
            keep = jnp.where(mask, 1.0, 0.0)
            s = jnp.where(jnp.concatenate([keep] * NSA_HPG, axis=1) > 0.5, s, NEG)
        _flash_step(s, v_ref[...].astype(BF16), m_s, l_s, acc_s)

    if selected:
        tile(True)
    else:
        full = ((kj + 1) * tk - 1 <= qi * tq) & (kj * tk > (qi + 1) * tq - 1 - NSA_WINDOW)
        pl.when(full)(lambda: tile(False))
        pl.when(jnp.logical_not(full))(lambda: tile(True))

    @pl.when(last_ref[p_id] == 1)
    def _():
        gates = gate_ref[...]
        o = acc_s[...] / l_s[...]
        for h in range(NSA_HPG):
            r0 = gate_base + h
            grow = jnp.where(g == 0, gates[r0:r0 + 1, :], gates[r0 + NSA_HPG:r0 + NSA_HPG + 1, :])
            o_ref[:, h * 64:(h + 1) * 64] = (o[:, h * tq:(h + 1) * tq] * grow).T


def _band_pairs(nt, tq, tk, window):
    qi, kj = [], []
    for i in range(nt // tq):
        lo_key = 0 if window is None else max(0, i * tq - (window - 1))
        for j in range(lo_key // tk, ((i + 1) * tq - 1) // tk + 1):
            qi.append(i)
            kj.append(j)
    first = [1 if (p == 0 or qi[p] != qi[p - 1]) else 0 for p in range(len(qi))]
    last = first[1:] + [1]
    return tuple(jnp.asarray(a, jnp.int32) for a in (qi, kj, first, last))


def _nsa_band_t(pt, sel, tq, tk, selected):
    nb, _, nt = pt.shape
    qi, kj, first, last = _band_pairs(nt, tq, tk, None if selected else NSA_WINDOW)
    if selected:
        rk, rv, gate_base = R_NSA + 256, R_NSA + 384, 8
    else:
        rk, rv, gate_base = R_WIN, R_WIN + 128, 16
    idx = lambda f: (lambda b, g, p, qi, kj, fi, la: f(b, g, p, qi, kj))
    in_specs = [
        pl.BlockSpec((None, 256, tq), idx(lambda b, g, p, qi, kj: (b, R_NQ // 256 + g, qi[p]))),
        pl.BlockSpec((None, HEAD_DIM, tk), idx(lambda b, g, p, qi, kj: (b, rk // 64 + g, kj[p]))),
        pl.BlockSpec((None, HEAD_DIM, tk), idx(lambda b, g, p, qi, kj: (b, rv // 64 + g, kj[p]))),
        pl.BlockSpec((None, 32, tq), idx(lambda b, g, p, qi, kj: (b, R_GATE // 32, qi[p]))),
    ]
    args = [pt, pt, pt, pt]
    if selected:
        in_specs.append(pl.BlockSpec((None, None, sel.shape[2], tq),
                                     idx(lambda b, g, p, qi, kj: (b, g, 0, qi[p]))))
        args.append(sel)
    grid_spec = pltpu.PrefetchScalarGridSpec(
        num_scalar_prefetch=4,
        grid=(nb, NSA_KV_GROUPS, qi.shape[0]),
        in_specs=in_specs,
        out_specs=pl.BlockSpec((None, tq, 256), idx(lambda b, g, p, qi, kj: (b, qi[p], g))),
        scratch_shapes=[
            pltpu.VMEM((HEAD_DIM, NSA_HPG * tq), BF16),
            pltpu.VMEM((1, NSA_HPG * tq), F32),
            pltpu.VMEM((1, NSA_HPG * tq), F32),
            pltpu.VMEM((HEAD_DIM, NSA_HPG * tq), F32),
        ],
    )
    return pl.pallas_call(
        functools.partial(_nsa_band_kernel_t, tq=tq, tk=tk, selected=selected, gate_base=gate_base),
        grid_spec=grid_spec,
        out_shape=jax.ShapeDtypeStruct((nb, nt, 512), F32),
        compiler_params=_cparams(("parallel", "parallel", "arbitrary")),
        name="nsa_sel" if selected else "nsa_win",
    )(qi, kj, first, last, *args)


def _layer_norm_rows(x, g, b):
    mu = jnp.mean(x, axis=-1, keepdims=True)
    xc = x - mu
    var = jnp.mean(xc * xc, axis=-1, keepdims=True)
    return xc * lax.rsqrt(var + LN_EPS) * g + b


def _outproj_kernel(fox_ref, n1_ref, n2_ref, n3_ref, diff_ref, h_ref, w_ref, g_ref, b_ref, wr_ref, br_ref,
                    o_ref, lg_ref):
    nsa = (n1_ref[...] + n2_ref[...] + n3_ref[...]).astype(BF16)
    y = jnp.dot(fox_ref[...].astype(BF16), w_ref[0:256, :], preferred_element_type=F32)
    y = y + jnp.dot(nsa, w_ref[256:768, :], preferred_element_type=F32)
    y = y + jnp.dot(diff_ref[...].astype(BF16), w_ref[768:1024, :], preferred_element_type=F32)
    h1 = _layer_norm_rows(DEEPNORM_ALPHA * h_ref[...] + y, g_ref[...], b_ref[...])
    o_ref[...] = h1
    lg_ref[...] = jnp.dot(h1.astype(BF16), wr_ref[...], preferred_element_type=F32) + br_ref[...]


def _outproj(fox, n1, n2, n3, diff, h, w, g, b, wr, br, tm):
    m = h.shape[0]
    row = lambda i: (i, 0)
    const = lambda i: (0, 0)
    return pl.pallas_call(
        _outproj_kernel,
        grid=(m // tm,),
        in_specs=[
            pl.BlockSpec((tm, 256), row), pl.BlockSpec((tm, 512), row), pl.BlockSpec((tm, 512), row),
            pl.BlockSpec((tm, 512), row), pl.BlockSpec((tm, 256), row), pl.BlockSpec((tm, D_MODEL), row),
            pl.BlockSpec((D_MODEL, D_MODEL), const), pl.BlockSpec((1, D_MODEL), const),
            pl.BlockSpec((1, D_MODEL), const), pl.BlockSpec((D_MODEL, LANES), const),
            pl.BlockSpec((1, LANES), const),
        ],
        out_specs=[pl.BlockSpec((tm, D_MODEL), row), pl.BlockSpec((tm, LANES), row)],
        out_shape=[jax.ShapeDtypeStruct((m, D_MODEL), F32), jax.ShapeDtypeStruct((m, LANES), F32)],
        compiler_params=_cparams(("parallel",)),
        name="outproj_ln",
    )(fox, n1, n2, n3, diff, h, w, g, b, wr, br)


PERM_CHUNK = 256


def _deinterleave_kernel(w_ref, o_ref):
    half = w_ref.shape[1] // 2
    hc = PERM_CHUNK // 2
    src = lax.broadcasted_iota(jnp.int32, (PERM_CHUNK, PERM_CHUNK), 0)
    dst = lax.broadcasted_iota(jnp.int32, (PERM_CHUNK, PERM_CHUNK), 1)
    want = jnp.where(dst < hc, 2 * dst, 2 * (dst - hc) + 1)
    perm = jnp.where(src == want, 1.0, 0.0).astype(BF16)
    for c in range(w_ref.shape[1] // PERM_CHUNK):
        y = jnp.dot(w_ref[:, c * PERM_CHUNK:(c + 1) * PERM_CHUNK].astype(BF16), perm, preferred_element_type=F32)
        o_ref[:, c * hc:(c + 1) * hc] = y[:, :hc].astype(BF16)
        o_ref[:, half + c * hc:half + (c + 1) * hc] = y[:, hc:].astype(BF16)


def _deinterleave(w, tr):
    ne, k, n2 = w.shape
    return pl.pallas_call(
        _deinterleave_kernel,
        grid=(ne, k // tr),
        in_specs=[pl.BlockSpec((None, tr, n2), lambda e, i: (e, i, 0))],
        out_specs=pl.BlockSpec((None, tr, n2), lambda e, i: (e, i, 0)),
        out_shape=jax.ShapeDtypeStruct((ne, k, n2), BF16),
        compiler_params=_cparams(("parallel", "parallel")),
        name="moe_deinterleave",
    )(w)


def _moe_kernel(te_ref, nv_ref, x_ref, wu_ref, bg_ref, bl_ref, wd_ref, bd_ref, o_ref):
    i = pl.program_id(0)

    @pl.when(i < nv_ref[0])
    def _():
        hcat = jnp.dot(x_ref[...].astype(BF16), wu_ref[...], preferred_element_type=F32)
        hg = hcat[:, :D_MODEL] + bg_ref[...]
        hl = hcat[:, D_MODEL:] + bl_ref[...]
        glu = jnp.minimum(hg, SWIGLU_LIMIT)
        lin = jnp.clip(hl, -SWIGLU_LIMIT, SWIGLU_LIMIT)
        act = glu * (1.0 / (1.0 + jnp.exp(-SWIGLU_ALPHA * glu))) * (lin + 1.0)
        o_ref[...] = jnp.dot(act.astype(BF16), wd_ref[...].astype(BF16), preferred_element_type=F32) + bd_ref[...]

    @pl.when(i >= nv_ref[0])
    def _():
        o_ref[...] = jnp.zeros(o_ref.shape, F32)


def _moe_experts(tile_e, n_valid, xs, wu, bg, bl, wd, bd, tm):
    n_rows = xs.shape[0]
    ex = lambda i, te, nv: (te[i], 0, 0)
    grid_spec = pltpu.PrefetchScalarGridSpec(
        num_scalar_prefetch=2,
        grid=(n_rows // tm,),
        in_specs=[
            pl.BlockSpec((tm, D_MODEL), lambda i, te, nv: (i, 0)),
            pl.BlockSpec((None, D_MODEL, 2 * D_MODEL), ex),
            pl.BlockSpec((None, 1, D_MODEL), ex),
            pl.BlockSpec((None, 1, D_MODEL), ex),
            pl.BlockSpec((None, D_MODEL, D_MODEL), ex),
            pl.BlockSpec((None, 1, D_MODEL), ex),
        ],
        out_specs=pl.BlockSpec((tm, D_MODEL), lambda i, te, nv: (i, 0)),
    )
    return pl.pallas_call(
        _moe_kernel,
        grid_spec=grid_spec,
        out_shape=jax.ShapeDtypeStruct((n_rows, D_MODEL), F32),
        compiler_params=_cparams(("arbitrary",)),
        name="moe_experts",
    )(tile_e, n_valid, xs, wu, bg, bl, wd, bd)


def _combine_kernel(h_ref, y_ref, gate_ref, g_ref, b_ref, o_ref):
    gates = gate_ref[...]
    f = y_ref[0] * gates[:, 0:1]
    for k in range(1, TOP_K):
        f = f + y_ref[k] * gates[:, k:k + 1]
    o_ref[...] = _layer_norm_rows(DEEPNORM_ALPHA * h_ref[...] + f, g_ref[...], b_ref[...])


def _combine_ln(h1, yg, gates, g, b, tm):
    m = h1.shape[0]
    row = lambda i: (i, 0)
    const = lambda i: (0, 0)
    return pl.pallas_call(
        _combine_kernel,
        grid=(m // tm,),
        in_specs=[
            pl.BlockSpec((tm, D_MODEL), row),
            pl.BlockSpec((TOP_K, tm, D_MODEL), lambda i: (0, i, 0)),
            pl.BlockSpec((tm, TOP_K), row),
            pl.BlockSpec((1, D_MODEL), const), pl.BlockSpec((1, D_MODEL), const),
        ],
        out_specs=pl.BlockSpec((tm, D_MODEL), row),
        out_shape=jax.ShapeDtypeStruct((m, D_MODEL), F32),
        compiler_params=_cparams(("parallel",)),
        name="moe_combine_ln",
    )(h1, yg, gates, g, b)


def _moe_block(h1, logits, wts, ln_g, ln_b, tm_moe, tm_row):
    wu, bg, bl, wd, bd = wts
    m = h1.shape[0]
    top_val, top_idx = lax.top_k(logits, TOP_K)
    gate = jax.nn.softmax(top_val, axis=-1)
    n = m * TOP_K
    flat_e = top_idx.reshape(n).astype(jnp.int32)
    iota_n = jnp.arange(n, dtype=jnp.int32)
    sorted_e, order = lax.sort_key_val(flat_e, iota_n)
    counts = jnp.sum(flat_e[:, None] == jnp.arange(N_EXPERTS, dtype=jnp.int32)[None, :], axis=0, dtype=jnp.int32)
    padded = (counts + tm_moe - 1) // tm_moe * tm_moe
    starts = jnp.cumsum(counts) - counts
    pends = jnp.cumsum(padded)
    gstart = pends - padded
    dest = gstart[sorted_e] + iota_n - starts[sorted_e]
    n_rows = -(-n // tm_moe) * tm_moe + N_EXPERTS * tm_moe
    n_tiles = n_rows // tm_moe
    tile_start = jnp.arange(n_tiles, dtype=jnp.int32) * tm_moe
    tile_e = jnp.minimum(jnp.sum(pends[None, :] <= tile_start[:, None], axis=1, dtype=jnp.int32), N_EXPERTS - 1)
    n_valid = (pends[-1] // tm_moe).astype(jnp.int32).reshape(1)
    k_in_group = (jnp.arange(n_rows, dtype=jnp.int32).reshape(n_tiles, tm_moe) - gstart[tile_e][:, None])
    src = jnp.clip(starts[tile_e][:, None] + k_in_group, 0, n - 1)
    row_tok = jnp.where(k_in_group < counts[tile_e][:, None], (order // TOP_K)[src], m).reshape(n_rows)
    _, pos = lax.sort_key_val(order, dest)
    pos = pos.reshape(m, TOP_K)
    xs = jnp.concatenate([h1, jnp.zeros((1, D_MODEL), F32)], 0)[row_tok]
    out = _moe_experts(tile_e, n_valid, xs, wu, bg, bl, wd, bd, tm_moe)
    yg = out[pos.T]
    return _combine_ln(h1, yg, gate, ln_g, ln_b, tm_row)


def _tree_sum(xs):
    xs = list(xs)
    while len(xs) > 1:
        xs = [xs[i] + xs[i + 1] for i in range(0, len(xs) - 1, 2)] + ([xs[-1]] if len(xs) % 2 else [])
    return xs[0]


def _sublane_allsum(x):
    x = x + pltpu.roll(x, 4, 0)
    x = x + pltpu.roll(x, 2, 0)
    return x + pltpu.roll(x, 1, 0)


def _lane_suffix_scan(x):
    lane = lax.broadcasted_iota(jnp.int32, x.shape, 1)
    k = 1
    while k < LANES:
        x = x + jnp.where(lane + k < LANES, pltpu.roll(x, LANES - k, 1), 0.0)
        k *= 2
    return x


def _stream_update(logits, m_ref, l_ref, valid=None):
    m_prev = m_ref[...]
    m_new = jnp.maximum(m_prev, logits)
    alpha = jnp.exp(m_prev - m_new)
    p = jnp.exp(logits - m_new)
    if valid is not None:
        p = jnp.where(valid, p, 0.0)
    l_ref[...] = l_ref[...] * alpha + p
    m_ref[...] = m_new
    return alpha, p


def _lane_merge(m_ref, l_ref):
    m = m_ref[...]
    w = jnp.exp(m - jnp.max(m, axis=1, keepdims=True))
    tot = jnp.sum(l_ref[...] * w, axis=1, keepdims=True)
    return w, 1.0 / tot


FOX_WIN = 520
FOX_VROW0 = 256


def _fox_sample_kernel(pt_ref, *refs, kpages, r0):
    pages = refs[:2 * kpages]
    qc_ref, new_ref, o_ref, qcb_s, m_s, l_s, s_s, acc_s, scan_s = refs[2 * kpages:]
    j = pl.program_id(1)
    sub = lax.broadcasted_iota(jnp.int32, (SUBLANES, LANES), 0)
    lane = lax.broadcasted_iota(jnp.int32, (SUBLANES, LANES), 1)
    lf_rows = (sub >= r0) & (sub < r0 + FOX_HEADS)
    n_kv = FOX_VROW0 // SUBLANES + 1

    def page(main, tail, valid):
        t = _tree_sum([main(i) * qcb_s[8 * i:8 * i + 8, :] for i in range(n_kv)])
        qk = t + pltpu.roll(t, 4, 0)
        lf = jnp.where(lf_rows, tail, 0.0)
        lf = lf + pltpu.roll(lf, 4, 0)
        if valid is not None:
            lf = jnp.where(valid, lf, 0.0)
        parts = jnp.concatenate([t.astype(F32) for t in _split3(lf)], axis=0).astype(BF16)
        scan = jnp.dot(parts, scan_s[...], preferred_element_type=F32)
        scan = scan[0:8, :] + scan[8:16, :] + scan[16:24, :]
        later, total = scan[:, 0:LANES], scan[:, LANES:2 * LANES]
        logits = qk + s_s[...] + later
        if valid is not None:
            logits = jnp.where(valid, logits, NEG)
        alpha, p = _stream_update(logits, m_s, l_s, valid)
        n_main = (FOX_WIN - SUBLANES - FOX_VROW0) // SUBLANES
        for i in range(n_main):
            r = 8 * i
            acc_s[r:r + 8, :] = acc_s[r:r + 8, :] * alpha + main(FOX_VROW0 // SUBLANES + i) * p
        r = 8 * n_main
        acc_s[r:r + 8, :] = acc_s[r:r + 8, :] * alpha + tail * p
        s_s[...] = s_s[...] + total

    @pl.when(j == 0)
    def _():
        src = lax.broadcasted_iota(jnp.int32, (LANES, 2 * LANES), 0)
        dst = lax.broadcasted_iota(jnp.int32, (LANES, 2 * LANES), 1)
        scan_s[...] = jnp.where((dst >= LANES) | (src > dst), 1.0, 0.0).astype(BF16)
        qcb_s[...] = jnp.broadcast_to(qc_ref[...], qcb_s.shape)
        m_s[...] = jnp.full(m_s.shape, NEG, F32)
        l_s[...] = jnp.zeros(l_s.shape, F32)
        s_s[...] = jnp.zeros(s_s.shape, F32)
        acc_s[...] = jnp.zeros(acc_s.shape, F32)
        page(lambda i: jnp.broadcast_to(new_ref[8 * i:8 * i + 8, :], (SUBLANES, LANES)),
             jnp.broadcast_to(new_ref[FOX_WIN - 8:FOX_WIN, :], (SUBLANES, LANES)), lane == 0)

    for k in range(kpages):
        main_ref, tail_ref = pages[2 * k], pages[2 * k + 1]
        page(lambda i, ref=main_ref: ref[8 * i:8 * i + 8, :], tail_ref[...], None)

    @pl.when(j == pl.num_programs(1) - 1)
    def _():
        w, inv = _lane_merge(m_s, l_s)
        for i in range((FOX_WIN - FOX_VROW0) // SUBLANES):
            r = 8 * i
            o_ref[r:r + 8, :] = jnp.sum(acc_s[r:r + 8, :] * w, axis=1, keepdims=True) * inv


def _fox_sample_attn(cache_fm, page_table, layer, qcol, newcol, kpages):
    nb, n_pages = page_table.shape
    r0 = 4 * layer
    n_chunks = n_pages // kpages

    def page_idx(b, j, pt, k):
        return pt[b, n_pages - 1 - (j * kpages + k)]

    in_specs = []
    for k in range(kpages):
        in_specs.append(pl.BlockSpec((None, 512, LANES), lambda b, j, pt, k=k: (page_idx(b, j, pt, k), layer, 0)))
        in_specs.append(pl.BlockSpec((None, 8, LANES), lambda b, j, pt, k=k: (page_idx(b, j, pt, k), 64 * (layer + 1), 0)))
    in_specs += [pl.BlockSpec((None, FOX_WIN, 1), lambda b, j, pt: (b, 0, 0)),
                 pl.BlockSpec((None, FOX_WIN, 1), lambda b, j, pt: (b, 0, 0))]
    n_out = FOX_WIN - FOX_VROW0
    grid_spec = pltpu.PrefetchScalarGridSpec(
        num_scalar_prefetch=1,
        grid=(nb, n_chunks),
        in_specs=in_specs,
        out_specs=pl.BlockSpec((None, n_out, 1), lambda b, j, pt: (b, 0, 0)),
        scratch_shapes=[
            pltpu.VMEM((FOX_WIN, LANES), F32),
            pltpu.VMEM((SUBLANES, LANES), F32), pltpu.VMEM((SUBLANES, LANES), F32), pltpu.VMEM((SUBLANES, LANES), F32),
            pltpu.VMEM((n_out, LANES), F32),
            pltpu.VMEM((LANES, 2 * LANES), BF16),
        ],
    )
    return pl.pallas_call(
        functools.partial(_fox_sample_kernel, kpages=kpages, r0=r0),
        grid_spec=grid_spec,
        out_shape=jax.ShapeDtypeStruct((nb, n_out, 1), F32),
        compiler_params=_cparams(("parallel", "arbitrary")),
        name="fox_sample",
    )(page_table, *([cache_fm] * (2 * kpages)), qcol, newcol)


def _bcast_row(x, s):
    return jnp.broadcast_to(x[s:s + 1, :], x.shape)


def _pack_streams(ts):
    sub = lax.broadcasted_iota(jnp.int32, ts[0].shape, 0)
    out = ts[0]
    for s in range(1, len(ts)):
        out = jnp.where(sub == s, ts[s], out)
    return out


def _kv_page_update(x, qcb_s, k_rows, v_rows, m_s, l_s, acc_s, valid):
    ts = []
    for s, (g0, cnt, c0) in enumerate(k_rows):
        t = _tree_sum([x(g0 + i) * qcb_s[c0 + 8 * i:c0 + 8 * i + 8, :] for i in range(cnt)])
        ts.append(_sublane_allsum(t))
    logits = _pack_streams(ts)
    if valid is not None:
        logits = jnp.where(valid, logits, NEG)
    alpha, p = _stream_update(logits, m_s, l_s, valid)
    for s, g0 in enumerate(v_rows):
        ab, pb = _bcast_row(alpha, s), _bcast_row(p, s)
        for i in range(8):
            r = s * 64 + 8 * i
            acc_s[r:r + 8, :] = acc_s[r:r + 8, :] * ab + x(g0 + i) * pb


def _merged_rows(acc_s, w, inv, s):
    wb = _bcast_row(w, s)
    return [jnp.sum(acc_s[s * 64 + 8 * i:s * 64 + 8 * i + 8, :] * wb, axis=1, keepdims=True) * inv[s:s + 1, :]
            for i in range(8)]


_DIFF_K_ROWS = tuple((16 * (s // 2) + 4 * (s % 2), 4, 128 * (s // 2) + 32 * (s % 2)) for s in range(8))
_DIFF_V_ROWS = tuple(16 * (s // 2) + 8 for s in range(8))


def _diff_sample_kernel(pt_ref, lam_ref, *refs, kpages, out_scale):
    pages = refs[:kpages]
    qc_ref, new_ref, g_ref, o_ref, qcb_s, m_s, l_s, acc_s = refs[kpages:]
    j = pl.program_id(1)
    lane = lax.broadcasted_iota(jnp.int32, (SUBLANES, LANES), 1)

    @pl.when(j == 0)
    def _():
        qcb_s[...] = jnp.broadcast_to(qc_ref[...], qcb_s.shape)
        m_s[...] = jnp.full(m_s.shape, NEG, F32)
        l_s[...] = jnp.zeros(l_s.shape, F32)
        acc_s[...] = jnp.zeros(acc_s.shape, F32)
        _kv_page_update(lambda i: jnp.broadcast_to(new_ref[8 * i:8 * i + 8, :], (SUBLANES, LANES)),
                        qcb_s, _DIFF_K_ROWS, _DIFF_V_ROWS, m_s, l_s, acc_s, lane == 0)

    for k in range(kpages):
        _kv_page_update(lambda i, ref=pages[k]: ref[8 * i:8 * i + 8, :],
                        qcb_s, _DIFF_K_ROWS, _DIFF_V_ROWS, m_s, l_s, acc_s, None)

    @pl.when(j == pl.num_programs(1) - 1)
    def _():
        w, inv = _lane_merge(m_s, l_s)
        lam = lam_ref[0]
        for h in range(DIFF_HEADS):
            o1 = _merged_rows(acc_s, w, inv, 2 * h)
            o2 = _merged_rows(acc_s, w, inv, 2 * h + 1)
            o = [a - lam * b for a, b in zip(o1, o2)]
            ss = o[0] * o[0]
            for i in range(1, 8):
                ss = ss + o[i] * o[i]
            r = lax.rsqrt(_sublane_allsum(ss) * (1.0 / HEAD_DIM) + RMS_EPS)
            for i in range(8):
                o_ref[h * 64 + 8 * i:h * 64 + 8 * i + 8, :] = o[i] * r * g_ref[8 * i:8 * i + 8, :] * out_scale


def _diff_sample_attn(cache_fm, page_table, layer, lam, qcol, newcol, gcol, out_scale, kpages):
    nb, n_pages = page_table.shape
    in_specs = [pl.BlockSpec((None, 512, LANES), lambda b, j, pt, lam, k=k: (pt[b, j * kpages + k], layer, 0))
                for k in range(kpages)]
    in_specs += [pl.BlockSpec((None, 512, 1), lambda b, j, pt, lam: (b, 0, 0)),
                 pl.BlockSpec((None, 512, 1), lambda b, j, pt, lam: (b, 0, 0)),
                 pl.BlockSpec((HEAD_DIM, 1), lambda b, j, pt, lam: (0, 0))]
    grid_spec = pltpu.PrefetchScalarGridSpec(
        num_scalar_prefetch=2,
        grid=(nb, n_pages // kpages),
        in_specs=in_specs,
        out_specs=pl.BlockSpec((None, 256, 1), lambda b, j, pt, lam: (b, 0, 0)),
        scratch_shapes=[pltpu.VMEM((512, LANES), F32), pltpu.VMEM((SUBLANES, LANES), F32),
                        pltpu.VMEM((SUBLANES, LANES), F32), pltpu.VMEM((512, LANES), F32)],
    )
    return pl.pallas_call(
        functools.partial(_diff_sample_kernel, kpages=kpages, out_scale=out_scale),
        grid_spec=grid_spec,
        out_shape=jax.ShapeDtypeStruct((nb, 256, 1), F32),
        compiler_params=_cparams(("parallel", "arbitrary")),
        name="diff_sample",
    )(page_table, lam, *([cache_fm] * kpages), qcol, newcol, gcol)


def _nsa_compress_sample_kernel(pt_ref, *refs, kpages, n_pages):
    pages = refs[:kpages]
    w2_ref, pe2_ref, o_ref, buf_s = refs[kpages:]
    j = pl.program_id(1)
    chunk = jnp.stack([pages[k][...] for k in range(kpages)], axis=0)
    buf_s[:, pl.ds(pl.multiple_of(j * kpages, kpages), kpages), :] = pltpu.einshape("prl->rpl", chunk)

    @pl.when(j == pl.num_programs(1) - 1)
    def _():
        for t in range(2):
            acc = jnp.zeros((2 * n_pages, LANES), F32)
            for d in range(HEAD_DIM):
                x = jnp.concatenate([buf_s[t * 128 + g * 64 + d] for g in range(NSA_KV_GROUPS)], axis=0)
                x = (x + pe2_ref[t, d:d + 1, :]).astype(BF16)
                acc = acc + jnp.dot(x, w2_ref[t, d], preferred_element_type=F32)
            o_ref[t] = acc


def _nsa_compress_sample(cache_fm, page_table, layer, w2, pe2, kpages):
    nb, n_pages = page_table.shape
    in_specs = [pl.BlockSpec((None, 256, LANES), lambda b, j, pt, k=k: (pt[b, j * kpages + k], 2 * layer, 0))
                for k in range(kpages)]
    in_specs += [pl.BlockSpec((2, HEAD_DIM, LANES, LANES), lambda b, j, pt: (0, 0, 0, 0)),
                 pl.BlockSpec((2, HEAD_DIM, LANES), lambda b, j, pt: (0, 0, 0))]
    grid_spec = pltpu.PrefetchScalarGridSpec(
        num_scalar_prefetch=1,
        grid=(nb, n_pages // kpages),
        in_specs=in_specs,
        out_specs=pl.BlockSpec((None, 2, 2 * n_pages, LANES), lambda b, j, pt: (b, 0, 0, 0)),
        scratch_shapes=[pltpu.VMEM((256, n_pages, LANES), F32)],
    )
    return pl.pallas_call(
        functools.partial(_nsa_compress_sample_kernel, kpages=kpages, n_pages=n_pages),
        grid_spec=grid_spec,
        out_shape=jax.ShapeDtypeStruct((nb, 2, 2 * n_pages, LANES), F32),
        compiler_params=_cparams(("parallel", "arbitrary")),
        name="nsa_compress_sample",
    )(page_table, *([cache_fm] * kpages), w2, pe2)


_NSA_K_ROWS = tuple((8 * (s // NSA_HPG), 8, 64 * s) for s in range(NSA_HEADS))
_NSA_V_ROWS = tuple(16 + 8 * (s // NSA_HPG) for s in range(NSA_HEADS))


def _col_to_row(col):
    n = col.shape[0]
    eye = lax.broadcasted_iota(jnp.int32, (n, n), 0) == lax.broadcasted_iota(jnp.int32, (n, n), 1)
    return jnp.sum(jnp.where(eye, col, 0.0), axis=0, keepdims=True)


def _nsa_sample_kernel(pt_ref, *refs, kpages, n_pages):
    pages = refs[:kpages]
    (kv_ref, qh_ref, qc_ref, newsel_ref, newwin_ref, win_ref, g0_ref, g12_ref,
     oc_ref, osw_ref, wout_ref,
     qcb_s, sel_s, m_s, l_s, acc_s, mw_s, lw_s, accw_s) = refs[kpages:]
    j = pl.program_id(1)
    lane = lax.broadcasted_iota(jnp.int32, (SUBLANES, LANES), 1)
    scale = HEAD_DIM ** -0.5
    n_blk = 2 * n_pages

    @pl.when(j == 0)
    def _():
        qcb_s[...] = jnp.broadcast_to(qc_ref[...], qcb_s.shape)
        for ref in (m_s, mw_s):
            ref[...] = jnp.full(ref.shape, NEG, F32)
        for ref in (l_s, acc_s, lw_s, accw_s):
            ref[...] = jnp.zeros(ref.shape, F32)
        imp = [jnp.zeros((n_pages, 1), F32), jnp.zeros((n_pages, 1), F32)]
        for g in range(NSA_KV_GROUPS):
            kc = kv_ref[0, g * n_pages:(g + 1) * n_pages, :].astype(BF16)
            vc = kv_ref[1, g * n_pages:(g + 1) * n_pages, :].astype(BF16)
            lc = [jnp.dot(kc, qh_ref[g, hf].astype(BF16), preferred_element_type=F32) * scale for hf in range(2)]
            mx = jnp.maximum(jnp.max(lc[0], axis=0, keepdims=True), jnp.max(lc[1], axis=0, keepdims=True))
            pe = [jnp.exp(c - mx) for c in lc]
            den = jnp.maximum(jnp.sum(pe[0], axis=0, keepdims=True) + jnp.sum(pe[1], axis=0, keepdims=True), 1e-30)
            pc = [e / den for e in pe]
            res = [lax.dot_general(c.astype(BF16), vc, (((0,), (0,)), ((), ())), preferred_element_type=F32)
                   for c in pc]
            oc = res[0][:, 0:HEAD_DIM] + res[1][:, HEAD_DIM:2 * HEAD_DIM]
            oc_ref[g * NSA_HPG:(g + 1) * NSA_HPG, :] = oc * g0_ref[g * NSA_HPG:(g + 1) * NSA_HPG, :]
            for hf in range(2):
                imp[hf] = imp[hf] + jnp.sum(pc[hf], axis=1, keepdims=True)
            pidx = lax.broadcasted_iota(jnp.int32, (n_pages, 1), 0)
            score = [jnp.where(pidx == 0, FORCE_SCORE, imp[0]), jnp.where(pidx == n_pages - 1, FORCE_SCORE, imp[1])]
            rows = [_col_to_row(c) for c in score]
            prow = lax.broadcasted_iota(jnp.int32, (n_pages, n_pages), 1)
            pcol = lax.broadcasted_iota(jnp.int32, (n_pages, n_pages), 0)
            sel = []
            for hf in range(2):
                cnt = jnp.zeros((n_pages, 1), F32)
                for hf2 in range(2):
                    before = (2 * prow + hf2) < (2 * pcol + hf)
                    ahead = (rows[hf2] > score[hf]) | ((rows[hf2] == score[hf]) & before)
                    cnt = cnt + jnp.sum(jnp.where(ahead, 1.0, 0.0), axis=1, keepdims=True)
                cnt = cnt + jnp.where(score[hf] < FORCE_SCORE, 1.0, 0.0)
                sel.append(jnp.where(cnt < float(NSA_TOP_N), 1.0, 0.0))
            lane_p = lax.broadcasted_iota(jnp.int32, (n_pages, LANES), 1)
            sel_s[g] = jnp.where(lane_p < NSA_BLOCK, sel[0], sel[1])
            imp = [jnp.zeros((n_pages, 1), F32), jnp.zeros((n_pages, 1), F32)]
        _kv_page_update(lambda i: jnp.broadcast_to(newsel_ref[8 * i:8 * i + 8, :], (SUBLANES, LANES)),
                        qcb_s, _NSA_K_ROWS, _NSA_V_ROWS, m_s, l_s, acc_s, lane == 0)
        _kv_page_update(lambda i: jnp.broadcast_to(newwin_ref[8 * i:8 * i + 8, :], (SUBLANES, LANES)),
                        qcb_s, _NSA_K_ROWS, _NSA_V_ROWS, mw_s, lw_s, accw_s, lane == 0)
        for c in range(NSA_WINDOW // LANES):
            valid = (lane >= 1) if c == 0 else None
            _kv_page_update(lambda i, c=c: win_ref[8 * i:8 * i + 8, c * LANES:(c + 1) * LANES],
                            qcb_s, _NSA_K_ROWS, _NSA_V_ROWS, mw_s, lw_s, accw_s, valid)
        lane_w = lax.broadcasted_iota(jnp.int32, (256, NSA_WINDOW), 1)
        shifted = pltpu.roll(win_ref[...], NSA_WINDOW - 1, 1)
        wout_ref[...] = jnp.where(lane_w == NSA_WINDOW - 1, newwin_ref[...], shifted)

    sub = lax.broadcasted_iota(jnp.int32, (SUBLANES, LANES), 0)
    for k in range(kpages):
        p_idx = j * kpages + k
        picked = jnp.where(sub < NSA_HPG, jnp.broadcast_to(sel_s[0, pl.ds(p_idx, 1), :], (SUBLANES, LANES)),
                           jnp.broadcast_to(sel_s[1, pl.ds(p_idx, 1), :], (SUBLANES, LANES)))
        _kv_page_update(lambda i, ref=pages[k]: ref[8 * i:8 * i + 8, :],
                        qcb_s, _NSA_K_ROWS, _NSA_V_ROWS, m_s, l_s, acc_s, picked > 0.5)

    @pl.when(j == pl.num_programs(1) - 1)
    def _():
        w, inv = _lane_merge(m_s, l_s)
        ww, invw = _lane_merge(mw_s, lw_s)
        for s in range(NSA_HEADS):
            o_sel = _merged_rows(acc_s, w, inv, s)
            o_win = _merged_rows(accw_s, ww, invw, s)
            for i in range(8):
                r = s * 64 + 8 * i
                osw_ref[r:r + 8, :] = o_sel[i] * g12_ref[0, r:r + 8, :] + o_win[i] * g12_ref[1, r:r + 8, :]


def _nsa_sample_attn(cache_fm, win_fm, page_table, layer, kv, qh, qcol, newsel, newwin, g0, g12, kpages):
    nb, n_pages = page_table.shape
    c3 = lambda b, j, pt: (b, 0, 0)
    c4 = lambda b, j, pt: (b, 0, 0, 0)
    in_specs = [pl.BlockSpec((None, 256, LANES), lambda b, j, pt, k=k: (pt[b, j * kpages + k], 2 * layer + 1, 0))
                for k in range(kpages)]
    in_specs += [
        pl.BlockSpec((None, 2, 2 * n_pages, LANES), c4),
        pl.BlockSpec((None, NSA_KV_GROUPS, 2, LANES, NSA_HPG), lambda b, j, pt: (b, 0, 0, 0, 0)),
        pl.BlockSpec((None, 512, 1), c3),
        pl.BlockSpec((None, 256, 1), c3),
        pl.BlockSpec((None, 256, 1), c3),
        pl.BlockSpec((None, None, 256, NSA_WINDOW), lambda b, j, pt: (b, layer, 0, 0)),
        pl.BlockSpec((None, NSA_HEADS, HEAD_DIM), c3),
        pl.BlockSpec((None, 2, 512, 1), c4),
    ]
    grid_spec = pltpu.PrefetchScalarGridSpec(
        num_scalar_prefetch=1,
        grid=(nb, n_pages // kpages),
        in_specs=in_specs,
        out_specs=[pl.BlockSpec((None, NSA_HEADS, HEAD_DIM), c3),
                   pl.BlockSpec((None, 512, 1), c3),
                   pl.BlockSpec((None, 256, NSA_WINDOW), c3)],
        scratch_shapes=[
            pltpu.VMEM((512, LANES), F32), pltpu.VMEM((NSA_KV_GROUPS, n_pages, LANES), F32),
            pltpu.VMEM((SUBLANES, LANES), F32), pltpu.VMEM((SUBLANES, LANES), F32), pltpu.VMEM((512, LANES), F32),
            pltpu.VMEM((SUBLANES, LANES), F32), pltpu.VMEM((SUBLANES, LANES), F32), pltpu.VMEM((512, LANES), F32),
        ],
    )
    return pl.pallas_call(
        functools.partial(_nsa_sample_kernel, kpages=kpages, n_pages=n_pages),
        grid_spec=grid_spec,
        out_shape=[jax.ShapeDtypeStruct((nb, NSA_HEADS, HEAD_DIM), F32),
                   jax.ShapeDtypeStruct((nb, 512, 1), F32),
                   jax.ShapeDtypeStruct((nb, 256, NSA_WINDOW), F32)],
        compiler_params=_cparams(("parallel", "arbitrary")),
        name="nsa_sample",
    )(page_table, *([cache_fm] * kpages), kv, qh, qcol, newsel, newwin, win_fm, g0, g12)


SampleProj = collections.namedtuple(
    "SampleProj", ["fq", "fox_rows", "nq", "ng", "nsa_rows", "win_rows", "dq", "diff_rows"])


def _fox_cache_view(c):
    return c.transpose(0, 1, 4, 3, 2).reshape(c.shape[0], 2 * 516, PAGE_SIZE)


def _kv_cache_view(c):
    return c.transpose(0, 1, 3, 4, 5, 2).reshape(c.shape[0], 2 * 512, PAGE_SIZE)


def _win_state_view(s):
    return s.transpose(0, 1, 3, 4, 5, 2).reshape(s.shape[0], DEPTH, 256, NSA_WINDOW)


def _fox_sample(cache_fm, page_table, layer, sp, kpages):
    nb = sp.fq.shape[0]
    r0 = 4 * layer
    qk = (sp.fq.reshape(nb, FOX_HEADS, HEAD_DIM) * (HEAD_DIM ** -0.5)).transpose(0, 2, 1).reshape(nb, 256)
    qcol = jnp.zeros((nb, FOX_WIN), F32).at[:, r0:r0 + 256].set(qk)
    rows = sp.fox_rows.reshape(nb, FOX_HEADS, 129).transpose(0, 2, 1).reshape(nb, 516)
    newcol = jnp.zeros((nb, FOX_WIN), F32).at[:, r0:r0 + 516].set(rows)
    o = _fox_sample_attn(cache_fm, page_table, layer, qcol[..., None], newcol[..., None], kpages)[..., 0]
    return o[:, r0:r0 + 256].reshape(nb, HEAD_DIM, FOX_HEADS).transpose(0, 2, 1).reshape(nb, 256)


def _diff_sample(cache_fm, page_table, layer, sp, lam, lam_init, g, kpages):
    nb = sp.dq.shape[0]
    q = sp.dq.reshape(nb, DIFF_HEADS, HEAD_DIM) * (DIFF_QK_DIM ** -0.5)
    qcol = jnp.concatenate([q, jnp.zeros_like(q)], axis=-1).reshape(nb, 512, 1)
    newcol = sp.diff_rows.reshape(nb, 512, 1)
    o = _diff_sample_attn(cache_fm, page_table, layer, lam.reshape(1).astype(F32), qcol, newcol,
                          g.reshape(HEAD_DIM, 1), 1.0 - lam_init, kpages)
    return o.reshape(nb, 256)


def _nsa_sample(cache_fm, win_fm, page_table, layer, sp, cmp_pos, cmp_w, kpages):
    nb = sp.nq.shape[0]
    w = cmp_w.transpose(0, 2, 1, 3)
    z = jnp.zeros_like(w)
    w2 = jnp.concatenate([jnp.concatenate([w, z], -1), jnp.concatenate([z, w], -1)], -2).astype(BF16)
    pe = cmp_pos.transpose(0, 2, 1)
    pe2 = jnp.concatenate([pe, pe], -1)
    kv = _nsa_compress_sample(cache_fm, page_table, layer, w2, pe2, kpages)
    q = sp.nq.reshape(nb, NSA_KV_GROUPS, NSA_HPG, HEAD_DIM)
    qt = q.transpose(0, 1, 3, 2)
    zq = jnp.zeros_like(qt)
    qh = jnp.stack([jnp.concatenate([qt, zq], 2), jnp.concatenate([zq, qt], 2)], 2)
    qcol = (sp.nq.reshape(nb, 512) * (HEAD_DIM ** -0.5))[..., None]
    newsel = sp.nsa_rows.reshape(nb, 512)[:, 256:, None]
    newwin = sp.win_rows.reshape(nb, 256, 1)
    gates = sp.ng.reshape(nb, 3, NSA_HEADS)
    g0 = jnp.broadcast_to(gates[:, 0, :, None], (nb, NSA_HEADS, HEAD_DIM))
    g12 = jnp.broadcast_to(gates[:, 1:, :, None], (nb, 2, NSA_HEADS, HEAD_DIM)).reshape(nb, 2, 512, 1)
    oc, osw, wnew = _nsa_sample_attn(cache_fm, win_fm, page_table, layer,
                                     kv, qh, qcol, newsel, newwin, g0, g12, kpages)
    win_new = wnew.reshape(nb, 2, NSA_KV_GROUPS, HEAD_DIM, NSA_WINDOW).transpose(0, 4, 1, 2, 3)
    return oc.reshape(nb, 512) + osw.reshape(nb, 512), win_new


def _diff_lambda(lp, layer):
    lam_init = 0.8 - 0.6 * math.exp(-0.3 * layer)
    lp = lp.astype(F32)
    lam = jnp.exp(jnp.sum(lp[0] * lp[1])) - jnp.exp(jnp.sum(lp[2] * lp[3])) + lam_init
    return lam, lam_init


def _pad_rows(a, n):
    return jnp.concatenate([a, jnp.zeros((n - a.shape[0],) + a.shape[1:], a.dtype)], 0)


def kernel(x_prompt, x_sample, cache_fox, cache_nsa, cache_diff, state_nsa_win, page_table,
           w_in, b_forget, b_nsa_gate, nsa_cmp_pos, nsa_cmp_w, diff_lambda, diff_subln_g,
           w_out, ln_attn_g, ln_attn_b, w_router, b_router, w_up, b_up, w_down, b_down,
           ln_ffn_g, ln_ffn_b):
    nb, seq, _ = x_prompt.shape
    n_dec, n_new, _ = x_sample.shape
    past = page_table.shape[1] * cache_fox.shape[2]
    tabs_p = _rope_tables(jnp.arange(seq, dtype=F32))
    n_dec_pad = LANES
    tabs_s = _rope_tables(jnp.full((n_dec_pad,), past, F32))
    perm = jnp.asarray(np.maximum(_PERM, 0), jnp.int32)
    valid = jnp.asarray((_PERM >= 0).astype(np.float32))[:, None]

    assert n_new == 1 and past % NSA_BLOCK == 0 and state_nsa_win.shape[2] == NSA_WINDOW
    fox_fm = _fox_cache_view(cache_fox)
    nsa_fm = _kv_cache_view(cache_nsa)
    diff_fm = _kv_cache_view(cache_diff)
    win_fm = _win_state_view(state_nsa_win)

    hp = x_prompt
    hs = x_sample.reshape(n_dec, D_MODEL)
    m_p = nb * seq
    m_all = m_p + n_dec
    m_pad = -(-m_all // 512) * 512
    outs = {k: [] for k in ('fox_p', 'fox_s', 'nsa_p', 'nsa_s', 'diff_p', 'diff_s', 'win_p', 'win_s')}
    for l in range(DEPTH):
        lam, lam_init = _diff_lambda(diff_lambda[l], l)
        lam_arr = lam.reshape(1).astype(F32)
        wt = (jnp.take(w_in[l], perm, axis=1).T * valid).astype(BF16)
        wg_tok = jnp.pad(w_in[l][:, C_NG:C_NG + 24], ((0, 0), (0, LANES - 24))).astype(BF16)
        bgate_col = jnp.pad(b_nsa_gate[l], (0, 8)).reshape(32, 1)
        bgate_row = jnp.pad(b_nsa_gate[l], (0, LANES - 24)).reshape(1, LANES)
        bforget_col = jnp.pad(b_forget[l], (0, 4)).reshape(8, 1)
        w_out_b = w_out[l].astype(BF16)
        wr = jnp.pad(w_router[l], ((0, 0), (0, LANES - N_EXPERTS))).astype(BF16)
        br = jnp.pad(b_router[l], (0, LANES - N_EXPERTS)).reshape(1, LANES)
        moe_w = (_deinterleave(w_up[l], 512), b_up[l][:, None, 0::2], b_up[l][:, None, 1::2],
                 w_down[l], b_down[l][:, None, :])
        g1, b1 = ln_attn_g[l].reshape(1, -1), ln_attn_b[l].reshape(1, -1)
        g2, b2 = ln_ffn_g[l].reshape(1, -1), ln_ffn_b[l].reshape(1, -1)

        pt, gates = _inproj(hp, wt, wg_tok, bgate_col, bforget_col, bgate_row, tabs_p, 256)
        outs['fox_p'].append(pt[:, R_FOX:R_FOX + 516].reshape(nb, 129, FOX_HEADS, seq).transpose(0, 3, 2, 1))
        outs['nsa_p'].append(pt[:, R_NSA:R_NSA + 512].reshape(nb, 4, NSA_KV_GROUPS, HEAD_DIM, seq)
                             .transpose(0, 4, 1, 2, 3))
        outs['diff_p'].append(pt[:, R_DIFF:R_DIFF + 512].reshape(nb, DIFF_HEADS, 2, HEAD_DIM, seq)
                              .transpose(0, 4, 1, 2, 3))
        outs['win_p'].append(pt[:, R_WIN:R_WIN + 256, seq - NSA_WINDOW:]
                             .reshape(nb, 2, NSA_KV_GROUPS, HEAD_DIM, NSA_WINDOW).transpose(0, 4, 1, 2, 3))
        c = jnp.cumsum(pt[:, R_LOGF:R_LOGF + FOX_HEADS, :], axis=-1)
        o_fox = _fox_attention_t(pt, c, 512)
        o_diff = _diff_attention_t(pt, lam_arr, diff_subln_g[l].reshape(HEAD_DIM, 1), 1.0 - lam_init, 512)
        n_cmp = seq // NSA_BLOCK
        kcv = pt[:, R_NSA:R_NSA + 256].reshape(nb, 2, NSA_KV_GROUPS, HEAD_DIM, n_cmp, NSA_BLOCK)
        xb = kcv.transpose(1, 0, 2, 4, 5, 3).reshape(2, nb * NSA_KV_GROUPS * n_cmp, NSA_BLOCK * HEAD_DIM)
        cmp = _compress(xb, nsa_cmp_pos[l].reshape(2, 1, NSA_BLOCK * HEAD_DIM),
                        nsa_cmp_w[l].reshape(2, NSA_BLOCK * HEAD_DIM, HEAD_DIM), 128)
        cmp = cmp.reshape(2, nb, NSA_KV_GROUPS, n_cmp, HEAD_DIM)
        o_cmp, sel = _nsa_cmp_t(pt, cmp[0], cmp[1].transpose(0, 1, 3, 2), 256)
        o_sel = _nsa_band_t(pt, sel, 256, NSA_SEL_TK, True)
        o_win = _nsa_band_t(pt, None, 256, 256, False)

        xs_pad = _pad_rows(hs, n_dec_pad)[None]
        pt_s, _ = _inproj(xs_pad, wt, wg_tok, bgate_col, bforget_col, bgate_row, tabs_s, n_dec_pad)
        ps = pt_s[0, :, :n_dec].T
        sp = SampleProj(
            fq=ps[:, R_FQ:R_FQ + 256].reshape(n_dec, 1, FOX_HEADS, HEAD_DIM),
            fox_rows=ps[:, R_FOX:R_FOX + 516].reshape(n_dec, 1, 129, FOX_HEADS).transpose(0, 1, 3, 2),
            nq=ps[:, R_NQ:R_NQ + 512].reshape(n_dec, 1, NSA_HEADS, HEAD_DIM),
            ng=ps[:, R_GATE:R_GATE + 24].reshape(n_dec, 1, 3, NSA_HEADS),
            nsa_rows=ps[:, R_NSA:R_NSA + 512].reshape(n_dec, 1, 4, NSA_KV_GROUPS, HEAD_DIM),
            win_rows=ps[:, R_WIN:R_WIN + 256].reshape(n_dec, 1, 2, NSA_KV_GROUPS, HEAD_DIM),
            dq=ps[:, R_DQ:R_DQ + 256].reshape(n_dec, 1, DIFF_HEADS, 2, DIFF_QK_DIM),
            diff_rows=ps[:, R_DIFF:R_DIFF + 512].reshape(n_dec, 1, DIFF_HEADS, 2, HEAD_DIM))
        s_fox = _fox_sample(fox_fm, page_table, l, sp, SAMPLE_PAGES_PER_STEP)
        s_nsa, win_new = _nsa_sample(nsa_fm, win_fm, page_table, l, sp, nsa_cmp_pos[l], nsa_cmp_w[l],
                                     SAMPLE_PAGES_PER_STEP)
        s_diff = _diff_sample(diff_fm, page_table, l, sp, lam, lam_init, diff_subln_g[l], SAMPLE_PAGES_PER_STEP)
        outs['fox_s'].append(sp.fox_rows)
        outs['nsa_s'].append(sp.nsa_rows)
        outs['diff_s'].append(sp.diff_rows)
        outs['win_s'].append(win_new)

        def rows(a_p, a_s):
            return _pad_rows(jnp.concatenate([a_p.reshape(m_p, -1), a_s], 0), m_pad)

        zero_s = jnp.zeros((n_dec, 512), F32)
        h_all = rows(hp, hs)
        h1, logits = _outproj(rows(o_fox, s_fox), rows(o_cmp, s_nsa), rows(o_sel, zero_s), rows(o_win, zero_s),
                              rows(o_diff, s_diff), h_all, w_out_b, g1, b1, wr, br, 512)
        h2 = _moe_block(h1, logits[:, :N_EXPERTS], moe_w, g2, b2, 512, 512)
        hp = h2[:m_p].reshape(nb, seq, D_MODEL)
        hs = h2[m_p:m_all]

    st = lambda k: jnp.stack(outs[k], axis=1)
    return (hp, hs.reshape(n_dec, n_new, D_MODEL),
            st('fox_p'), st('fox_s'), st('nsa_p'), st('nsa_s'),
            st('diff_p'), st('diff_s'), st('win_p'), st('win_s'))
```

```python
import collections
import functools
import math

import numpy as np
import jax
import jax.numpy as jnp
from jax import lax
from jax.experimental import pallas as pl
from jax.experimental.pallas import tpu as pltpu

F32 = jnp.float32
BF16 = jnp.bfloat16

D_MODEL = 1024
DEPTH = 2
PAGE_SIZE = 128
HEAD_DIM = 64
FOX_HEADS = 4
NSA_HEADS = 8
NSA_KV_GROUPS = 2
NSA_HPG = NSA_HEADS // NSA_KV_GROUPS
DIFF_HEADS = 4
DIFF_QK_DIM = HEAD_DIM // 2
NSA_BLOCK = 64
BLOCK_SHIFT = 6
NSA_TOP_N = 16
NSA_WINDOW = 512
ROPE_THETA = 500000.0
N_EXPERTS = 32
TOP_K = 4
SWIGLU_LIMIT = 7.0
SWIGLU_ALPHA = 1.702
LN_EPS = 1e-5
RMS_EPS = 1e-5
DEEPNORM_ALPHA = (2 * DEPTH) ** 0.25
NEG = -1e30
FORCE_SCORE = 1e4

SUBLANES = 8
LANES = 128
VMEM_LIMIT_BYTES = 48 * 1024 * 1024
SAMPLE_PAGES_PER_STEP = 16
NSA_SEL_TK = 1024

C_FQ, C_FK, C_FV, C_FF = 0, 256, 512, 768
C_NQ, C_NKC, C_NKW, C_NVW, C_NG = 772, 1284, 1796, 1924, 2052
C_DQ, C_DK, C_DV = 2076, 2332, 2588
N_IN = 2844

R_NQ = 0
R_NSA = 512
R_DIFF = 1024
R_FQ = 1536
R_FK = 1792
R_FV = 2048
R_WIN = 2304
R_DQ = 2560
R_GATE = 2816
R_FOX = 2848
R_LOGF = R_FOX + 512
NP_ROWS = 3368


def _build_perm():
    perm = np.full((NP_ROWS,), -1, np.int64)
    perm[R_NQ:R_NQ + 512] = C_NQ + np.arange(512)
    perm[R_NSA:R_NSA + 512] = C_NKC + np.arange(512)
    for h in range(DIFF_HEADS):
        perm[R_DIFF + h * 128:R_DIFF + h * 128 + 64] = C_DK + h * 64 + np.arange(64)
        perm[R_DIFF + h * 128 + 64:R_DIFF + h * 128 + 128] = C_DV + h * 64 + np.arange(64)
    perm[R_FQ:R_FQ + 256] = C_FQ + np.arange(256)
    perm[R_FK:R_FK + 256] = C_FK + np.arange(256)
    perm[R_FV:R_FV + 256] = C_FV + np.arange(256)
    perm[R_WIN:R_WIN + 256] = C_NKW + np.arange(256)
    perm[R_DQ:R_DQ + 256] = C_DQ + np.arange(256)
    perm[R_GATE:R_GATE + 24] = C_NG + np.arange(24)
    for c in range(64):
        for h in range(FOX_HEADS):
            perm[R_FOX + c * 4 + h] = C_FK + h * 64 + c
            perm[R_FOX + (64 + c) * 4 + h] = C_FV + h * 64 + c
    for h in range(FOX_HEADS):
        perm[R_FOX + 512 + h] = C_FF + h
    return perm


_PERM = _build_perm()
_ROT64_ROWS = tuple([R_NQ + h * 64 for h in range(NSA_HEADS)]
                    + [R_NSA + g * 64 for g in range(NSA_KV_GROUPS)]
                    + [R_NSA + 256 + g * 64 for g in range(NSA_KV_GROUPS)]
                    + [R_WIN + g * 64 for g in range(NSA_KV_GROUPS)])
_ROT32_ROWS = tuple([R_DQ + i * 32 for i in range(2 * DIFF_HEADS)]
                    + [R_DIFF + h * 128 + m * 32 for h in range(DIFF_HEADS) for m in range(2)])


def _cparams(sem):
    return pltpu.CompilerParams(dimension_semantics=sem, vmem_limit_bytes=VMEM_LIMIT_BYTES)


def _inproj_kernel(x_ref, w_ref, wg_ref, bgate_ref, bforget_ref, bgrow_ref, cn_ref, sn_ref, cd_ref, sd_ref,
                   o_ref, g_ref):
    x = x_ref[...].astype(BF16)
    o_ref[...] = lax.dot_general(w_ref[...], x, (((1,), (1,)), ((), ())), preferred_element_type=F32)
    cn = cn_ref[...]
    sn = sn_ref[...]
    for r in _ROT64_ROWS:
        x1 = o_ref[r:r + 8, :]
        x2 = o_ref[r + 8:r + 16, :]
        o_ref[r:r + 8, :] = x1 * cn - x2 * sn
        o_ref[r + 8:r + 16, :] = x2 * cn + x1 * sn
    cd = cd_ref[...]
    sd = sd_ref[...]
    for r in _ROT32_ROWS:
        v = o_ref[r:r + 8, :]
        o_ref[r:r + 8, :] = v * cd + pltpu.roll(v, 4, 0) * sd
    z = o_ref[R_GATE:R_GATE + 32, :] + bgate_ref[...]
    o_ref[R_GATE:R_GATE + 32, :] = 1.0 / (1.0 + jnp.exp(-z))
    z = o_ref[R_LOGF:R_LOGF + 8, :] + bforget_ref[...]
    o_ref[R_LOGF:R_LOGF + 8, :] = jnp.minimum(z, 0.0) - jnp.log1p(jnp.exp(-jnp.abs(z)))
    zg = jnp.dot(x, wg_ref[...], preferred_element_type=F32) + bgrow_ref[...]
    g_ref[...] = 1.0 / (1.0 + jnp.exp(-zg))


def _inproj(x, wt, wg, bgate_col, bforget_col, bgate_row, tabs, tn):
    nb, nt, k = x.shape
    cn, sn, cd, sd = tabs
    const = lambda b, i: (0, 0)
    tab = lambda b, i: (0, i)
    return pl.pallas_call(
        _inproj_kernel,
        grid=(nb, nt // tn),
        in_specs=[
            pl.BlockSpec((None, tn, k), lambda b, i: (b, i, 0)),
            pl.BlockSpec((NP_ROWS, k), const),
            pl.BlockSpec((k, LANES), const),
            pl.BlockSpec((32, 1), const),
            pl.BlockSpec((8, 1), const),
            pl.BlockSpec((1, LANES), const),
            pl.BlockSpec((8, tn), tab),
            pl.BlockSpec((8, tn), tab),
            pl.BlockSpec((8, tn), tab),
            pl.BlockSpec((8, tn), tab),
        ],
        out_specs=[
            pl.BlockSpec((None, NP_ROWS, tn), lambda b, i: (b, 0, i)),
            pl.BlockSpec((None, tn, LANES), lambda b, i: (b, i, 0)),
        ],
        out_shape=[jax.ShapeDtypeStruct((nb, NP_ROWS, nt), F32),
                   jax.ShapeDtypeStruct((nb, nt, LANES), F32)],
        compiler_params=_cparams(("parallel", "parallel")),
        name="inproj",
    )(x, wt, wg, bgate_col, bforget_col, bgate_row, cn, sn, cd, sd)


def _rope_tables(pos):
    inv8 = ROPE_THETA ** (-jnp.arange(8, dtype=F32) / 8)
    ang8 = pos[:, None] * inv8[None, :]
    cn, sn = jnp.cos(ang8).T, jnp.sin(ang8).T
    inv4 = ROPE_THETA ** (-jnp.arange(4, dtype=F32) / 4)
    ang4 = pos[:, None] * inv4[None, :]
    c4, s4 = jnp.cos(ang4).T, jnp.sin(ang4).T
    cd = jnp.concatenate([c4, c4], 0)
    sd = jnp.concatenate([-s4, s4], 0)
    return cn, sn, cd, sd


def _tri_pairs(nq, lo_tiles=None):
    qi, kj = [], []
    for i in range(nq):
        j0 = 0 if lo_tiles is None else max(0, i - lo_tiles)
        for j in range(j0, i + 1):
            qi.append(i)
            kj.append(j)
    first = [1 if (p == 0 or qi[p] != qi[p - 1]) else 0 for p in range(len(qi))]
    return (jnp.asarray(qi, jnp.int32), jnp.asarray(kj, jnp.int32), jnp.asarray(first, jnp.int32))


def _online_update(s, v_t, m_ref, l_ref, acc_ref):
    m_prev = m_ref[:, 0:1]
    m_new = jnp.maximum(m_prev, jnp.max(s, axis=1, keepdims=True))
    alpha = jnp.exp(m_prev - m_new)
    p = jnp.exp(s - m_new)
    l_ref[...] = jnp.broadcast_to(alpha * l_ref[:, 0:1] + jnp.sum(p, axis=1, keepdims=True), l_ref.shape)
    m_ref[...] = jnp.broadcast_to(m_new, m_ref.shape)
    pv = lax.dot_general(p.astype(BF16), v_t, (((1,), (1,)), ((), ())), preferred_element_type=F32)
    acc_ref[...] = alpha * acc_ref[...] + pv


def _fox_kernel(qi_ref, kj_ref, first_ref, q_ref, k_ref, v_ref, c_ref, o_ref, q_s, m_s, l_s, acc_s, *, tq, tk):
    p_id = pl.program_id(1)
    qi = qi_ref[p_id]
    kj = kj_ref[p_id]
    scale = HEAD_DIM ** -0.5

    @pl.when(first_ref[p_id] == 1)
    def _():
        for h in range(FOX_HEADS):
            q_s[h] = q_ref[h * 64:(h + 1) * 64, :].T.astype(BF16)
        m_s[...] = jnp.full(m_s.shape, NEG, F32)
        l_s[...] = jnp.zeros(l_s.shape, F32)
        acc_s[...] = jnp.zeros(acc_s.shape, F32)

    row = qi * tq + lax.broadcasted_iota(jnp.int32, (tq, tk), 0)
    col = kj * tk + lax.broadcasted_iota(jnp.int32, (tq, tk), 1)
    mask = col <= row
    for h in range(FOX_HEADS):
        k_t = k_ref[h * 64:(h + 1) * 64, :].astype(BF16)
        v_t = v_ref[h * 64:(h + 1) * 64, :].astype(BF16)
        s = jnp.dot(q_s[h], k_t, preferred_element_type=F32) * scale - c_ref[h:h + 1, :]
        s = jnp.where(mask, s, NEG)
        _online_update(s, v_t, m_s.at[h], l_s.at[h], acc_s.at[h])

    @pl.when(kj == qi)
    def _():
        for h in range(FOX_HEADS):
            o_ref[:, h * 64:(h + 1) * 64] = acc_s[h] / l_s[h][:, 0:1]


def _fox_attention(pt, c, tq):
    nb, _, nt = pt.shape
    tk = tq
    qi, kj, first = _tri_pairs(nt // tq)
    blk = 256
    grid_spec = pltpu.PrefetchScalarGridSpec(
        num_scalar_prefetch=3,
        grid=(nb, qi.shape[0]),
        in_specs=[
            pl.BlockSpec((None, blk, tq), lambda b, p, qi, kj, f: (b, R_FQ // blk, qi[p])),
            pl.BlockSpec((None, blk, tk), lambda b, p, qi, kj, f: (b, R_FK // blk, kj[p])),
            pl.BlockSpec((None, blk, tk), lambda b, p, qi, kj, f: (b, R_FV // blk, kj[p])),
            pl.BlockSpec((None, FOX_HEADS, tk), lambda b, p, qi, kj, f: (b, 0, kj[p])),
        ],
        out_specs=pl.BlockSpec((None, tq, blk), lambda b, p, qi, kj, f: (b, qi[p], 0)),
        scratch_shapes=[
            pltpu.VMEM((FOX_HEADS, tq, HEAD_DIM), BF16),
            pltpu.VMEM((FOX_HEADS, tq, LANES), F32),
            pltpu.VMEM((FOX_HEADS, tq, LANES), F32),
            pltpu.VMEM((FOX_HEADS, tq, HEAD_DIM), F32),
        ],
    )
    return pl.pallas_call(
        functools.partial(_fox_kernel, tq=tq, tk=tk),
        grid_spec=grid_spec,
        out_shape=jax.ShapeDtypeStruct((nb, nt, blk), F32),
        compiler_params=_cparams(("parallel", "arbitrary")),
        name="fox_attn",
    )(qi, kj, first, pt, pt, pt, c)


def _diff_kernel(qi_ref, kj_ref, first_ref, lam_ref, q_ref, kv_ref, g_ref, o_ref, q_s, m_s, l_s, acc_s,
                 *, tq, tk, out_scale):
    p_id = pl.program_id(1)
    qi = qi_ref[p_id]
    kj = kj_ref[p_id]
    scale = DIFF_QK_DIM ** -0.5

    @pl.when(first_ref[p_id] == 1)
    def _():
        for i in range(2 * DIFF_HEADS):
            q_s[i] = q_ref[i * 32:(i + 1) * 32, :].T.astype(BF16)
        m_s[...] = jnp.full(m_s.shape, NEG, F32)
        l_s[...] = jnp.zeros(l_s.shape, F32)
        acc_s[...] = jnp.zeros(acc_s.shape, F32)

    row = qi * tq + lax.broadcasted_iota(jnp.int32, (tq, tk), 0)
    col = kj * tk + lax.broadcasted_iota(jnp.int32, (tq, tk), 1)
    mask = col <= row
    for h in range(DIFF_HEADS):
        v_t = kv_ref[h * 128 + 64:h * 128 + 128, :].astype(BF16)
        for m in range(2):
            i = 2 * h + m
            k_t = kv_ref[h * 128 + m * 32:h * 128 + (m + 1) * 32, :].astype(BF16)
            s = jnp.dot(q_s[i], k_t, preferred_element_type=F32) * scale
            s = jnp.where(mask, s, NEG)
            _online_update(s, v_t, m_s.at[i], l_s.at[i], acc_s.at[i])

    @pl.when(kj == qi)
    def _():
        lam = lam_ref[0]
        for h in range(DIFF_HEADS):
            o1 = acc_s[2 * h] / l_s[2 * h][:, 0:1]
            o2 = acc_s[2 * h + 1] / l_s[2 * h + 1][:, 0:1]
            o = o1 - lam * o2
            o = o * lax.rsqrt(jnp.mean(o * o, axis=-1, keepdims=True) + RMS_EPS) * g_ref[...] * out_scale
            o_ref[:, h * 64:(h + 1) * 64] = o


def _diff_attention(pt, lam, g_row, out_scale, tq):
    nb, _, nt = pt.shape
    tk = tq
    qi, kj, first = _tri_pairs(nt // tq)
    grid_spec = pltpu.PrefetchScalarGridSpec(
        num_scalar_prefetch=4,
        grid=(nb, qi.shape[0]),
        in_specs=[
            pl.BlockSpec((None, 256, tq), lambda b, p, qi, kj, f, lam: (b, R_DQ // 256, qi[p])),
            pl.BlockSpec((None, 512, tk), lambda b, p, qi, kj, f, lam: (b, R_DIFF // 512, kj[p])),
            pl.BlockSpec((1, HEAD_DIM), lambda b, p, qi, kj, f, lam: (0, 0)),
        ],
        out_specs=pl.BlockSpec((None, tq, 256), lambda b, p, qi, kj, f, lam: (b, qi[p], 0)),
        scratch_shapes=[
            pltpu.VMEM((2 * DIFF_HEADS, tq, DIFF_QK_DIM), BF16),
            pltpu.VMEM((2 * DIFF_HEADS, tq, LANES), F32),
            pltpu.VMEM((2 * DIFF_HEADS, tq, LANES), F32),
            pltpu.VMEM((2 * DIFF_HEADS, tq, HEAD_DIM), F32),
        ],
    )
    return pl.pallas_call(
        functools.partial(_diff_kernel, tq=tq, tk=tk, out_scale=out_scale),
        grid_spec=grid_spec,
        out_shape=jax.ShapeDtypeStruct((nb, nt, 256), F32),
        compiler_params=_cparams(("parallel", "arbitrary")),
        name="diff_attn",
    )(qi, kj, first, lam, pt, pt, g_row)


def _compress_kernel(x_ref, pe_ref, w_ref, o_ref):
    x = (x_ref[...] + pe_ref[...]).astype(BF16)
    o_ref[...] = jnp.dot(x, w_ref[...].astype(BF16), preferred_element_type=F32)


def _compress(xb, pe, w, tm):
    _, m, kk = xb.shape
    return pl.pallas_call(
        _compress_kernel,
        grid=(2, m // tm),
        in_specs=[
            pl.BlockSpec((None, tm, kk), lambda t, i: (t, i, 0)),
            pl.BlockSpec((None, 1, kk), lambda t, i: (t, 0, 0)),
            pl.BlockSpec((None, kk, HEAD_DIM), lambda t, i: (t, 0, 0)),
        ],
        out_specs=pl.BlockSpec((None, tm, HEAD_DIM), lambda t, i: (t, i, 0)),
        out_shape=jax.ShapeDtypeStruct((2, m, HEAD_DIM), F32),
        compiler_params=_cparams(("parallel", "parallel")),
        name="nsa_compress",
    )(xb, pe, w)


def _select_mask(imp, qpos, n_blk):
    jb = lax.broadcasted_iota(jnp.int32, imp.shape, 1)
    cur = qpos >> BLOCK_SHIFT
    forced = (jb == 0) | (jb == cur) | (jb == cur - 1)
    score = jnp.where(jb <= cur, jnp.where(forced, FORCE_SCORE, imp), NEG)
    rank = jnp.zeros(imp.shape, F32)
    for j in range(n_blk):
        col = score[:, j:j + 1]
        ahead = (col > score) | ((col == score) & (jb > j))
        rank = rank + jnp.where(ahead, 1.0, 0.0)
    return jnp.where(rank < float(NSA_TOP_N), 1.0, 0.0)


def _nsa_cmp_kernel(q_ref, kc_ref, vc_ref, gate_ref, o_ref, sel_ref, *, tq, n_cmp):
    qi = pl.program_id(2)
    g = pl.program_id(1)
    scale = HEAD_DIM ** -0.5
    kc = kc_ref[...].astype(BF16)
    vc = vc_ref[...].astype(BF16)
    qpos = qi * tq + lax.broadcasted_iota(jnp.int32, (tq, 1), 0)
    nb_iota = lax.broadcasted_iota(jnp.int32, (tq, n_cmp), 1)
    mask = nb_iota < ((qpos + 1) >> BLOCK_SHIFT)
    maskf = jnp.where(mask, 1.0, 0.0)
    imp = jnp.zeros((tq, n_cmp), F32)
    gates = gate_ref[...]
    for h in range(NSA_HPG):
        qh = q_ref[h * 64:(h + 1) * 64, :].T.astype(BF16)
        lc = lax.dot_general(qh, kc, (((1,), (1,)), ((), ())), preferred_element_type=F32) * scale
        lc = jnp.where(mask, lc, NEG)
        m = jnp.max(lc, axis=1, keepdims=True)
        p = jnp.exp(lc - m) * maskf
        pc = p / jnp.maximum(jnp.sum(p, axis=1, keepdims=True), 1e-30)
        imp = imp + pc
        oc = jnp.dot(pc.astype(BF16), vc, preferred_element_type=F32)
        gcol = jnp.where(g == 0, gates[:, h:h + 1], gates[:, NSA_HPG + h:NSA_HPG + h + 1])
        o_ref[:, h * 64:(h + 1) * 64] = oc * gcol
    sel_ref[...] = _select_mask(imp, qpos, n_cmp)


def _nsa_cmp(pt, gates, kc, vc, tq):
    nb, _, nt = pt.shape
    n_cmp = kc.shape[2]
    return pl.pallas_call(
        functools.partial(_nsa_cmp_kernel, tq=tq, n_cmp=n_cmp),
        grid=(nb, NSA_KV_GROUPS, nt // tq),
        in_specs=[
            pl.BlockSpec((None, 256, tq), lambda b, g, i: (b, R_NQ // 256 + g, i)),
            pl.BlockSpec((None, None, n_cmp, HEAD_DIM), lambda b, g, i: (b, g, 0, 0)),
            pl.BlockSpec((None, None, n_cmp, HEAD_DIM), lambda b, g, i: (b, g, 0, 0)),
            pl.BlockSpec((None, tq, LANES), lambda b, g, i: (b, i, 0)),
        ],
        out_specs=[
            pl.BlockSpec((None, tq, 256), lambda b, g, i: (b, i, g)),
            pl.BlockSpec((None, None, tq, n_cmp), lambda b, g, i: (b, g, i, 0)),
        ],
        out_shape=[jax.ShapeDtypeStruct((nb, nt, 512), F32),
                   jax.ShapeDtypeStruct((nb, NSA_KV_GROUPS, nt, n_cmp), F32)],
        compiler_params=_cparams(("parallel", "parallel", "parallel")),
        name="nsa_cmp",
    )(pt, kc, vc, gates)


def _nsa_band_kernel(qi_ref, kj_ref, first_ref, last_ref, q_ref, k_ref, v_ref, gate_ref, *rest,
                     tq, tk, selected, gate_base):
    if selected:
        sel_ref, o_ref, q_s, m_s, l_s, acc_s = rest
    else:
        o_ref, q_s, m_s, l_s, acc_s = rest
    p_id = pl.program_id(2)
    g = pl.program_id(1)
    qi = qi_ref[p_id]
    kj = kj_ref[p_id]
    scale = HEAD_DIM ** -0.5

    @pl.when(first_ref[p_id] == 1)
    def _():
        for h in range(NSA_HPG):
            q_s[h * tq:(h + 1) * tq, :] = q_ref[h * 64:(h + 1) * 64, :].T.astype(BF16)
        m_s[...] = jnp.full(m_s.shape, NEG, F32)
        l_s[...] = jnp.zeros(l_s.shape, F32)
        acc_s[...] = jnp.zeros(acc_s.shape, F32)

    row = qi * tq + lax.broadcasted_iota(jnp.int32, (tq, tk), 0)
    col = kj * tk + lax.broadcasted_iota(jnp.int32, (tq, tk), 1)
    mask = col <= row
    if selected:
        n_blk = sel_ref.shape[-1]
        blk_of_col = kj * (tk // NSA_BLOCK) + (lax.broadcasted_iota(jnp.int32, (n_blk, tk), 1) >> BLOCK_SHIFT)
        expand = jnp.where(lax.broadcasted_iota(jnp.int32, (n_blk, tk), 0) == blk_of_col, 1.0, 0.0).astype(BF16)
        picked = jnp.dot(sel_ref[...].astype(BF16), expand, preferred_element_type=F32)
        mask = mask & (picked > 0.5)
    else:
        mask = mask & (col > row - NSA_WINDOW)
    keep = jnp.where(mask, 1.0, 0.0)
    k_t = k_ref[...].astype(BF16)
    v_t = v_ref[...].astype(BF16)
    s = jnp.dot(q_s[...], k_t, preferred_element_type=F32) * scale
    s = jnp.where(jnp.concatenate([keep] * NSA_HPG, axis=0) > 0.5, s, NEG)
    _online_update(s, v_t, m_s, l_s, acc_s)

    @pl.when(last_ref[p_id] == 1)
    def _():
        gates = gate_ref[...]
        for h in range(NSA_HPG):
            c0 = gate_base + h
            gcol = jnp.where(g == 0, gates[:, c0:c0 + 1], gates[:, c0 + NSA_HPG:c0 + NSA_HPG + 1])
            o_ref[:, h * 64:(h + 1) * 64] = acc_s[h * tq:(h + 1) * tq, :] / l_s[h * tq:(h + 1) * tq, 0:1] * gcol


def _nsa_band(pt, gates, sel, tq, selected):
    nb, _, nt = pt.shape
    tk = tq
    nq = nt // tq
    qi, kj, first = _tri_pairs(nq, None if selected else -(-NSA_WINDOW // tk))
    last = jnp.concatenate([first[1:], jnp.ones((1,), jnp.int32)])
    if selected:
        rk, rv, gate_base = R_NSA + 256, R_NSA + 384, 8
    else:
        rk, rv, gate_base = R_WIN, R_WIN + 128, 16
    idx = lambda f: (lambda b, g, p, qi, kj, fi, la: f(b, g, p, qi, kj))
    in_specs = [
        pl.BlockSpec((None, 256, tq), idx(lambda b, g, p, qi, kj: (b, R_NQ // 256 + g, qi[p]))),
        pl.BlockSpec((None, HEAD_DIM, tk), idx(lambda b, g, p, qi, kj: (b, rk // 64 + g, kj[p]))),
        pl.BlockSpec((None, HEAD_DIM, tk), idx(lambda b, g, p, qi, kj: (b, rv // 64 + g, kj[p]))),
        pl.BlockSpec((None, tq, LANES), idx(lambda b, g, p, qi, kj: (b, qi[p], 0))),
    ]
    args = [pt, pt, pt, gates]
    if selected:
        in_specs.append(pl.BlockSpec((None, None, tq, sel.shape[-1]),
                                     idx(lambda b, g, p, qi, kj: (b, g, qi[p], 0))))
        args.append(sel)
    grid_spec = pltpu.PrefetchScalarGridSpec(
        num_scalar_prefetch=4,
        grid=(nb, NSA_KV_GROUPS, qi.shape[0]),
        in_specs=in_specs,
        out_specs=pl.BlockSpec((None, tq, 256), idx(lambda b, g, p, qi, kj: (b, qi[p], g))),
        scratch_shapes=[
            pltpu.VMEM((NSA_HPG * tq, HEAD_DIM), BF16),
            pltpu.VMEM((NSA_HPG * tq, LANES), F32),
            pltpu.VMEM((NSA_HPG * tq, LANES), F32),
            pltpu.VMEM((NSA_HPG * tq, HEAD_DIM), F32),
        ],
    )
    return pl.pallas_call(
        functools.partial(_nsa_band_kernel, tq=tq, tk=tk, selected=selected, gate_base=gate_base),
        grid_spec=grid_spec,
        out_shape=jax.ShapeDtypeStruct((nb, nt, 512), F32),
        compiler_params=_cparams(("parallel", "parallel", "arbitrary")),
        name="nsa_sel" if selected else "nsa_win",
    )(qi, kj, first, last, *args)


def _flash_step(s, v_t, m_ref, l_ref, acc_ref):
    m_prev = m_ref[...]
    m_new = jnp.maximum(m_prev, jnp.max(s, axis=0, keepdims=True))
    alpha = jnp.exp(m_prev - m_new)
    p = jnp.exp(s - m_new)
    l_ref[...] = alpha * l_ref[...] + jnp.sum(p, axis=0, keepdims=True)
    m_ref[...] = m_new
    acc_ref[...] = alpha * acc_ref[...] + jnp.dot(v_t, p.astype(BF16), preferred_element_type=F32)


def _kq(k_t, q_t):
    return lax.dot_general(k_t, q_t, (((0,), (0,)), ((), ())), preferred_element_type=F32)


def _causal_keep(qi, kj, tq, tk):
    key = kj * tk + lax.broadcasted_iota(jnp.int32, (tk, tq), 0)
    qry = qi * tq + lax.broadcasted_iota(jnp.int32, (tk, tq), 1)
    return key, qry


def _split3(x):
    x1 = x.astype(BF16)
    r1 = x - x1.astype(F32)
    x2 = r1.astype(BF16)
    x3 = (r1 - x2.astype(F32)).astype(BF16)
    return x1, x2, x3


def _fox_kernel_t(qi_ref, kj_ref, first_ref, q_ref, k_ref, v_ref, c_ref, o_ref, q_s, m_s, l_s, acc_s, *, tq, tk):
    p_id = pl.program_id(1)
    qi = qi_ref[p_id]
    kj = kj_ref[p_id]

    @pl.when(first_ref[p_id] == 1)
    def _():
        extra = jnp.where(lax.broadcasted_iota(jnp.int32, (16, tq), 0) < 3, -1.0, 0.0).astype(BF16)
        for h in range(FOX_HEADS):
            q_s[h, 0:64, :] = (q_ref[h * 64:(h + 1) * 64, :] * (HEAD_DIM ** -0.5)).astype(BF16)
            q_s[h, 64:80, :] = extra
        m_s[...] = jnp.full(m_s.shape, NEG, F32)
        l_s[...] = jnp.zeros(l_s.shape, F32)
        acc_s[...] = jnp.zeros(acc_s.shape, F32)

    def tile(masked):
        sub16 = lax.broadcasted_iota(jnp.int32, (16, tk), 0)
        if masked:
            key, qry = _causal_keep(qi, kj, tq, tk)
            mask = key <= qry
        for h in range(FOX_HEADS):
            c1, c2, c3 = [jnp.broadcast_to(t.astype(F32), (16, tk)) for t in _split3(c_ref[h:h + 1, :])]
            extra = jnp.where(sub16 == 0, c1, jnp.where(sub16 == 1, c2, jnp.where(sub16 == 2, c3, 0.0)))
            k_aug = jnp.concatenate([k_ref[h * 64:(h + 1) * 64, :].astype(BF16), extra.astype(BF16)], axis=0)
            s = _kq(k_aug, q_s[h])
            if masked:
                s = jnp.where(mask, s, NEG)
            _flash_step(s, v_ref[h * 64:(h + 1) * 64, :].astype(BF16), m_s.at[h], l_s.at[h], acc_s.at[h])

    pl.when(kj == qi)(lambda: tile(True))
    pl.when(kj != qi)(lambda: tile(False))

    @pl.when(kj == qi)
    def _():
        for h in range(FOX_HEADS):
            o_ref[:, h * 64:(h + 1) * 64] = (acc_s[h] / l_s[h]).T


def _fox_attention_t(pt, c, tq):
    nb, _, nt = pt.shape
    tk = tq
    qi, kj, first = _tri_pairs(nt // tq)
    blk = 256
    grid_spec = pltpu.PrefetchScalarGridSpec(
        num_scalar_prefetch=3,
        grid=(nb, qi.shape[0]),
        in_specs=[
            pl.BlockSpec((None, blk, tq), lambda b, p, qi, kj, f: (b, R_FQ // blk, qi[p])),
            pl.BlockSpec((None, blk, tk), lambda b, p, qi, kj, f: (b, R_FK // blk, kj[p])),
            pl.BlockSpec((None, blk, tk), lambda b, p, qi, kj, f: (b, R_FV // blk, kj[p])),
            pl.BlockSpec((None, FOX_HEADS, tk), lambda b, p, qi, kj, f: (b, 0, kj[p])),
        ],
        out_specs=pl.BlockSpec((None, tq, blk), lambda b, p, qi, kj, f: (b, qi[p], 0)),
        scratch_shapes=[
            pltpu.VMEM((FOX_HEADS, HEAD_DIM + 16, tq), BF16),
            pltpu.VMEM((FOX_HEADS, 1, tq), F32),
            pltpu.VMEM((FOX_HEADS, 1, tq), F32),
            pltpu.VMEM((FOX_HEADS, HEAD_DIM, tq), F32),
        ],
    )
    return pl.pallas_call(
        functools.partial(_fox_kernel_t, tq=tq, tk=tk),
        grid_spec=grid_spec,
        out_shape=jax.ShapeDtypeStruct((nb, nt, blk), F32),
        compiler_params=_cparams(("parallel", "arbitrary")),
        name="fox_attn",
    )(qi, kj, first, pt, pt, pt, c)


def _diff_kernel_t(qi_ref, kj_ref, first_ref, lam_ref, q_ref, kv_ref, g_ref, o_ref, q_s, m_s, l_s, acc_s,
                   *, tq, tk, out_scale):
    p_id = pl.program_id(1)
    qi = qi_ref[p_id]
    kj = kj_ref[p_id]
    scale = DIFF_QK_DIM ** -0.5

    @pl.when(first_ref[p_id] == 1)
    def _():
        q_s[...] = q_ref[...].astype(BF16)
        m_s[...] = jnp.full(m_s.shape, NEG, F32)
        l_s[...] = jnp.zeros(l_s.shape, F32)
        acc_s[...] = jnp.zeros(acc_s.shape, F32)

    def tile(masked):
        if masked:
            key, qry = _causal_keep(qi, kj, tq, tk)
            mask = key <= qry
        for h in range(DIFF_HEADS):
            v_t = kv_ref[h * 128 + 64:h * 128 + 128, :].astype(BF16)
            for m in range(2):
                i = 2 * h + m
                k_t = kv_ref[h * 128 + m * 32:h * 128 + (m + 1) * 32, :].astype(BF16)
                s = _kq(k_t, q_s[i * 32:(i + 1) * 32, :]) * scale
                if masked:
                    s = jnp.where(mask, s, NEG)
                _flash_step(s, v_t, m_s.at[i], l_s.at[i], acc_s.at[i])

    pl.when(kj == qi)(lambda: tile(True))
    pl.when(kj != qi)(lambda: tile(False))

    @pl.when(kj == qi)
    def _():
        lam = lam_ref[0]
        for h in range(DIFF_HEADS):
            o = acc_s[2 * h] / l_s[2 * h] - lam * (acc_s[2 * h + 1] / l_s[2 * h + 1])
            o = o * lax.rsqrt(jnp.mean(o * o, axis=0, keepdims=True) + RMS_EPS) * g_ref[...] * out_scale
            o_ref[:, h * 64:(h + 1) * 64] = o.T


def _diff_attention_t(pt, lam, g_col, out_scale, tq):
    nb, _, nt = pt.shape
    tk = tq
    qi, kj, first = _tri_pairs(nt // tq)
    grid_spec = pltpu.PrefetchScalarGridSpec(
        num_scalar_prefetch=4,
        grid=(nb, qi.shape[0]),
        in_specs=[
            pl.BlockSpec((None, 256, tq), lambda b, p, qi, kj, f, lam: (b, R_DQ // 256, qi[p])),
            pl.BlockSpec((None, 512, tk), lambda b, p, qi, kj, f, lam: (b, R_DIFF // 512, kj[p])),
            pl.BlockSpec((HEAD_DIM, 1), lambda b, p, qi, kj, f, lam: (0, 0)),
        ],
        out_specs=pl.BlockSpec((None, tq, 256), lambda b, p, qi, kj, f, lam: (b, qi[p], 0)),
        scratch_shapes=[
            pltpu.VMEM((256, tq), BF16),
            pltpu.VMEM((2 * DIFF_HEADS, 1, tq), F32),
            pltpu.VMEM((2 * DIFF_HEADS, 1, tq), F32),
            pltpu.VMEM((2 * DIFF_HEADS, HEAD_DIM, tq), F32),
        ],
    )
    return pl.pallas_call(
        functools.partial(_diff_kernel_t, tq=tq, tk=tk, out_scale=out_scale),
        grid_spec=grid_spec,
        out_shape=jax.ShapeDtypeStruct((nb, nt, 256), F32),
        compiler_params=_cparams(("parallel", "arbitrary")),
        name="diff_attn",
    )(qi, kj, first, lam, pt, pt, g_col)


def _stack_heads(q_ref, tq):
    return jnp.concatenate([q_ref[h * 64:(h + 1) * 64, :] for h in range(NSA_HPG)], axis=1)


def _select_mask_t(imp, qpos, n_blk):
    jb = lax.broadcasted_iota(jnp.int32, imp.shape, 0)
    cur = qpos >> BLOCK_SHIFT
    forced = (jb == 0) | (jb == cur) | (jb == cur - 1)
    score = jnp.where(jb <= cur, jnp.where(forced, FORCE_SCORE, imp), NEG)
    rank = jnp.zeros(imp.shape, F32)
    for j in range(n_blk):
        row = score[j:j + 1, :]
        ahead = (row > score) | ((row == score) & (jb > j))
        rank = rank + jnp.where(ahead, 1.0, 0.0)
    return jnp.where(rank < float(NSA_TOP_N), 1.0, 0.0)


def _nsa_cmp_kernel_t(q_ref, kc_ref, vct_ref, gate_ref, o_ref, sel_ref, *, tq, n_cmp):
    qi = pl.program_id(2)
    g = pl.program_id(1)
    q_t = (_stack_heads(q_ref, tq) * (HEAD_DIM ** -0.5)).astype(BF16)
    lc = jnp.dot(kc_ref[...].astype(BF16), q_t, preferred_element_type=F32)
    qpos = qi * tq + lax.broadcasted_iota(jnp.int32, (1, tq), 1)
    blk = lax.broadcasted_iota(jnp.int32, (n_cmp, tq), 0)
    keep = jnp.where(blk < ((qpos + 1) >> BLOCK_SHIFT), 1.0, 0.0)
    keep4 = jnp.concatenate([keep] * NSA_HPG, axis=1)
    lc = jnp.where(keep4 > 0.5, lc, NEG)
    p = jnp.exp(lc - jnp.max(lc, axis=0, keepdims=True)) * keep4
    pc = p / jnp.maximum(jnp.sum(p, axis=0, keepdims=True), 1e-30)
    oc = jnp.dot(vct_ref[...].astype(BF16), pc.astype(BF16), preferred_element_type=F32)
    imp = pc[:, 0:tq]
    for h in range(1, NSA_HPG):
        imp = imp + pc[:, h * tq:(h + 1) * tq]
    sel_ref[...] = _select_mask_t(imp, qpos, n_cmp)
    gates = gate_ref[...]
    for h in range(NSA_HPG):
        grow = jnp.where(g == 0, gates[h:h + 1, :], gates[NSA_HPG + h:NSA_HPG + h + 1, :])
        o_ref[:, h * 64:(h + 1) * 64] = (oc[:, h * tq:(h + 1) * tq] * grow).T


def _nsa_cmp_t(pt, kc, vct, tq):
    nb, _, nt = pt.shape
    n_cmp = kc.shape[2]
    return pl.pallas_call(
        functools.partial(_nsa_cmp_kernel_t, tq=tq, n_cmp=n_cmp),
        grid=(nb, NSA_KV_GROUPS, nt // tq),
        in_specs=[
            pl.BlockSpec((None, 256, tq), lambda b, g, i: (b, R_NQ // 256 + g, i)),
            pl.BlockSpec((None, None, n_cmp, HEAD_DIM), lambda b, g, i: (b, g, 0, 0)),
            pl.BlockSpec((None, None, HEAD_DIM, n_cmp), lambda b, g, i: (b, g, 0, 0)),
            pl.BlockSpec((None, 32, tq), lambda b, g, i: (b, R_GATE // 32, i)),
        ],
        out_specs=[
            pl.BlockSpec((None, tq, 256), lambda b, g, i: (b, i, g)),
            pl.BlockSpec((None, None, n_cmp, tq), lambda b, g, i: (b, g, 0, i)),
        ],
        out_shape=[jax.ShapeDtypeStruct((nb, nt, 512), F32),
                   jax.ShapeDtypeStruct((nb, NSA_KV_GROUPS, n_cmp, nt), F32)],
        compiler_params=_cparams(("parallel", "parallel", "parallel")),
        name="nsa_cmp",
    )(pt, kc, vct, pt)


def _nsa_band_kernel_t(qi_ref, kj_ref, first_ref, last_ref, q_ref, k_ref, v_ref, gate_ref, *rest,
                       tq, tk, selected, gate_base):
    if selected:
        sel_ref, o_ref, q_s, m_s, l_s, acc_s = rest
    else:
        o_ref, q_s, m_s, l_s, acc_s = rest
    p_id = pl.program_id(2)
    g = pl.program_id(1)
    qi = qi_ref[p_id]
    kj = kj_ref[p_id]

    @pl.when(first_ref[p_id] == 1)
    def _():
        q_s[...] = (_stack_heads(q_ref, tq) * (HEAD_DIM ** -0.5)).astype(BF16)
        m_s[...] = jnp.full(m_s.shape, NEG, F32)
        l_s[...] = jnp.zeros(l_s.shape, F32)
        acc_s[...] = jnp.zeros(acc_s.shape, F32)

    def tile(masked):
        s = _kq(k_ref[...].astype(BF16), q_s[...])
        if masked:
            key, qry = _causal_keep(qi, kj, tq, tk)
            mask = key <= qry
            if selected:
                n_blk = sel_ref.shape[0]
                blk_of_key = kj * (tk // NSA_BLOCK) + (lax.broadcasted_iota(jnp.int32, (tk, n_blk), 0) >> BLOCK_SHIFT)
                expand = jnp.where(lax.broadcasted_iota(jnp.int32, (tk, n_blk), 1) == blk_of_key, 1.0, 0.0)
                picked = jnp.dot(expand.astype(BF16), sel_ref[...].astype(BF16), preferred_element_type=F32)
                mask = mask & (picked > 0.5)
            else:
                mask = mask & (key > qry - NSA_WINDOW)
            keep = jnp.where(mask, 1.0, 0.0)
            s = jnp.where(jnp.concatenate([keep] * NSA_HPG, axis=1) > 0.5, s, NEG)
        _flash_step(s, v_ref[...].astype(BF16), m_s, l_s, acc_s)

    if selected:
        tile(True)
    else:
        full = ((kj + 1) * tk - 1 <= qi * tq) & (kj * tk > (qi + 1) * tq - 1 - NSA_WINDOW)
        pl.when(full)(lambda: tile(False))
        pl.when(jnp.logical_not(full))(lambda: tile(True))

    @pl.when(last_ref[p_id] == 1)
    def _():
        gates = gate_ref[...]
        o = acc_s[...] / l_s[...]
        for h in range(NSA_HPG):
            r0 = gate_base + h
            grow = jnp.where(g == 0, gates[r0:r0 + 1, :], gates[r0 + NSA_HPG:r0 + NSA_HPG + 1, :])
            o_ref[:, h * 64:(h + 1) * 64] = (o[:, h * tq:(h + 1) * tq] * grow).T


def _band_pairs(nt, tq, tk, window):
    qi, kj = [], []
    for i in range(nt // tq):
        lo_key = 0 if window is None else max(0, i * tq - (window - 1))
        for j in range(lo_key // tk, ((i + 1) * tq - 1) // tk + 1):
            qi.append(i)
            kj.append(j)
    first = [1 if (p == 0 or qi[p] != qi[p - 1]) else 0 for p in range(len(qi))]
    last = first[1:] + [1]
    return tuple(jnp.asarray(a, jnp.int32) for a in (qi, kj, first, last))


def _nsa_band_t(pt, sel, tq, tk, selected):
    nb, _, nt = pt.shape
    qi, kj, first, last = _band_pairs(nt, tq, tk, None if selected else NSA_WINDOW)
    if selected:
        rk, rv, gate_base = R_NSA + 256, R_NSA + 384, 8
    else:
        rk, rv, gate_base = R_WIN, R_WIN + 128, 16
    idx = lambda f: (lambda b, g, p, qi, kj, fi, la: f(b, g, p, qi, kj))
    in_specs = [
        pl.BlockSpec((None, 256, tq), idx(lambda b, g, p, qi, kj: (b, R_NQ // 256 + g, qi[p]))),
        pl.BlockSpec((None, HEAD_DIM, tk), idx(lambda b, g, p, qi, kj: (b, rk // 64 + g, kj[p]))),
        pl.BlockSpec((None, HEAD_DIM, tk), idx(lambda b, g, p, qi, kj: (b, rv // 64 + g, kj[p]))),
        pl.BlockSpec((None, 32, tq), idx(lambda b, g, p, qi, kj: (b, R_GATE // 32, qi[p]))),
    ]
    args = [pt, pt, pt, pt]
    if selected:
        in_specs.append(pl.BlockSpec((None, None, sel.shape[2], tq),
                                     idx(lambda b, g, p, qi, kj: (b, g, 0, qi[p]))))
        args.append(sel)
    grid_spec = pltpu.PrefetchScalarGridSpec(
        num_scalar_prefetch=4,
        grid=(nb, NSA_KV_GROUPS, qi.shape[0]),
        in_specs=in_specs,
        out_specs=pl.BlockSpec((None, tq, 256), idx(lambda b, g, p, qi, kj: (b, qi[p], g))),
        scratch_shapes=[
            pltpu.VMEM((HEAD_DIM, NSA_HPG * tq), BF16),
            pltpu.VMEM((1, NSA_HPG * tq), F32),
            pltpu.VMEM((1, NSA_HPG * tq), F32),
            pltpu.VMEM((HEAD_DIM, NSA_HPG * tq), F32),
        ],
    )
    return pl.pallas_call(
        functools.partial(_nsa_band_kernel_t, tq=tq, tk=tk, selected=selected, gate_base=gate_base),
        grid_spec=grid_spec,
        out_shape=jax.ShapeDtypeStruct((nb, nt, 512), F32),
        compiler_params=_cparams(("parallel", "parallel", "arbitrary")),
        name="nsa_sel" if selected else "nsa_win",
    )(qi, kj, first, last, *args)


def _layer_norm_rows(x, g, b):
    mu = jnp.mean(x, axis=-1, keepdims=True)
    xc = x - mu
    var = jnp.mean(xc * xc, axis=-1, keepdims=True)
    return xc * lax.rsqrt(var + LN_EPS) * g + b


def _outproj_kernel(fox_ref, n1_ref, n2_ref, n3_ref, diff_ref, h_ref, w_ref, g_ref, b_ref, wr_ref, br_ref,
                    o_ref, lg_ref):
    nsa = (n1_ref[...] + n2_ref[...] + n3_ref[...]).astype(BF16)
    y = jnp.dot(fox_ref[...].astype(BF16), w_ref[0:256, :], preferred_element_type=F32)
    y = y + jnp.dot(nsa, w_ref[256:768, :], preferred_element_type=F32)
    y = y + jnp.dot(diff_ref[...].astype(BF16), w_ref[768:1024, :], preferred_element_type=F32)
    h1 = _layer_norm_rows(DEEPNORM_ALPHA * h_ref[...] + y, g_ref[...], b_ref[...])
    o_ref[...] = h1
    lg_ref[...] = jnp.dot(h1.astype(BF16), wr_ref[...], preferred_element_type=F32) + br_ref[...]


def _outproj(fox, n1, n2, n3, diff, h, w, g, b, wr, br, tm):
    m = h.shape[0]
    row = lambda i: (i, 0)
    const = lambda i: (0, 0)
    return pl.pallas_call(
        _outproj_kernel,
        grid=(m // tm,),
        in_specs=[
            pl.BlockSpec((tm, 256), row), pl.BlockSpec((tm, 512), row), pl.BlockSpec((tm, 512), row),
            pl.BlockSpec((tm, 512), row), pl.BlockSpec((tm, 256), row), pl.BlockSpec((tm, D_MODEL), row),
            pl.BlockSpec((D_MODEL, D_MODEL), const), pl.BlockSpec((1, D_MODEL), const),
            pl.BlockSpec((1, D_MODEL), const), pl.BlockSpec((D_MODEL, LANES), const),
            pl.BlockSpec((1, LANES), const),
        ],
        out_specs=[pl.BlockSpec((tm, D_MODEL), row), pl.BlockSpec((tm, LANES), row)],
        out_shape=[jax.ShapeDtypeStruct((m, D_MODEL), F32), jax.ShapeDtypeStruct((m, LANES), F32)],
        compiler_params=_cparams(("parallel",)),
        name="outproj_ln",
    )(fox, n1, n2, n3, diff, h, w, g, b, wr, br)


PERM_CHUNK = 256


def _deinterleave_kernel(w_ref, o_ref):
    half = w_ref.shape[1] // 2
    hc = PERM_CHUNK // 2
    src = lax.broadcasted_iota(jnp.int32, (PERM_CHUNK, PERM_CHUNK), 0)
    dst = lax.broadcasted_iota(jnp.int32, (PERM_CHUNK, PERM_CHUNK), 1)
    want = jnp.where(dst < hc, 2 * dst, 2 * (dst - hc) + 1)
    perm = jnp.where(src == want, 1.0, 0.0).astype(BF16)
    for c in range(w_ref.shape[1] // PERM_CHUNK):
        y = jnp.dot(w_ref[:, c * PERM_CHUNK:(c + 1) * PERM_CHUNK].astype(BF16), perm, preferred_element_type=F32)
        o_ref[:, c * hc:(c + 1) * hc] = y[:, :hc].astype(BF16)
        o_ref[:, half + c * hc:half + (c + 1) * hc] = y[:, hc:].astype(BF16)


def _deinterleave(w, tr, layer=None):
    ne, k, n2 = w.shape[-3:]
    if layer is None:
        in_spec = pl.BlockSpec((None, tr, n2), lambda e, i: (e, i, 0))
    else:
        in_spec = pl.BlockSpec((None, None, tr, n2), lambda e, i: (layer, e, i, 0))
    return pl.pallas_call(
        _deinterleave_kernel,
        grid=(ne, k // tr),
        in_specs=[in_spec],
        out_specs=pl.BlockSpec((None, tr, n2), lambda e, i: (e, i, 0)),
        out_shape=jax.ShapeDtypeStruct((ne, k, n2), BF16),
        compiler_params=_cparams(("parallel", "parallel")),
        name="moe_deinterleave",
    )(w)


def _moe_kernel(te_ref, nv_ref, x_ref, wu_ref, bg_ref, bl_ref, wd_ref, bd_ref, o_ref):
    i = pl.program_id(0)

    @pl.when(i < nv_ref[0])
    def _():
        hcat = jnp.dot(x_ref[...].astype(BF16), wu_ref[...], preferred_element_type=F32)
        hg = hcat[:, :D_MODEL] + bg_ref[...]
        hl = hcat[:, D_MODEL:] + bl_ref[...]
        glu = jnp.minimum(hg, SWIGLU_LIMIT)
        lin = jnp.clip(hl, -SWIGLU_LIMIT, SWIGLU_LIMIT)
        act = glu * (1.0 / (1.0 + jnp.exp(-SWIGLU_ALPHA * glu))) * (lin + 1.0)
        o_ref[...] = jnp.dot(act.astype(BF16), wd_ref[...].astype(BF16), preferred_element_type=F32) + bd_ref[...]

    @pl.when(i >= nv_ref[0])
    def _():
        o_ref[...] = jnp.zeros(o_ref.shape, F32)


def _moe_experts(tile_e, n_valid, xs, wu, bg, bl, wd, bd, tm, wd_layer=None):
    n_rows = xs.shape[0]
    ex = lambda i, te, nv: (te[i], 0, 0)
    if wd_layer is None:
        wd_spec = pl.BlockSpec((None, D_MODEL, D_MODEL), ex)
    else:
        wd_spec = pl.BlockSpec((None, None, D_MODEL, D_MODEL), lambda i, te, nv: (wd_layer, te[i], 0, 0))
    grid_spec = pltpu.PrefetchScalarGridSpec(
        num_scalar_prefetch=2,
        grid=(n_rows // tm,),
        in_specs=[
            pl.BlockSpec((tm, D_MODEL), lambda i, te, nv: (i, 0)),
            pl.BlockSpec((None, D_MODEL, 2 * D_MODEL), ex),
            pl.BlockSpec((None, 1, D_MODEL), ex),
            pl.BlockSpec((None, 1, D_MODEL), ex),
            wd_spec,
            pl.BlockSpec((None, 1, D_MODEL), ex),
        ],
        out_specs=pl.BlockSpec((tm, D_MODEL), lambda i, te, nv: (i, 0)),
    )
    return pl.pallas_call(
        _moe_kernel,
        grid_spec=grid_spec,
        out_shape=jax.ShapeDtypeStruct((n_rows, D_MODEL), F32),
        compiler_params=_cparams(("arbitrary",)),
        name="moe_experts",
    )(tile_e, n_valid, xs, wu, bg, bl, wd, bd)


def _combine_kernel(h_ref, y_ref, gate_ref, g_ref, b_ref, o_ref):
    gates = gate_ref[...]
    f = y_ref[0] * gates[:, 0:1]
    for k in range(1, TOP_K):
        f = f + y_ref[k] * gates[:, k:k + 1]
    o_ref[...] = _layer_norm_rows(DEEPNORM_ALPHA * h_ref[...] + f, g_ref[...], b_ref[...])


def _combine_ln(h1, yg, gates, g, b, tm):
    m = h1.shape[0]
    row = lambda i: (i, 0)
    const = lambda i: (0, 0)
    return pl.pallas_call(
        _combine_kernel,
        grid=(m // tm,),
        in_specs=[
            pl.BlockSpec((tm, D_MODEL), row),
            pl.BlockSpec((TOP_K, tm, D_MODEL), lambda i: (0, i, 0)),
            pl.BlockSpec((tm, TOP_K), row),
            pl.BlockSpec((1, D_MODEL), const), pl.BlockSpec((1, D_MODEL), const),
        ],
        out_specs=pl.BlockSpec((tm, D_MODEL), row),
        out_shape=jax.ShapeDtypeStruct((m, D_MODEL), F32),
        compiler_params=_cparams(("parallel",)),
        name="moe_combine_ln",
    )(h1, yg, gates, g, b)


def _moe_block(h1, logits, wts, ln_g, ln_b, tm_moe, tm_row, wd_layer=None):
    wu, bg, bl, wd, bd = wts
    m = h1.shape[0]
    top_val, top_idx = lax.top_k(logits, TOP_K)
    gate = jax.nn.softmax(top_val, axis=-1)
    n = m * TOP_K
    flat_e = top_idx.reshape(n).astype(jnp.int32)
    iota_n = jnp.arange(n, dtype=jnp.int32)
    sorted_e, order = lax.sort_key_val(flat_e, iota_n)
    counts = jnp.sum(flat_e[:, None] == jnp.arange(N_EXPERTS, dtype=jnp.int32)[None, :], axis=0, dtype=jnp.int32)
    padded = (counts + tm_moe - 1) // tm_moe * tm_moe
    starts = jnp.cumsum(counts) - counts
    pends = jnp.cumsum(padded)
    gstart = pends - padded
    dest = gstart[sorted_e] + iota_n - starts[sorted_e]
    n_rows = -(-n // tm_moe) * tm_moe + N_EXPERTS * tm_moe
    n_tiles = n_rows // tm_moe
    tile_start = jnp.arange(n_tiles, dtype=jnp.int32) * tm_moe
    tile_e = jnp.minimum(jnp.sum(pends[None, :] <= tile_start[:, None], axis=1, dtype=jnp.int32), N_EXPERTS - 1)
    n_valid = (pends[-1] // tm_moe).astype(jnp.int32).reshape(1)
    k_in_group = (jnp.arange(n_rows, dtype=jnp.int32).reshape(n_tiles, tm_moe) - gstart[tile_e][:, None])
    src = jnp.clip(starts[tile_e][:, None] + k_in_group, 0, n - 1)
    row_tok = jnp.where(k_in_group < counts[tile_e][:, None], (order // TOP_K)[src], m).reshape(n_rows)
    _, pos = lax.sort_key_val(order, dest)
    pos = pos.reshape(m, TOP_K)
    xs = jnp.concatenate([h1, jnp.zeros((1, D_MODEL), F32)], 0)[row_tok]
    out = _moe_experts(tile_e, n_valid, xs, wu, bg, bl, wd, bd, tm_moe, wd_layer)
    yg = out[pos.T]
    return _combine_ln(h1, yg, gate, ln_g, ln_b, tm_row)


def _tree_sum(xs):
    xs = list(xs)
    while len(xs) > 1:
        xs = [xs[i] + xs[i + 1] for i in range(0, len(xs) - 1, 2)] + ([xs[-1]] if len(xs) % 2 else [])
    return xs[0]


def _sublane_allsum(x):
    x = x + pltpu.roll(x, 4, 0)
    x = x + pltpu.roll(x, 2, 0)
    return x + pltpu.roll(x, 1, 0)


def _lane_suffix_scan(x):
    lane = lax.broadcasted_iota(jnp.int32, x.shape, 1)
    k = 1
    while k < LANES:
        x = x + jnp.where(lane + k < LANES, pltpu.roll(x, LANES - k, 1), 0.0)
        k *= 2
    return x


def _stream_update(logits, m_ref, l_ref, valid=None):
    m_prev = m_ref[...]
    m_new = jnp.maximum(m_prev, logits)
    alpha = jnp.exp(m_prev - m_new)
    p = jnp.exp(logits - m_new)
    if valid is not None:
        p = jnp.where(valid, p, 0.0)
    l_ref[...] = l_ref[...] * alpha + p
    m_ref[...] = m_new
    return alpha, p


def _lane_merge(m_ref, l_ref):
    m = m_ref[...]
    w = jnp.exp(m - jnp.max(m, axis=1, keepdims=True))
    tot = jnp.sum(l_ref[...] * w, axis=1, keepdims=True)
    return w, 1.0 / tot


FOX_WIN = 520
FOX_VROW0 = 256


def _fox_sample_kernel(pt_ref, *refs, kpages, r0):
    pages = refs[:2 * kpages]
    qc_ref, new_ref, o_ref, qcb_s, m_s, l_s, s_s, acc_s, scan_s = refs[2 * kpages:]
    j = pl.program_id(1)
    sub = lax.broadcasted_iota(jnp.int32, (SUBLANES, LANES), 0)
    lane = lax.broadcasted_iota(jnp.int32, (SUBLANES, LANES), 1)
    lf_rows = (sub >= r0) & (sub < r0 + FOX_HEADS)
    n_kv = FOX_VROW0 // SUBLANES + 1

    def page(main, tail, valid):
        t = _tree_sum([main(i) * qcb_s[8 * i:8 * i + 8, :] for i in range(n_kv)])
        qk = t + pltpu.roll(t, 4, 0)
        lf = jnp.where(lf_rows, tail, 0.0)
        lf = lf + pltpu.roll(lf, 4, 0)
        if valid is not None:
            lf = jnp.where(valid, lf, 0.0)
        parts = jnp.concatenate([t.astype(F32) for t in _split3(lf)], axis=0).astype(BF16)
        scan = jnp.dot(parts, scan_s[...], preferred_element_type=F32)
        scan = scan[0:8, :] + scan[8:16, :] + scan[16:24, :]
        later, total = scan[:, 0:LANES], scan[:, LANES:2 * LANES]
        logits = qk + s_s[...] + later
        if valid is not None:
            logits = jnp.where(valid, logits, NEG)
        alpha, p = _stream_update(logits, m_s, l_s, valid)
        n_main = (FOX_WIN - SUBLANES - FOX_VROW0) // SUBLANES
        for i in range(n_main):
            r = 8 * i
            acc_s[r:r + 8, :] = acc_s[r:r + 8, :] * alpha + main(FOX_VROW0 // SUBLANES + i) * p
        r = 8 * n_main
        acc_s[r:r + 8, :] = acc_s[r:r + 8, :] * alpha + tail * p
        s_s[...] = s_s[...] + total

    @pl.when(j == 0)
    def _():
        src = lax.broadcasted_iota(jnp.int32, (LANES, 2 * LANES), 0)
        dst = lax.broadcasted_iota(jnp.int32, (LANES, 2 * LANES), 1)
        scan_s[...] = jnp.where((dst >= LANES) | (src > dst), 1.0, 0.0).astype(BF16)
        qcb_s[...] = jnp.broadcast_to(qc_ref[...], qcb_s.shape)
        m_s[...] = jnp.full(m_s.shape, NEG, F32)
        l_s[...] = jnp.zeros(l_s.shape, F32)
        s_s[...] = jnp.zeros(s_s.shape, F32)
        acc_s[...] = jnp.zeros(acc_s.shape, F32)
        page(lambda i: jnp.broadcast_to(new_ref[8 * i:8 * i + 8, :], (SUBLANES, LANES)),
             jnp.broadcast_to(new_ref[FOX_WIN - 8:FOX_WIN, :], (SUBLANES, LANES)), lane == 0)

    for k in range(kpages):
        main_ref, tail_ref = pages[2 * k], pages[2 * k + 1]
        page(lambda i, ref=main_ref: ref[8 * i:8 * i + 8, :], tail_ref[...], None)

    @pl.when(j == pl.num_programs(1) - 1)
    def _():
        w, inv = _lane_merge(m_s, l_s)
        for i in range((FOX_WIN - FOX_VROW0) // SUBLANES):
            r = 8 * i
            o_ref[r:r + 8, :] = jnp.sum(acc_s[r:r + 8, :] * w, axis=1, keepdims=True) * inv


def _fox_sample_attn(cache_fm, page_table, layer, qcol, newcol, kpages):
    nb, n_pages = page_table.shape
    r0 = 4 * layer
    n_chunks = n_pages // kpages

    def page_idx(b, j, pt, k):
        return pt[b, n_pages - 1 - (j * kpages + k)]

    in_specs = []
    for k in range(kpages):
        in_specs.append(pl.BlockSpec((None, 512, LANES), lambda b, j, pt, k=k: (page_idx(b, j, pt, k), layer, 0)))
        in_specs.append(pl.BlockSpec((None, 8, LANES), lambda b, j, pt, k=k: (page_idx(b, j, pt, k), 64 * (layer + 1), 0)))
    in_specs += [pl.BlockSpec((None, FOX_WIN, 1), lambda b, j, pt: (b, 0, 0)),
                 pl.BlockSpec((None, FOX_WIN, 1), lambda b, j, pt: (b, 0, 0))]
    n_out = FOX_WIN - FOX_VROW0
    grid_spec = pltpu.PrefetchScalarGridSpec(
        num_scalar_prefetch=1,
        grid=(nb, n_chunks),
        in_specs=in_specs,
        out_specs=pl.BlockSpec((None, n_out, 1), lambda b, j, pt: (b, 0, 0)),
        scratch_shapes=[
            pltpu.VMEM((FOX_WIN, LANES), F32),
            pltpu.VMEM((SUBLANES, LANES), F32), pltpu.VMEM((SUBLANES, LANES), F32), pltpu.VMEM((SUBLANES, LANES), F32),
            pltpu.VMEM((n_out, LANES), F32),
            pltpu.VMEM((LANES, 2 * LANES), BF16),
        ],
    )
    return pl.pallas_call(
        functools.partial(_fox_sample_kernel, kpages=kpages, r0=r0),
        grid_spec=grid_spec,
        out_shape=jax.ShapeDtypeStruct((nb, n_out, 1), F32),
        compiler_params=_cparams(("parallel", "arbitrary")),
        name="fox_sample",
    )(page_table, *([cache_fm] * (2 * kpages)), qcol, newcol)


def _bcast_row(x, s):
    return jnp.broadcast_to(x[s:s + 1, :], x.shape)


def _pack_streams(ts):
    sub = lax.broadcasted_iota(jnp.int32, ts[0].shape, 0)
    out = ts[0]
    for s in range(1, len(ts)):
        out = jnp.where(sub == s, ts[s], out)
    return out


def _kv_page_update(x, qcb_s, k_rows, v_rows, m_s, l_s, acc_s, valid):
    ts = []
    for s, (g0, cnt, c0) in enumerate(k_rows):
        t = _tree_sum([x(g0 + i) * qcb_s[c0 + 8 * i:c0 + 8 * i + 8, :] for i in range(cnt)])
        ts.append(_sublane_allsum(t))
    logits = _pack_streams(ts)
    if valid is not None:
        logits = jnp.where(valid, logits, NEG)
    alpha, p = _stream_update(logits, m_s, l_s, valid)
    for s, g0 in enumerate(v_rows):
        ab, pb = _bcast_row(alpha, s), _bcast_row(p, s)
        for i in range(8):
            r = s * 64 + 8 * i
            acc_s[r:r + 8, :] = acc_s[r:r + 8, :] * ab + x(g0 + i) * pb


def _merged_rows(acc_s, w, inv, s):
    wb = _bcast_row(w, s)
    return [jnp.sum(acc_s[s * 64 + 8 * i:s * 64 + 8 * i + 8, :] * wb, axis=1, keepdims=True) * inv[s:s + 1, :]
            for i in range(8)]


_DIFF_K_ROWS = tuple((16 * (s // 2) + 4 * (s % 2), 4, 128 * (s // 2) + 32 * (s % 2)) for s in range(8))
_DIFF_V_ROWS = tuple(16 * (s // 2) + 8 for s in range(8))


def _diff_sample_kernel(pt_ref, lam_ref, *refs, kpages, out_scale):
    pages = refs[:kpages]
    qc_ref, new_ref, g_ref, o_ref, qcb_s, m_s, l_s, acc_s = refs[kpages:]
    j = pl.program_id(1)
    lane = lax.broadcasted_iota(jnp.int32, (SUBLANES, LANES), 1)

    @pl.when(j == 0)
    def _():
        qcb_s[...] = jnp.broadcast_to(qc_ref[...], qcb_s.shape)
        m_s[...] = jnp.full(m_s.shape, NEG, F32)
        l_s[...] = jnp.zeros(l_s.shape, F32)
        acc_s[...] = jnp.zeros(acc_s.shape, F32)
        _kv_page_update(lambda i: jnp.broadcast_to(new_ref[8 * i:8 * i + 8, :], (SUBLANES, LANES)),
                        qcb_s, _DIFF_K_ROWS, _DIFF_V_ROWS, m_s, l_s, acc_s, lane == 0)

    for k in range(kpages):
        _kv_page_update(lambda i, ref=pages[k]: ref[8 * i:8 * i + 8, :],
                        qcb_s, _DIFF_K_ROWS, _DIFF_V_ROWS, m_s, l_s, acc_s, None)

    @pl.when(j == pl.num_programs(1) - 1)
    def _():
        w, inv = _lane_merge(m_s, l_s)
        lam = lam_ref[0]
        for h in range(DIFF_HEADS):
            o1 = _merged_rows(acc_s, w, inv, 2 * h)
            o2 = _merged_rows(acc_s, w, inv, 2 * h + 1)
            o = [a - lam * b for a, b in zip(o1, o2)]
            ss = o[0] * o[0]
            for i in range(1, 8):
                ss = ss + o[i] * o[i]
            r = lax.rsqrt(_sublane_allsum(ss) * (1.0 / HEAD_DIM) + RMS_EPS)
            for i in range(8):
                o_ref[h * 64 + 8 * i:h * 64 + 8 * i + 8, :] = o[i] * r * g_ref[8 * i:8 * i + 8, :] * out_scale


def _diff_sample_attn(cache_fm, page_table, layer, lam, qcol, newcol, gcol, out_scale, kpages):
    nb, n_pages = page_table.shape
    in_specs = [pl.BlockSpec((None, 512, LANES), lambda b, j, pt, lam, k=k: (pt[b, j * kpages + k], layer, 0))
                for k in range(kpages)]
    in_specs += [pl.BlockSpec((None, 512, 1), lambda b, j, pt, lam: (b, 0, 0)),
                 pl.BlockSpec((None, 512, 1), lambda b, j, pt, lam: (b, 0, 0)),
                 pl.BlockSpec((HEAD_DIM, 1), lambda b, j, pt, lam: (0, 0))]
    grid_spec = pltpu.PrefetchScalarGridSpec(
        num_scalar_prefetch=2,
        grid=(nb, n_pages // kpages),
        in_specs=in_specs,
        out_specs=pl.BlockSpec((None, 256, 1), lambda b, j, pt, lam: (b, 0, 0)),
        scratch_shapes=[pltpu.VMEM((512, LANES), F32), pltpu.VMEM((SUBLANES, LANES), F32),
                        pltpu.VMEM((SUBLANES, LANES), F32), pltpu.VMEM((512, LANES), F32)],
    )
    return pl.pallas_call(
        functools.partial(_diff_sample_kernel, kpages=kpages, out_scale=out_scale),
        grid_spec=grid_spec,
        out_shape=jax.ShapeDtypeStruct((nb, 256, 1), F32),
        compiler_params=_cparams(("parallel", "arbitrary")),
        name="diff_sample",
    )(page_table, lam, *([cache_fm] * kpages), qcol, newcol, gcol)


def _nsa_compress_sample_kernel(pt_ref, *refs, kpages, n_pages):
    pages = refs[:kpages]
    w2_ref, pe2_ref, qh_ref, g0_ref, oc_ref, sel_ref, flag_ref, buf_s = refs[kpages:]
    j = pl.program_id(1)
    scale = HEAD_DIM ** -0.5
    chunk = jnp.stack([pages[k][...] for k in range(kpages)], axis=0)
    buf_s[:, pl.ds(pl.multiple_of(j * kpages, kpages), kpages), :] = pltpu.einshape("prl->rpl", chunk)

    @pl.when(j == pl.num_programs(1) - 1)
    def _():
        kv = []
        for t in range(2):
            acc = jnp.zeros((2 * n_pages, LANES), F32)
            for d in range(HEAD_DIM):
                x = jnp.concatenate([buf_s[t * 128 + g * 64 + d] for g in range(NSA_KV_GROUPS)], axis=0)
                x = (x + pe2_ref[t, d:d + 1, :]).astype(BF16)
                acc = acc + jnp.dot(x, w2_ref[t, d], preferred_element_type=F32)
            kv.append(acc)
        any_sel = jnp.zeros((n_pages, 1), F32)
        for g in range(NSA_KV_GROUPS):
            kc = kv[0][g * n_pages:(g + 1) * n_pages, :].astype(BF16)
            vc = kv[1][g * n_pages:(g + 1) * n_pages, :].astype(BF16)
            lc = [jnp.dot(kc, qh_ref[g, hf].astype(BF16), preferred_element_type=F32) * scale for hf in range(2)]
            mx = jnp.maximum(jnp.max(lc[0], axis=0, keepdims=True), jnp.max(lc[1], axis=0, keepdims=True))
            pe = [jnp.exp(c - mx) for c in lc]
            den = jnp.maximum(jnp.sum(pe[0], axis=0, keepdims=True) + jnp.sum(pe[1], axis=0, keepdims=True), 1e-30)
            pc = [e / den for e in pe]
            res = [lax.dot_general(c.astype(BF16), vc, (((0,), (0,)), ((), ())), preferred_element_type=F32)
                   for c in pc]
            oc = res[0][:, 0:HEAD_DIM] + res[1][:, HEAD_DIM:2 * HEAD_DIM]
            oc_ref[g * NSA_HPG:(g + 1) * NSA_HPG, :] = oc * g0_ref[g * NSA_HPG:(g + 1) * NSA_HPG, :]
            imp = [jnp.sum(c, axis=1, keepdims=True) for c in pc]
            pidx = lax.broadcasted_iota(jnp.int32, (n_pages, 1), 0)
            score = [jnp.where(pidx == 0, FORCE_SCORE, imp[0]), jnp.where(pidx == n_pages - 1, FORCE_SCORE, imp[1])]
            rows = [_col_to_row(c) for c in score]
            prow = lax.broadcasted_iota(jnp.int32, (n_pages, n_pages), 1)
            pcol = lax.broadcasted_iota(jnp.int32, (n_pages, n_pages), 0)
            sel = []
            for hf in range(2):
                cnt = jnp.zeros((n_pages, 1), F32)
                for hf2 in range(2):
                    before = (2 * prow + hf2) < (2 * pcol + hf)
                    ahead = (rows[hf2] > score[hf]) | ((rows[hf2] == score[hf]) & before)
                    cnt = cnt + jnp.sum(jnp.where(ahead, 1.0, 0.0), axis=1, keepdims=True)
                cnt = cnt + jnp.where(score[hf] < FORCE_SCORE, 1.0, 0.0)
                sel.append(jnp.where(cnt < float(NSA_TOP_N), 1.0, 0.0))
            lane_p = lax.broadcasted_iota(jnp.int32, (n_pages, LANES), 1)
            sel_ref[g] = jnp.where(lane_p < NSA_BLOCK, sel[0], sel[1])
            any_sel = jnp.maximum(any_sel, jnp.maximum(sel[0], sel[1]))
        flag_ref[...] = any_sel


def _nsa_compress_sample(cache_fm, page_table, layer, w2, pe2, qh, g0, kpages):
    nb, n_pages = page_table.shape
    in_specs = [pl.BlockSpec((None, 256, LANES), lambda b, j, pt, k=k: (pt[b, j * kpages + k], 2 * layer, 0))
                for k in range(kpages)]
    in_specs += [pl.BlockSpec((2, HEAD_DIM, LANES, LANES), lambda b, j, pt: (0, 0, 0, 0)),
                 pl.BlockSpec((2, HEAD_DIM, LANES), lambda b, j, pt: (0, 0, 0)),
                 pl.BlockSpec((None, NSA_KV_GROUPS, 2, LANES, NSA_HPG), lambda b, j, pt: (b, 0, 0, 0, 0)),
                 pl.BlockSpec((None, NSA_HEADS, HEAD_DIM), lambda b, j, pt: (b, 0, 0))]
    grid_spec = pltpu.PrefetchScalarGridSpec(
        num_scalar_prefetch=1,
        grid=(nb, n_pages // kpages),
        in_specs=in_specs,
        out_specs=[pl.BlockSpec((None, NSA_HEADS, HEAD_DIM), lambda b, j, pt: (b, 0, 0)),
                   pl.BlockSpec((None, NSA_KV_GROUPS, n_pages, LANES), lambda b, j, pt: (b, 0, 0, 0)),
                   pl.BlockSpec((None, n_pages, 1), lambda b, j, pt: (b, 0, 0))],
        scratch_shapes=[pltpu.VMEM((256, n_pages, LANES), F32)],
    )
    return pl.pallas_call(
        functools.partial(_nsa_compress_sample_kernel, kpages=kpages, n_pages=n_pages),
        grid_spec=grid_spec,
        out_shape=[jax.ShapeDtypeStruct((nb, NSA_HEADS, HEAD_DIM), F32),
                   jax.ShapeDtypeStruct((nb, NSA_KV_GROUPS, n_pages, LANES), F32),
                   jax.ShapeDtypeStruct((nb, n_pages, 1), F32)],
        compiler_params=_cparams(("parallel", "arbitrary")),
        name="nsa_compress_sample",
    )(page_table, *([cache_fm] * kpages), w2, pe2, qh, g0)


_NSA_K_ROWS = tuple((8 * (s // NSA_HPG), 8, 64 * s) for s in range(NSA_HEADS))
_NSA_V_ROWS = tuple(16 + 8 * (s // NSA_HPG) for s in range(NSA_HEADS))


def _col_to_row(col):
    n = col.shape[0]
    eye = lax.broadcasted_iota(jnp.int32, (n, n), 0) == lax.broadcasted_iota(jnp.int32, (n, n), 1)
    return jnp.sum(jnp.where(eye, col, 0.0), axis=0, keepdims=True)


def _nsa_sample_kernel(eff_ref, flag_ref, *refs, kpages, n_pages):
    pages = refs[:kpages]
    (sel_ref, qc_ref, newsel_ref, newwin_ref, win_ref, g12_ref, osw_ref, wout_ref,
     qcb_s, m_s, l_s, acc_s, mw_s, lw_s, accw_s) = refs[kpages:]
    b = pl.program_id(0)
    j = pl.program_id(1)
    lane = lax.broadcasted_iota(jnp.int32, (SUBLANES, LANES), 1)

    @pl.when(j == 0)
    def _():
        qcb_s[...] = jnp.broadcast_to(qc_ref[...], qcb_s.shape)
        for ref in (m_s, mw_s):
            ref[...] = jnp.full(ref.shape, NEG, F32)
        for ref in (l_s, acc_s, lw_s, accw_s):
            ref[...] = jnp.zeros(ref.shape, F32)
        _kv_page_update(lambda i: jnp.broadcast_to(newsel_ref[8 * i:8 * i + 8, :], (SUBLANES, LANES)),
                        qcb_s, _NSA_K_ROWS, _NSA_V_ROWS, m_s, l_s, acc_s, lane == 0)
        _kv_page_update(lambda i: jnp.broadcast_to(newwin_ref[8 * i:8 * i + 8, :], (SUBLANES, LANES)),
                        qcb_s, _NSA_K_ROWS, _NSA_V_ROWS, mw_s, lw_s, accw_s, lane == 0)
        for c in range(NSA_WINDOW // LANES):
            valid = (lane >= 1) if c == 0 else None
            _kv_page_update(lambda i, c=c: win_ref[8 * i:8 * i + 8, c * LANES:(c + 1) * LANES],
                            qcb_s, _NSA_K_ROWS, _NSA_V_ROWS, mw_s, lw_s, accw_s, valid)
        lane_w = lax.broadcasted_iota(jnp.int32, (256, NSA_WINDOW), 1)
        shifted = pltpu.roll(win_ref[...], NSA_WINDOW - 1, 1)
        wout_ref[...] = jnp.where(lane_w == NSA_WINDOW - 1, newwin_ref[...], shifted)

    sub = lax.broadcasted_iota(jnp.int32, (SUBLANES, LANES), 0)
    for k in range(kpages):
        p_idx = j * kpages + k

        @pl.when(flag_ref[b, p_idx] != 0)
        def _(k=k, p_idx=p_idx):
            picked = jnp.where(sub < NSA_HPG, jnp.broadcast_to(sel_ref[0, pl.ds(p_idx, 1), :], (SUBLANES, LANES)),
                               jnp.broadcast_to(sel_ref[1, pl.ds(p_idx, 1), :], (SUBLANES, LANES)))
            _kv_page_update(lambda i, ref=pages[k]: ref[8 * i:8 * i + 8, :],
                            qcb_s, _NSA_K_ROWS, _NSA_V_ROWS, m_s, l_s, acc_s, picked > 0.5)

    @pl.when(j == pl.num_programs(1) - 1)
    def _():
        w, inv = _lane_merge(m_s, l_s)
        ww, invw = _lane_merge(mw_s, lw_s)
        for s in range(NSA_HEADS):
            o_sel = _merged_rows(acc_s, w, inv, s)
            o_win = _merged_rows(accw_s, ww, invw, s)
            for i in range(8):
                r = s * 64 + 8 * i
                osw_ref[r:r + 8, :] = o_sel[i] * g12_ref[0, r:r + 8, :] + o_win[i] * g12_ref[1, r:r + 8, :]


def _nsa_sample_attn(cache_fm, win_fm, page_table, layer, sel, flags, qcol, newsel, newwin, g12, kpages):
    nb, n_pages = page_table.shape
    n_chunks = n_pages // kpages
    step = jnp.arange(n_chunks, dtype=jnp.int32)[None, :, None]
    last = lax.cummax(jnp.where(flags.reshape(nb, n_chunks, kpages) != 0, step, -1), axis=1)
    slot = jnp.arange(kpages, dtype=jnp.int32)[None, None, :]
    eff_idx = (jnp.maximum(last, 0) * kpages + slot).reshape(nb, n_pages)
    eff = jnp.take_along_axis(page_table, eff_idx, axis=1).astype(jnp.int32)
    c3 = lambda b, j, eff, fl: (b, 0, 0)
    c4 = lambda b, j, eff, fl: (b, 0, 0, 0)
    in_specs = [pl.BlockSpec((None, 256, LANES),
                             lambda b, j, eff, fl, k=k: (eff[b, j * kpages + k], 2 * layer + 1, 0))
                for k in range(kpages)]
    in_specs += [
        pl.BlockSpec((None, NSA_KV_GROUPS, n_pages, LANES), c4),
        pl.BlockSpec((None, 512, 1), c3),
        pl.BlockSpec((None, 256, 1), c3),
        pl.BlockSpec((None, 256, 1), c3),
        pl.BlockSpec((None, None, 256, NSA_WINDOW), lambda b, j, eff, fl: (b, layer, 0, 0)),
        pl.BlockSpec((None, 2, 512, 1), c4),
    ]
    grid_spec = pltpu.PrefetchScalarGridSpec(
        num_scalar_prefetch=2,
        grid=(nb, n_chunks),
        in_specs=in_specs,
        out_specs=[pl.BlockSpec((None, 512, 1), c3),
                   pl.BlockSpec((None, 256, NSA_WINDOW), c3)],
        scratch_shapes=[
            pltpu.VMEM((512, LANES), F32),
            pltpu.VMEM((SUBLANES, LANES), F32), pltpu.VMEM((SUBLANES, LANES), F32), pltpu.VMEM((512, LANES), F32),
            pltpu.VMEM((SUBLANES, LANES), F32), pltpu.VMEM((SUBLANES, LANES), F32), pltpu.VMEM((512, LANES), F32),
        ],
    )
    return pl.pallas_call(
        functools.partial(_nsa_sample_kernel, kpages=kpages, n_pages=n_pages),
        grid_spec=grid_spec,
        out_shape=[jax.ShapeDtypeStruct((nb, 512, 1), F32),
                   jax.ShapeDtypeStruct((nb, 256, NSA_WINDOW), F32)],
        compiler_params=_cparams(("parallel", "arbitrary")),
        name="nsa_sample",
    )(eff, flags, *([cache_fm] * kpages), sel, qcol, newsel, newwin, win_fm, g12)


SampleProj = collections.namedtuple(
    "SampleProj", ["fq", "fox_rows", "nq", "ng", "nsa_rows", "win_rows", "dq", "diff_rows"])


def _fox_cache_view(c):
    return c.transpose(0, 1, 4, 3, 2).reshape(c.shape[0], 2 * 516, PAGE_SIZE)


def _kv_cache_view(c):
    return c.transpose(0, 1, 3, 4, 5, 2).reshape(c.shape[0], 2 * 512, PAGE_SIZE)


def _win_state_view(s):
    return s.transpose(0, 1, 3, 4, 5, 2).reshape(s.shape[0], DEPTH, 256, NSA_WINDOW)


def _fox_sample(cache_fm, page_table, layer, sp, kpages):
    nb = sp.fq.shape[0]
    r0 = 4 * layer
    qk = (sp.fq.reshape(nb, FOX_HEADS, HEAD_DIM) * (HEAD_DIM ** -0.5)).transpose(0, 2, 1).reshape(nb, 256)
    qcol = jnp.zeros((nb, FOX_WIN), F32).at[:, r0:r0 + 256].set(qk)
    rows = sp.fox_rows.reshape(nb, FOX_HEADS, 129).transpose(0, 2, 1).reshape(nb, 516)
    newcol = jnp.zeros((nb, FOX_WIN), F32).at[:, r0:r0 + 516].set(rows)
    o = _fox_sample_attn(cache_fm, page_table, layer, qcol[..., None], newcol[..., None], kpages)[..., 0]
    return o[:, r0:r0 + 256].reshape(nb, HEAD_DIM, FOX_HEADS).transpose(0, 2, 1).reshape(nb, 256)


def _diff_sample(cache_fm, page_table, layer, sp, lam, lam_init, g, kpages):
    nb = sp.dq.shape[0]
    q = sp.dq.reshape(nb, DIFF_HEADS, HEAD_DIM) * (DIFF_QK_DIM ** -0.5)
    qcol = jnp.concatenate([q, jnp.zeros_like(q)], axis=-1).reshape(nb, 512, 1)
    newcol = sp.diff_rows.reshape(nb, 512, 1)
    o = _diff_sample_attn(cache_fm, page_table, layer, lam.reshape(1).astype(F32), qcol, newcol,
                          g.reshape(HEAD_DIM, 1), 1.0 - lam_init, kpages)
    return o.reshape(nb, 256)


def _nsa_sample(cache_fm, win_fm, page_table, layer, sp, cmp_pos, cmp_w, kpages):
    nb = sp.nq.shape[0]
    w = cmp_w.transpose(0, 2, 1, 3)
    z = jnp.zeros_like(w)
    w2 = jnp.concatenate([jnp.concatenate([w, z], -1), jnp.concatenate([z, w], -1)], -2).astype(BF16)
    pe = cmp_pos.transpose(0, 2, 1)
    pe2 = jnp.concatenate([pe, pe], -1)
    q = sp.nq.reshape(nb, NSA_KV_GROUPS, NSA_HPG, HEAD_DIM)
    qt = q.transpose(0, 1, 3, 2)
    zq = jnp.zeros_like(qt)
    qh = jnp.stack([jnp.concatenate([qt, zq], 2), jnp.concatenate([zq, qt], 2)], 2)
    gates = sp.ng.reshape(nb, 3, NSA_HEADS)
    g0 = jnp.broadcast_to(gates[:, 0, :, None], (nb, NSA_HEADS, HEAD_DIM))
    g12 = jnp.broadcast_to(gates[:, 1:, :, None], (nb, 2, NSA_HEADS, HEAD_DIM)).reshape(nb, 2, 512, 1)
    oc, sel, flags = _nsa_compress_sample(cache_fm, page_table, layer, w2, pe2, qh, g0, kpages)
    qcol = (sp.nq.reshape(nb, 512) * (HEAD_DIM ** -0.5))[..., None]
    newsel = sp.nsa_rows.reshape(nb, 512)[:, 256:, None]
    newwin = sp.win_rows.reshape(nb, 256, 1)
    osw, wnew = _nsa_sample_attn(cache_fm, win_fm, page_table, layer, sel, (flags[..., 0] > 0.5).astype(jnp.int32),
                                 qcol, newsel, newwin, g12, kpages)
    win_new = wnew.reshape(nb, 2, NSA_KV_GROUPS, HEAD_DIM, NSA_WINDOW).transpose(0, 4, 1, 2, 3)
    return oc.reshape(nb, 512) + osw.reshape(nb, 512), win_new


def _diff_lambda(lp, layer):
    lam_init = 0.8 - 0.6 * math.exp(-0.3 * layer)
    lp = lp.astype(F32)
    lam = jnp.exp(jnp.sum(lp[0] * lp[1])) - jnp.exp(jnp.sum(lp[2] * lp[3])) + lam_init
    return lam, lam_init


def _pad_rows(a, n):
    return jnp.concatenate([a, jnp.zeros((n - a.shape[0],) + a.shape[1:], a.dtype)], 0)


def kernel(x_prompt, x_sample, cache_fox, cache_nsa, cache_diff, state_nsa_win, page_table,
           w_in, b_forget, b_nsa_gate, nsa_cmp_pos, nsa_cmp_w, diff_lambda, diff_subln_g,
           w_out, ln_attn_g, ln_attn_b, w_router, b_router, w_up, b_up, w_down, b_down,
           ln_ffn_g, ln_ffn_b):
    nb, seq, _ = x_prompt.shape
    n_dec, n_new, _ = x_sample.shape
    past = page_table.shape[1] * cache_fox.shape[2]
    tabs_p = _rope_tables(jnp.arange(seq, dtype=F32))
    n_dec_pad = LANES
    tabs_s = _rope_tables(jnp.full((n_dec_pad,), past, F32))
    perm = jnp.asarray(np.maximum(_PERM, 0), jnp.int32)
    valid = jnp.asarray((_PERM >= 0).astype(np.float32))[:, None]

    assert n_new == 1 and past % NSA_BLOCK == 0 and state_nsa_win.shape[2] == NSA_WINDOW
    fox_fm = _fox_cache_view(cache_fox)
    nsa_fm = _kv_cache_view(cache_nsa)
    diff_fm = _kv_cache_view(cache_diff)
    win_fm = _win_state_view(state_nsa_win)

    hp = x_prompt
    hs = x_sample.reshape(n_dec, D_MODEL)
    m_p = nb * seq
    m_all = m_p + n_dec
    m_pad = -(-m_all // 512) * 512
    outs = {k: [] for k in ('fox_p', 'fox_s', 'nsa_p', 'nsa_s', 'diff_p', 'diff_s', 'win_p', 'win_s')}
    for l in range(DEPTH):
        lam, lam_init = _diff_lambda(diff_lambda[l], l)
        lam_arr = lam.reshape(1).astype(F32)
        wt = (jnp.take(w_in[l], perm, axis=1).T * valid).astype(BF16)
        wg_tok = jnp.pad(w_in[l][:, C_NG:C_NG + 24], ((0, 0), (0, LANES - 24))).astype(BF16)
        bgate_col = jnp.pad(b_nsa_gate[l], (0, 8)).reshape(32, 1)
        bgate_row = jnp.pad(b_nsa_gate[l], (0, LANES - 24)).reshape(1, LANES)
        bforget_col = jnp.pad(b_forget[l], (0, 4)).reshape(8, 1)
        w_out_b = w_out[l].astype(BF16)
        wr = jnp.pad(w_router[l], ((0, 0), (0, LANES - N_EXPERTS))).astype(BF16)
        br = jnp.pad(b_router[l], (0, LANES - N_EXPERTS)).reshape(1, LANES)
        moe_w = (_deinterleave(w_up, 512, layer=l), b_up[l][:, None, 0::2], b_up[l][:, None, 1::2],
                 w_down, b_down[l][:, None, :])
        g1, b1 = ln_attn_g[l].reshape(1, -1), ln_attn_b[l].reshape(1, -1)
        g2, b2 = ln_ffn_g[l].reshape(1, -1), ln_ffn_b[l].reshape(1, -1)

        pt, gates = _inproj(hp, wt, wg_tok, bgate_col, bforget_col, bgate_row, tabs_p, 256)
        outs['fox_p'].append(pt[:, R_FOX:R_FOX + 516].reshape(nb, 129, FOX_HEADS, seq).transpose(0, 3, 2, 1))
        outs['nsa_p'].append(pt[:, R_NSA:R_NSA + 512].reshape(nb, 4, NSA_KV_GROUPS, HEAD_DIM, seq)
                             .transpose(0, 4, 1, 2, 3))
        outs['diff_p'].append(pt[:, R_DIFF:R_DIFF + 512].reshape(nb, DIFF_HEADS, 2, HEAD_DIM, seq)
                              .transpose(0, 4, 1, 2, 3))
        outs['win_p'].append(pt[:, R_WIN:R_WIN + 256, seq - NSA_WINDOW:]
                             .reshape(nb, 2, NSA_KV_GROUPS, HEAD_DIM, NSA_WINDOW).transpose(0, 4, 1, 2, 3))
        c = jnp.cumsum(pt[:, R_LOGF:R_LOGF + FOX_HEADS, :], axis=-1)
        o_fox = _fox_attention_t(pt, c, 512)
        o_diff = _diff_attention_t(pt, lam_arr, diff_subln_g[l].reshape(HEAD_DIM, 1), 1.0 - lam_init, 512)
        n_cmp = seq // NSA_BLOCK
        kcv = pt[:, R_NSA:R_NSA + 256].reshape(nb, 2, NSA_KV_GROUPS, HEAD_DIM, n_cmp, NSA_BLOCK)
        xb = kcv.transpose(1, 0, 2, 4, 5, 3).reshape(2, nb * NSA_KV_GROUPS * n_cmp, NSA_BLOCK * HEAD_DIM)
        cmp = _compress(xb, nsa_cmp_pos[l].reshape(2, 1, NSA_BLOCK * HEAD_DIM),
                        nsa_cmp_w[l].reshape(2, NSA_BLOCK * HEAD_DIM, HEAD_DIM), 128)
        cmp = cmp.reshape(2, nb, NSA_KV_GROUPS, n_cmp, HEAD_DIM)
        o_cmp, sel = _nsa_cmp_t(pt, cmp[0], cmp[1].transpose(0, 1, 3, 2), 256)
        o_sel = _nsa_band_t(pt, sel, 256, NSA_SEL_TK, True)
        o_win = _nsa_band_t(pt, None, 256, 256, False)

        xs_pad = _pad_rows(hs, n_dec_pad)[None]
        pt_s, _ = _inproj(xs_pad, wt, wg_tok, bgate_col, bforget_col, bgate_row, tabs_s, n_dec_pad)
        ps = pt_s[0, :, :n_dec].T
        sp = SampleProj(
            fq=ps[:, R_FQ:R_FQ + 256].reshape(n_dec, 1, FOX_HEADS, HEAD_DIM),
            fox_rows=ps[:, R_FOX:R_FOX + 516].reshape(n_dec, 1, 129, FOX_HEADS).transpose(0, 1, 3, 2),
            nq=ps[:, R_NQ:R_NQ + 512].reshape(n_dec, 1, NSA_HEADS, HEAD_DIM),
            ng=ps[:, R_GATE:R_GATE + 24].reshape(n_dec, 1, 3, NSA_HEADS),
            nsa_rows=ps[:, R_NSA:R_NSA + 512].reshape(n_dec, 1, 4, NSA_KV_GROUPS, HEAD_DIM),
            win_rows=ps[:, R_WIN:R_WIN + 256].reshape(n_dec, 1, 2, NSA_KV_GROUPS, HEAD_DIM),
            dq=ps[:, R_DQ:R_DQ + 256].reshape(n_dec, 1, DIFF_HEADS, 2, DIFF_QK_DIM),
            diff_rows=ps[:, R_DIFF:R_DIFF + 512].reshape(n_dec, 1, DIFF_HEADS, 2, HEAD_DIM))
        s_fox = _fox_sample(fox_fm, page_table, l, sp, SAMPLE_PAGES_PER_STEP)
        s_nsa, win_new = _nsa_sample(nsa_fm, win_fm, page_table, l, sp, nsa_cmp_pos[l], nsa_cmp_w[l],
                                     SAMPLE_PAGES_PER_STEP)
        s_diff = _diff_sample(diff_fm, page_table, l, sp, lam, lam_init, diff_subln_g[l], SAMPLE_PAGES_PER_STEP)
        outs['fox_s'].append(sp.fox_rows)
        outs['nsa_s'].append(sp.nsa_rows)
        outs['diff_s'].append(sp.diff_rows)
        outs['win_s'].append(win_new)

        def rows(a_p, a_s):
            return _pad_rows(jnp.concatenate([a_p.reshape(m_p, -1), a_s], 0), m_pad)

        zero_s = jnp.zeros((n_dec, 512), F32)
        h_all = rows(hp, hs)
        h1, logits = _outproj(rows(o_fox, s_fox), rows(o_cmp, s_nsa), rows(o_sel, zero_s), rows(o_win, zero_s),
                              rows(o_diff, s_diff), h_all, w_out_b, g1, b1, wr, br, 512)
        h2 = _moe_block(h1, logits[:, :N_EXPERTS], moe_w, g2, b2, 512, 512, wd_layer=l)
        hp = h2[:m_p].reshape(nb, seq, D_MODEL)
        hs = h2[m_p:m_all]

    st = lambda k: jnp.stack(outs[k], axis=1)
    return (hp, hs.reshape(n_dec, n_new, D_MODEL),
            st('fox_p'), st('fox_s'), st('nsa_p'), st('nsa_s'),
            st('diff_p'), st('diff_s'), st('win_p'), st('win_s'))
```

```python
import collections
import functools
import math

import numpy as np
import jax
import jax.numpy as jnp
from jax import lax
from jax.experimental import pallas as pl
from jax.experimental.pallas import tpu as pltpu

F32 = jnp.float32
BF16 = jnp.bfloat16

D_MODEL = 1024
DEPTH = 2
PAGE_SIZE = 128
HEAD_DIM = 64
FOX_HEADS = 4
NSA_HEADS = 8
NSA_KV_GROUPS = 2
NSA_HPG = NSA_HEADS // NSA_KV_GROUPS
DIFF_HEADS = 4
DIFF_QK_DIM = HEAD_DIM // 2
NSA_BLOCK = 64
BLOCK_SHIFT = 6
NSA_TOP_N = 16
NSA_WINDOW = 512
ROPE_THETA = 500000.0
N_EXPERTS = 32
TOP_K = 4
SWIGLU_LIMIT = 7.0
SWIGLU_ALPHA = 1.702
LN_EPS = 1e-5
RMS_EPS = 1e-5
DEEPNORM_ALPHA = (2 * DEPTH) ** 0.25
NEG = -1e30
FORCE_SCORE = 1e4

SUBLANES = 8
LANES = 128
VMEM_LIMIT_BYTES = 48 * 1024 * 1024
SAMPLE_PAGES_PER_STEP = 16
NSA_SEL_TK = 1024

C_FQ, C_FK, C_FV, C_FF = 0, 256, 512, 768
C_NQ, C_NKC, C_NKW, C_NVW, C_NG = 772, 1284, 1796, 1924, 2052
C_DQ, C_DK, C_DV = 2076, 2332, 2588
N_IN = 2844

R_NQ = 0
R_NSA = 512
R_DIFF = 1024
R_FQ = 1536
R_FK = 1792
R_FV = 2048
R_WIN = 2304
R_DQ = 2560
R_GATE = 2816
R_FOX = 2848
R_LOGF = R_FOX + 512
NP_ROWS = 3368


def _build_perm():
    perm = np.full((NP_ROWS,), -1, np.int64)
    perm[R_NQ:R_NQ + 512] = C_NQ + np.arange(512)
    perm[R_NSA:R_NSA + 512] = C_NKC + np.arange(512)
    for h in range(DIFF_HEADS):
        perm[R_DIFF + h * 128:R_DIFF + h * 128 + 64] = C_DK + h * 64 + np.arange(64)
        perm[R_DIFF + h * 128 + 64:R_DIFF + h * 128 + 128] = C_DV + h * 64 + np.arange(64)
    perm[R_FQ:R_FQ + 256] = C_FQ + np.arange(256)
    perm[R_FK:R_FK + 256] = C_FK + np.arange(256)
    perm[R_FV:R_FV + 256] = C_FV + np.arange(256)
    perm[R_WIN:R_WIN + 256] = C_NKW + np.arange(256)
    perm[R_DQ:R_DQ + 256] = C_DQ + np.arange(256)
    perm[R_GATE:R_GATE + 24] = C_NG + np.arange(24)
    for c in range(64):
        for h in range(FOX_HEADS):
            perm[R_FOX + c * 4 + h] = C_FK + h * 64 + c
            perm[R_FOX + (64 + c) * 4 + h] = C_FV + h * 64 + c
    for h in range(FOX_HEADS):
        perm[R_FOX + 512 + h] = C_FF + h
    return perm


_PERM = _build_perm()
_ROT64_ROWS = tuple([R_NQ + h * 64 for h in range(NSA_HEADS)]
                    + [R_NSA + g * 64 for g in range(NSA_KV_GROUPS)]
                    + [R_NSA + 256 + g * 64 for g in range(NSA_KV_GROUPS)]
                    + [R_WIN + g * 64 for g in range(NSA_KV_GROUPS)])
_ROT32_ROWS = tuple([R_DQ + i * 32 for i in range(2 * DIFF_HEADS)]
                    + [R_DIFF + h * 128 + m * 32 for h in range(DIFF_HEADS) for m in range(2)])


def _cparams(sem):
    return pltpu.CompilerParams(dimension_semantics=sem, vmem_limit_bytes=VMEM_LIMIT_BYTES)


def _inproj_kernel(x_ref, w_ref, bgate_ref, bforget_ref, cn_ref, sn_ref, cd_ref, sd_ref, o_ref):
    x = x_ref[...].astype(BF16)
    o_ref[...] = lax.dot_general(w_ref[...], x, (((1,), (1,)), ((), ())), preferred_element_type=F32)
    cn = cn_ref[...]
    sn = sn_ref[...]
    for r in _ROT64_ROWS:
        x1 = o_ref[r:r + 8, :]
        x2 = o_ref[r + 8:r + 16, :]
        o_ref[r:r + 8, :] = x1 * cn - x2 * sn
        o_ref[r + 8:r + 16, :] = x2 * cn + x1 * sn
    cd = cd_ref[...]
    sd = sd_ref[...]
    for r in _ROT32_ROWS:
        v = o_ref[r:r + 8, :]
        o_ref[r:r + 8, :] = v * cd + pltpu.roll(v, 4, 0) * sd
    z = o_ref[R_GATE:R_GATE + 32, :] + bgate_ref[...]
    o_ref[R_GATE:R_GATE + 32, :] = 1.0 / (1.0 + jnp.exp(-z))
    z = o_ref[R_LOGF:R_LOGF + 8, :] + bforget_ref[...]
    o_ref[R_LOGF:R_LOGF + 8, :] = jnp.minimum(z, 0.0) - jnp.log1p(jnp.exp(-jnp.abs(z)))


def _inproj(x, wt, bgate_col, bforget_col, tabs, tn):
    nb, nt, k = x.shape
    cn, sn, cd, sd = tabs
    const = lambda b, i: (0, 0)
    tab = lambda b, i: (0, i)
    return pl.pallas_call(
        _inproj_kernel,
        grid=(nb, nt // tn),
        in_specs=[
            pl.BlockSpec((None, tn, k), lambda b, i: (b, i, 0)),
            pl.BlockSpec((NP_ROWS, k), const),
            pl.BlockSpec((32, 1), const),
            pl.BlockSpec((8, 1), const),
            pl.BlockSpec((8, tn), tab),
            pl.BlockSpec((8, tn), tab),
            pl.BlockSpec((8, tn), tab),
            pl.BlockSpec((8, tn), tab),
        ],
        out_specs=pl.BlockSpec((None, NP_ROWS, tn), lambda b, i: (b, 0, i)),
        out_shape=jax.ShapeDtypeStruct((nb, NP_ROWS, nt), F32),
        compiler_params=_cparams(("parallel", "parallel")),
        name="inproj",
    )(x, wt, bgate_col, bforget_col, cn, sn, cd, sd)


def _rope_tables(pos):
    inv8 = ROPE_THETA ** (-jnp.arange(8, dtype=F32) / 8)
    ang8 = pos[:, None] * inv8[None, :]
    cn, sn = jnp.cos(ang8).T, jnp.sin(ang8).T
    inv4 = ROPE_THETA ** (-jnp.arange(4, dtype=F32) / 4)
    ang4 = pos[:, None] * inv4[None, :]
    c4, s4 = jnp.cos(ang4).T, jnp.sin(ang4).T
    cd = jnp.concatenate([c4, c4], 0)
    sd = jnp.concatenate([-s4, s4], 0)
    return cn, sn, cd, sd


def _tri_pairs(nq, lo_tiles=None):
    qi, kj = [], []
    for i in range(nq):
        j0 = 0 if lo_tiles is None else max(0, i - lo_tiles)
        for j in range(j0, i + 1):
            qi.append(i)
            kj.append(j)
    first = [1 if (p == 0 or qi[p] != qi[p - 1]) else 0 for p in range(len(qi))]
    return (jnp.asarray(qi, jnp.int32), jnp.asarray(kj, jnp.int32), jnp.asarray(first, jnp.int32))


def _compress_kernel(x_ref, pe_ref, w_ref, o_ref):
    x = (x_ref[...] + pe_ref[...]).astype(BF16)
    o_ref[...] = jnp.dot(x, w_ref[...].astype(BF16), preferred_element_type=F32)


def _compress(xb, pe, w, tm):
    _, m, kk = xb.shape
    return pl.pallas_call(
        _compress_kernel,
        grid=(2, m // tm),
        in_specs=[
            pl.BlockSpec((None, tm, kk), lambda t, i: (t, i, 0)),
            pl.BlockSpec((None, 1, kk), lambda t, i: (t, 0, 0)),
            pl.BlockSpec((None, kk, HEAD_DIM), lambda t, i: (t, 0, 0)),
        ],
        out_specs=pl.BlockSpec((None, tm, HEAD_DIM), lambda t, i: (t, i, 0)),
        out_shape=jax.ShapeDtypeStruct((2, m, HEAD_DIM), F32),
        compiler_params=_cparams(("parallel", "parallel")),
        name="nsa_compress",
    )(xb, pe, w)


def _flash_step(s, v_t, m_ref, l_ref, acc_ref):
    m_prev = m_ref[...]
    m_new = jnp.maximum(m_prev, jnp.max(s, axis=0, keepdims=True))
    alpha = jnp.exp(m_prev - m_new)
    p = jnp.exp(s - m_new)
    l_ref[...] = alpha * l_ref[...] + jnp.sum(p, axis=0, keepdims=True)
    m_ref[...] = m_new
    acc_ref[...] = alpha * acc_ref[...] + jnp.dot(v_t, p.astype(BF16), preferred_element_type=F32)


def _kq(k_t, q_t):
    return lax.dot_general(k_t, q_t, (((0,), (0,)), ((), ())), preferred_element_type=F32)


def _causal_keep(qi, kj, tq, tk):
    key = kj * tk + lax.broadcasted_iota(jnp.int32, (tk, tq), 0)
    qry = qi * tq + lax.broadcasted_iota(jnp.int32, (tk, tq), 1)
    return key, qry


def _split3(x):
    x1 = x.astype(BF16)
    r1 = x - x1.astype(F32)
    x2 = r1.astype(BF16)
    x3 = (r1 - x2.astype(F32)).astype(BF16)
    return x1, x2, x3


def _fox_kernel_t(qi_ref, kj_ref, first_ref, q_ref, k_ref, v_ref, c_ref, o_ref, q_s, m_s, l_s, acc_s, *, tq, tk):
    p_id = pl.program_id(1)
    qi = qi_ref[p_id]
    kj = kj_ref[p_id]

    @pl.when(first_ref[p_id] == 1)
    def _():
        extra = jnp.where(lax.broadcasted_iota(jnp.int32, (16, tq), 0) < 3, -1.0, 0.0).astype(BF16)
        for h in range(FOX_HEADS):
            q_s[h, 0:64, :] = (q_ref[h * 64:(h + 1) * 64, :] * (HEAD_DIM ** -0.5)).astype(BF16)
            q_s[h, 64:80, :] = extra
        m_s[...] = jnp.full(m_s.shape, NEG, F32)
        l_s[...] = jnp.zeros(l_s.shape, F32)
        acc_s[...] = jnp.zeros(acc_s.shape, F32)

    def tile(masked):
        sub16 = lax.broadcasted_iota(jnp.int32, (16, tk), 0)
        if masked:
            key, qry = _causal_keep(qi, kj, tq, tk)
            mask = key <= qry
        for h in range(FOX_HEADS):
            c1, c2, c3 = [jnp.broadcast_to(t.astype(F32), (16, tk)) for t in _split3(c_ref[h:h + 1, :])]
            extra = jnp.where(sub16 == 0, c1, jnp.where(sub16 == 1, c2, jnp.where(sub16 == 2, c3, 0.0)))
            k_aug = jnp.concatenate([k_ref[h * 64:(h + 1) * 64, :].astype(BF16), extra.astype(BF16)], axis=0)
            s = _kq(k_aug, q_s[h])
            if masked:
                s = jnp.where(mask, s, NEG)
            _flash_step(s, v_ref[h * 64:(h + 1) * 64, :].astype(BF16), m_s.at[h], l_s.at[h], acc_s.at[h])

    pl.when(kj == qi)(lambda: tile(True))
    pl.when(kj != qi)(lambda: tile(False))

    @pl.when(kj == qi)
    def _():
        for h in range(FOX_HEADS):
            o_ref[:, h * 64:(h + 1) * 64] = (acc_s[h] / l_s[h]).T


def _fox_attention_t(pt, c, tq):
    nb, _, nt = pt.shape
    tk = tq
    qi, kj, first = _tri_pairs(nt // tq)
    blk = 256
    grid_spec = pltpu.PrefetchScalarGridSpec(
        num_scalar_prefetch=3,
        grid=(nb, qi.shape[0]),
        in_specs=[
            pl.BlockSpec((None, blk, tq), lambda b, p, qi, kj, f: (b, R_FQ // blk, qi[p])),
            pl.BlockSpec((None, blk, tk), lambda b, p, qi, kj, f: (b, R_FK // blk, kj[p])),
            pl.BlockSpec((None, blk, tk), lambda b, p, qi, kj, f: (b, R_FV // blk, kj[p])),
            pl.BlockSpec((None, FOX_HEADS, tk), lambda b, p, qi, kj, f: (b, 0, kj[p])),
        ],
        out_specs=pl.BlockSpec((None, tq, blk), lambda b, p, qi, kj, f: (b, qi[p], 0)),
        scratch_shapes=[
            pltpu.VMEM((FOX_HEADS, HEAD_DIM + 16, tq), BF16),
            pltpu.VMEM((FOX_HEADS, 1, tq), F32),
            pltpu.VMEM((FOX_HEADS, 1, tq), F32),
            pltpu.VMEM((FOX_HEADS, HEAD_DIM, tq), F32),
        ],
    )
    return pl.pallas_call(
        functools.partial(_fox_kernel_t, tq=tq, tk=tk),
        grid_spec=grid_spec,
        out_shape=jax.ShapeDtypeStruct((nb, nt, blk), F32),
        compiler_params=_cparams(("parallel", "arbitrary")),
        name="fox_attn",
    )(qi, kj, first, pt, pt, pt, c)


def _diff_kernel_t(qi_ref, kj_ref, first_ref, lam_ref, q_ref, kv_ref, g_ref, o_ref, q_s, m_s, l_s, acc_s,
                   *, tq, tk, out_scale):
    p_id = pl.program_id(1)
    qi = qi_ref[p_id]
    kj = kj_ref[p_id]
    scale = DIFF_QK_DIM ** -0.5

    @pl.when(first_ref[p_id] == 1)
    def _():
        q_s[...] = q_ref[...].astype(BF16)
        m_s[...] = jnp.full(m_s.shape, NEG, F32)
        l_s[...] = jnp.zeros(l_s.shape, F32)
        acc_s[...] = jnp.zeros(acc_s.shape, F32)

    def tile(masked):
        if masked:
            key, qry = _causal_keep(qi, kj, tq, tk)
            mask = key <= qry
        for h in range(DIFF_HEADS):
            v_t = kv_ref[h * 128 + 64:h * 128 + 128, :].astype(BF16)
            for m in range(2):
                i = 2 * h + m
                k_t = kv_ref[h * 128 + m * 32:h * 128 + (m + 1) * 32, :].astype(BF16)
                s = _kq(k_t, q_s[i * 32:(i + 1) * 32, :]) * scale
                if masked:
                    s = jnp.where(mask, s, NEG)
                _flash_step(s, v_t, m_s.at[i], l_s.at[i], acc_s.at[i])

    pl.when(kj == qi)(lambda: tile(True))
    pl.when(kj != qi)(lambda: tile(False))

    @pl.when(kj == qi)
    def _():
        lam = lam_ref[0]
        for h in range(DIFF_HEADS):
            o = acc_s[2 * h] / l_s[2 * h] - lam * (acc_s[2 * h + 1] / l_s[2 * h + 1])
            o = o * lax.rsqrt(jnp.mean(o * o, axis=0, keepdims=True) + RMS_EPS) * g_ref[...] * out_scale
            o_ref[:, h * 64:(h + 1) * 64] = o.T


def _diff_attention_t(pt, lam, g_col, out_scale, tq):
    nb, _, nt = pt.shape
    tk = tq
    qi, kj, first = _tri_pairs(nt // tq)
    grid_spec = pltpu.PrefetchScalarGridSpec(
        num_scalar_prefetch=4,
        grid=(nb, qi.shape[0]),
        in_specs=[
            pl.BlockSpec((None, 256, tq), lambda b, p, qi, kj, f, lam: (b, R_DQ // 256, qi[p])),
            pl.BlockSpec((None, 512, tk), lambda b, p, qi, kj, f, lam: (b, R_DIFF // 512, kj[p])),
            pl.BlockSpec((HEAD_DIM, 1), lambda b, p, qi, kj, f, lam: (0, 0)),
        ],
        out_specs=pl.BlockSpec((None, tq, 256), lambda b, p, qi, kj, f, lam: (b, qi[p], 0)),
        scratch_shapes=[
            pltpu.VMEM((256, tq), BF16),
            pltpu.VMEM((2 * DIFF_HEADS, 1, tq), F32),
            pltpu.VMEM((2 * DIFF_HEADS, 1, tq), F32),
            pltpu.VMEM((2 * DIFF_HEADS, HEAD_DIM, tq), F32),
        ],
    )
    return pl.pallas_call(
        functools.partial(_diff_kernel_t, tq=tq, tk=tk, out_scale=out_scale),
        grid_spec=grid_spec,
        out_shape=jax.ShapeDtypeStruct((nb, nt, 256), F32),
        compiler_params=_cparams(("parallel", "arbitrary")),
        name="diff_attn",
    )(qi, kj, first, lam, pt, pt, g_col)


def _stack_heads(q_ref, tq):
    return jnp.concatenate([q_ref[h * 64:(h + 1) * 64, :] for h in range(NSA_HPG)], axis=1)


def _select_mask_t(imp, qpos, n_blk):
    jb = lax.broadcasted_iota(jnp.int32, imp.shape, 0)
    cur = qpos >> BLOCK_SHIFT
    forced = (jb == 0) | (jb == cur) | (jb == cur - 1)
    score = jnp.where(jb <= cur, jnp.where(forced, FORCE_SCORE, imp), NEG)
    rank = jnp.zeros(imp.shape, F32)
    for j in range(n_blk):
        row = score[j:j + 1, :]
        ahead = (row > score) | ((row == score) & (jb > j))
        rank = rank + jnp.where(ahead, 1.0, 0.0)
    return jnp.where(rank < float(NSA_TOP_N), 1.0, 0.0)


def _nsa_cmp_kernel_t(q_ref, kc_ref, vct_ref, gate_ref, o_ref, sel_ref, *, tq, n_cmp):
    qi = pl.program_id(2)
    g = pl.program_id(1)
    q_t = (_stack_heads(q_ref, tq) * (HEAD_DIM ** -0.5)).astype(BF16)
    lc = jnp.dot(kc_ref[...].astype(BF16), q_t, preferred_element_type=F32)
    qpos = qi * tq + lax.broadcasted_iota(jnp.int32, (1, tq), 1)
    blk = lax.broadcasted_iota(jnp.int32, (n_cmp, tq), 0)
    keep = jnp.where(blk < ((qpos + 1) >> BLOCK_SHIFT), 1.0, 0.0)
    keep4 = jnp.concatenate([keep] * NSA_HPG, axis=1)
    lc = jnp.where(keep4 > 0.5, lc, NEG)
    p = jnp.exp(lc - jnp.max(lc, axis=0, keepdims=True)) * keep4
    pc = p / jnp.maximum(jnp.sum(p, axis=0, keepdims=True), 1e-30)
    oc = jnp.dot(vct_ref[...].astype(BF16), pc.astype(BF16), preferred_element_type=F32)
    imp = pc[:, 0:tq]
    for h in range(1, NSA_HPG):
        imp = imp + pc[:, h * tq:(h + 1) * tq]
    sel_ref[...] = _select_mask_t(imp, qpos, n_cmp)
    gates = gate_ref[...]
    for h in range(NSA_HPG):
        grow = jnp.where(g == 0, gates[h:h + 1, :], gates[NSA_HPG + h:NSA_HPG + h + 1, :])
        o_ref[:, h * 64:(h + 1) * 64] = (oc[:, h * tq:(h + 1) * tq] * grow).T


def _nsa_cmp_t(pt, kc, vct, tq):
    nb, _, nt = pt.shape
    n_cmp = kc.shape[2]
    return pl.pallas_call(
        functools.partial(_nsa_cmp_kernel_t, tq=tq, n_cmp=n_cmp),
        grid=(nb, NSA_KV_GROUPS, nt // tq),
        in_specs=[
            pl.BlockSpec((None, 256, tq), lambda b, g, i: (b, R_NQ // 256 + g, i)),
            pl.BlockSpec((None, None, n_cmp, HEAD_DIM), lambda b, g, i: (b, g, 0, 0)),
            pl.BlockSpec((None, None, HEAD_DIM, n_cmp), lambda b, g, i: (b, g, 0, 0)),
            pl.BlockSpec((None, 32, tq), lambda b, g, i: (b, R_GATE // 32, i)),
        ],
        out_specs=[
            pl.BlockSpec((None, tq, 256), lambda b, g, i: (b, i, g)),
            pl.BlockSpec((None, None, n_cmp, tq), lambda b, g, i: (b, g, 0, i)),
        ],
        out_shape=[jax.ShapeDtypeStruct((nb, nt, 512), F32),
                   jax.ShapeDtypeStruct((nb, NSA_KV_GROUPS, n_cmp, nt), F32)],
        compiler_params=_cparams(("parallel", "parallel", "parallel")),
        name="nsa_cmp",
    )(pt, kc, vct, pt)


def _nsa_band_kernel_t(qi_ref, kj_ref, first_ref, last_ref, q_ref, k_ref, v_ref, gate_ref, *rest,
                       tq, tk, selected, gate_base):
    if selected:
        sel_ref, o_ref, q_s, m_s, l_s, acc_s = rest
    else:
        o_ref, q_s, m_s, l_s, acc_s = rest
    p_id = pl.program_id(2)
    g = pl.program_id(1)
    qi = qi_ref[p_id]
    kj = kj_ref[p_id]

    @pl.when(first_ref[p_id] == 1)
    def _():
        q_s[...] = (_stack_heads(q_ref, tq) * (HEAD_DIM ** -0.5)).astype(BF16)
        m_s[...] = jnp.full(m_s.shape, NEG, F32)
        l_s[...] = jnp.zeros(l_s.shape, F32)
        acc_s[...] = jnp.zeros(acc_s.shape, F32)

    def tile(masked):
        s = _kq(k_ref[...].astype(BF16), q_s[...])
        if masked:
            key, qry = _causal_keep(qi, kj, tq, tk)
            mask = key <= qry
            if selected:
                n_blk = sel_ref.shape[0]
                blk_of_key = kj * (tk // NSA_BLOCK) + (lax.broadcasted_iota(jnp.int32, (tk, n_blk), 0) >> BLOCK_SHIFT)
                expand = jnp.where(lax.broadcasted_iota(jnp.int32, (tk, n_blk), 1) == blk_of_key, 1.0, 0.0)
                picked = jnp.dot(expand.astype(BF16), sel_ref[...].astype(BF16), preferred_element_type=F32)
                mask = mask & (picked > 0.5)
            else:
                mask = mask & (key > qry - NSA_WINDOW)
            keep = jnp.where(mask, 1.0, 0.0)
            s = jnp.where(jnp.concatenate([keep] * NSA_HPG, axis=1) > 0.5, s, NEG)
        _flash_step(s, v_ref[...].astype(BF16), m_s, l_s, acc_s)

    if selected:
        tile(True)
    else:
        full = ((kj + 1) * tk - 1 <= qi * tq) & (kj * tk > (qi + 1) * tq - 1 - NSA_WINDOW)
        pl.when(full)(lambda: tile(False))
        pl.when(jnp.logical_not(full))(lambda: tile(True))

    @pl.when(last_ref[p_id] == 1)
    def _():
        gates = gate_ref[...]
        o = acc_s[...] / l_s[...]
        for h in range(NSA_HPG):
            r0 = gate_base + h
            grow = jnp.where(g == 0, gates[r0:r0 + 1, :], gates[r0 + NSA_HPG:r0 + NSA_HPG + 1, :])
            o_ref[:, h * 64:(h + 1) * 64] = (o[:, h * tq:(h + 1) * tq] * grow).T


def _band_pairs(nt, tq, tk, window):
    qi, kj = [], []
    for i in range(nt // tq):
        lo_key = 0 if window is None else max(0, i * tq - (window - 1))
        for j in range(lo_key // tk, ((i + 1) * tq - 1) // tk + 1):
            qi.append(i)
            kj.append(j)
    first = [1 if (p == 0 or qi[p] != qi[p - 1]) else 0 for p in range(len(qi))]
    last = first[1:] + [1]
    return tuple(jnp.asarray(a, jnp.int32) for a in (qi, kj, first, last))


def _nsa_band_t(pt, sel, tq, tk, selected):
    nb, _, nt = pt.shape
    qi, kj, first, last = _band_pairs(nt, tq, tk, None if selected else NSA_WINDOW)
    if selected:
        rk, rv, gate_base = R_NSA + 256, R_NSA + 384, 8
    else:
        rk, rv, gate_base = R_WIN, R_WIN + 128, 16
    idx = lambda f: (lambda b, g, p, qi, kj, fi, la: f(b, g, p, qi, kj))
    in_specs = [
        pl.BlockSpec((None, 256, tq), idx(lambda b, g, p, qi, kj: (b, R_NQ // 256 + g, qi[p]))),
        pl.BlockSpec((None, HEAD_DIM, tk), idx(lambda b, g, p, qi, kj: (b, rk // 64 + g, kj[p]))),
        pl.BlockSpec((None, HEAD_DIM, tk), idx(lambda b, g, p, qi, kj: (b, rv // 64 + g, kj[p]))),
        pl.BlockSpec((None, 32, tq), idx(lambda b, g, p, qi, kj: (b, R_GATE // 32, qi[p]))),
    ]
    args = [pt, pt, pt, pt]
    if selected:
        in_specs.append(pl.BlockSpec((None, None, sel.shape[2], tq),
                                     idx(lambda b, g, p, qi, kj: (b, g, 0, qi[p]))))
        args.append(sel)
    grid_spec = pltpu.PrefetchScalarGridSpec(
        num_scalar_prefetch=4,
        grid=(nb, NSA_KV_GROUPS, qi.shape[0]),
        in_specs=in_specs,
        out_specs=pl.BlockSpec((None, tq, 256), idx(lambda b, g, p, qi, kj: (b, qi[p], g))),
        scratch_shapes=[
            pltpu.VMEM((HEAD_DIM, NSA_HPG * tq), BF16),
            pltpu.VMEM((1, NSA_HPG * tq), F32),
            pltpu.VMEM((1, NSA_HPG * tq), F32),
            pltpu.VMEM((HEAD_DIM, NSA_HPG * tq), F32),
        ],
    )
    return pl.pallas_call(
        functools.partial(_nsa_band_kernel_t, tq=tq, tk=tk, selected=selected, gate_base=gate_base),
        grid_spec=grid_spec,
        out_shape=jax.ShapeDtypeStruct((nb, nt, 512), F32),
        compiler_params=_cparams(("parallel", "parallel", "arbitrary")),
        name="nsa_sel" if selected else "nsa_win",
    )(qi, kj, first, last, *args)


def _layer_norm_rows(x, g, b):
    mu = jnp.mean(x, axis=-1, keepdims=True)
    xc = x - mu
    var = jnp.mean(xc * xc, axis=-1, keepdims=True)
    return xc * lax.rsqrt(var + LN_EPS) * g + b


def _outproj_kernel(fox_ref, n1_ref, n2_ref, n3_ref, diff_ref, h_ref, w_ref, g_ref, b_ref, wr_ref, br_ref,
                    o_ref, lg_ref):
    nsa = (n1_ref[...] + n2_ref[...] + n3_ref[...]).astype(BF16)
    y = jnp.dot(fox_ref[...].astype(BF16), w_ref[0:256, :], preferred_element_type=F32)
    y = y + jnp.dot(nsa, w_ref[256:768, :], preferred_element_type=F32)
    y = y + jnp.dot(diff_ref[...].astype(BF16), w_ref[768:1024, :], preferred_element_type=F32)
    h1 = _layer_norm_rows(DEEPNORM_ALPHA * h_ref[...] + y, g_ref[...], b_ref[...])
    o_ref[...] = h1
    lg_ref[...] = jnp.dot(h1.astype(BF16), wr_ref[...], preferred_element_type=F32) + br_ref[...]


def _outproj(fox, n1, n2, n3, diff, h, w, g, b, wr, br, tm):
    m = h.shape[0]
    row = lambda i: (i, 0)
    const = lambda i: (0, 0)
    return pl.pallas_call(
        _outproj_kernel,
        grid=(m // tm,),
        in_specs=[
            pl.BlockSpec((tm, 256), row), pl.BlockSpec((tm, 512), row), pl.BlockSpec((tm, 512), row),
            pl.BlockSpec((tm, 512), row), pl.BlockSpec((tm, 256), row), pl.BlockSpec((tm, D_MODEL), row),
            pl.BlockSpec((D_MODEL, D_MODEL), const), pl.BlockSpec((1, D_MODEL), const),
            pl.BlockSpec((1, D_MODEL), const), pl.BlockSpec((D_MODEL, LANES), const),
            pl.BlockSpec((1, LANES), const),
        ],
        out_specs=[pl.BlockSpec((tm, D_MODEL), row), pl.BlockSpec((tm, LANES), row)],
        out_shape=[jax.ShapeDtypeStruct((m, D_MODEL), F32), jax.ShapeDtypeStruct((m, LANES), F32)],
        compiler_params=_cparams(("parallel",)),
        name="outproj_ln",
    )(fox, n1, n2, n3, diff, h, w, g, b, wr, br)


PERM_CHUNK = 256


def _deinterleave_kernel(w_ref, o_ref):
    half = w_ref.shape[1] // 2
    hc = PERM_CHUNK // 2
    src = lax.broadcasted_iota(jnp.int32, (PERM_CHUNK, PERM_CHUNK), 0)
    dst = lax.broadcasted_iota(jnp.int32, (PERM_CHUNK, PERM_CHUNK), 1)
    want = jnp.where(dst < hc, 2 * dst, 2 * (dst - hc) + 1)
    perm = jnp.where(src == want, 1.0, 0.0).astype(BF16)
    for c in range(w_ref.shape[1] // PERM_CHUNK):
        y = jnp.dot(w_ref[:, c * PERM_CHUNK:(c + 1) * PERM_CHUNK].astype(BF16), perm, preferred_element_type=F32)
        o_ref[:, c * hc:(c + 1) * hc] = y[:, :hc].astype(BF16)
        o_ref[:, half + c * hc:half + (c + 1) * hc] = y[:, hc:].astype(BF16)


def _deinterleave(w, tr, layer=None):
    ne, k, n2 = w.shape[-3:]
    if layer is None:
        in_spec = pl.BlockSpec((None, tr, n2), lambda e, i: (e, i, 0))
    else:
        in_spec = pl.BlockSpec((None, None, tr, n2), lambda e, i: (layer, e, i, 0))
    return pl.pallas_call(
        _deinterleave_kernel,
        grid=(ne, k // tr),
        in_specs=[in_spec],
        out_specs=pl.BlockSpec((None, tr, n2), lambda e, i: (e, i, 0)),
        out_shape=jax.ShapeDtypeStruct((ne, k, n2), BF16),
        compiler_params=_cparams(("parallel", "parallel")),
        name="moe_deinterleave",
    )(w)


def _moe_kernel(te_ref, nv_ref, x_ref, wu_ref, bg_ref, bl_ref, wd_ref, bd_ref, o_ref):
    i = pl.program_id(0)

    @pl.when(i < nv_ref[0])
    def _():
        hcat = jnp.dot(x_ref[...].astype(BF16), wu_ref[...], preferred_element_type=F32)
        hg = hcat[:, :D_MODEL] + bg_ref[...]
        hl = hcat[:, D_MODEL:] + bl_ref[...]
        glu = jnp.minimum(hg, SWIGLU_LIMIT)
        lin = jnp.clip(hl, -SWIGLU_LIMIT, SWIGLU_LIMIT)
        act = glu * (1.0 / (1.0 + jnp.exp(-SWIGLU_ALPHA * glu))) * (lin + 1.0)
        o_ref[...] = jnp.dot(act.astype(BF16), wd_ref[...].astype(BF16), preferred_element_type=F32) + bd_ref[...]

    @pl.when(i >= nv_ref[0])
    def _():
        o_ref[...] = jnp.zeros(o_ref.shape, F32)


def _moe_experts(tile_e, n_valid, xs, wu, bg, bl, wd, bd, tm, wd_layer=None):
    n_rows = xs.shape[0]
    ex = lambda i, te, nv: (te[i], 0, 0)
    if wd_layer is None:
        wd_spec = pl.BlockSpec((None, D_MODEL, D_MODEL), ex)
    else:
        wd_spec = pl.BlockSpec((None, None, D_MODEL, D_MODEL), lambda i, te, nv: (wd_layer, te[i], 0, 0))
    grid_spec = pltpu.PrefetchScalarGridSpec(
        num_scalar_prefetch=2,
        grid=(n_rows // tm,),
        in_specs=[
            pl.BlockSpec((tm, D_MODEL), lambda i, te, nv: (i, 0)),
            pl.BlockSpec((None, D_MODEL, 2 * D_MODEL), ex),
            pl.BlockSpec((None, 1, D_MODEL), ex),
            pl.BlockSpec((None, 1, D_MODEL), ex),
            wd_spec,
            pl.BlockSpec((None, 1, D_MODEL), ex),
        ],
        out_specs=pl.BlockSpec((tm, D_MODEL), lambda i, te, nv: (i, 0)),
    )
    return pl.pallas_call(
        _moe_kernel,
        grid_spec=grid_spec,
        out_shape=jax.ShapeDtypeStruct((n_rows, D_MODEL), F32),
        compiler_params=_cparams(("arbitrary",)),
        name="moe_experts",
    )(tile_e, n_valid, xs, wu, bg, bl, wd, bd)


def _combine_kernel(h_ref, y_ref, gate_ref, g_ref, b_ref, o_ref):
    gates = gate_ref[...]
    f = y_ref[0] * gates[:, 0:1]
    for k in range(1, TOP_K):
        f = f + y_ref[k] * gates[:, k:k + 1]
    o_ref[...] = _layer_norm_rows(DEEPNORM_ALPHA * h_ref[...] + f, g_ref[...], b_ref[...])


def _combine_ln(h1, yg, gates, g, b, tm):
    m = h1.shape[0]
    row = lambda i: (i, 0)
    const = lambda i: (0, 0)
    return pl.pallas_call(
        _combine_kernel,
        grid=(m // tm,),
        in_specs=[
            pl.BlockSpec((tm, D_MODEL), row),
            pl.BlockSpec((TOP_K, tm, D_MODEL), lambda i: (0, i, 0)),
            pl.BlockSpec((tm, TOP_K), row),
            pl.BlockSpec((1, D_MODEL), const), pl.BlockSpec((1, D_MODEL), const),
        ],
        out_specs=pl.BlockSpec((tm, D_MODEL), row),
        out_shape=jax.ShapeDtypeStruct((m, D_MODEL), F32),
        compiler_params=_cparams(("parallel",)),
        name="moe_combine_ln",
    )(h1, yg, gates, g, b)


def _moe_block(h1, logits, wts, ln_g, ln_b, tm_moe, tm_row, wd_layer=None):
    wu, bg, bl, wd, bd = wts
    m = h1.shape[0]
    top_val, top_idx = lax.top_k(logits, TOP_K)
    gate = jax.nn.softmax(top_val, axis=-1)
    n = m * TOP_K
    flat_e = top_idx.reshape(n).astype(jnp.int32)
    iota_n = jnp.arange(n, dtype=jnp.int32)
    sorted_e, order = lax.sort_key_val(flat_e, iota_n)
    counts = jnp.sum(flat_e[:, None] == jnp.arange(N_EXPERTS, dtype=jnp.int32)[None, :], axis=0, dtype=jnp.int32)
    padded = (counts + tm_moe - 1) // tm_moe * tm_moe
    starts = jnp.cumsum(counts) - counts
    pends = jnp.cumsum(padded)
    gstart = pends - padded
    dest = gstart[sorted_e] + iota_n - starts[sorted_e]
    n_rows = -(-n // tm_moe) * tm_moe + N_EXPERTS * tm_moe
    n_tiles = n_rows // tm_moe
    tile_start = jnp.arange(n_tiles, dtype=jnp.int32) * tm_moe
    tile_e = jnp.minimum(jnp.sum(pends[None, :] <= tile_start[:, None], axis=1, dtype=jnp.int32), N_EXPERTS - 1)
    n_valid = (pends[-1] // tm_moe).astype(jnp.int32).reshape(1)
    k_in_group = (jnp.arange(n_rows, dtype=jnp.int32).reshape(n_tiles, tm_moe) - gstart[tile_e][:, None])
    src = jnp.clip(starts[tile_e][:, None] + k_in_group, 0, n - 1)
    row_tok = jnp.where(k_in_group < counts[tile_e][:, None], (order // TOP_K)[src], m).reshape(n_rows)
    _, pos = lax.sort_key_val(order, dest)
    pos = pos.reshape(m, TOP_K)
    xs = jnp.concatenate([h1, jnp.zeros((1, D_MODEL), F32)], 0)[row_tok]
    out = _moe_experts(tile_e, n_valid, xs, wu, bg, bl, wd, bd, tm_moe, wd_layer)
    yg = out[pos.T]
    return _combine_ln(h1, yg, gate, ln_g, ln_b, tm_row)


def _tree_sum(xs):
    xs = list(xs)
    while len(xs) > 1:
        xs = [xs[i] + xs[i + 1] for i in range(0, len(xs) - 1, 2)] + ([xs[-1]] if len(xs) % 2 else [])
    return xs[0]


def _sublane_allsum(x):
    x = x + pltpu.roll(x, 4, 0)
    x = x + pltpu.roll(x, 2, 0)
    return x + pltpu.roll(x, 1, 0)


def _stream_update(logits, m_ref, l_ref, valid=None):
    m_prev = m_ref[...]
    m_new = jnp.maximum(m_prev, logits)
    alpha = jnp.exp(m_prev - m_new)
    p = jnp.exp(logits - m_new)
    if valid is not None:
        p = jnp.where(valid, p, 0.0)
    l_ref[...] = l_ref[...] * alpha + p
    m_ref[...] = m_new
    return alpha, p


def _lane_merge(m_ref, l_ref):
    m = m_ref[...]
    w = jnp.exp(m - jnp.max(m, axis=1, keepdims=True))
    tot = jnp.sum(l_ref[...] * w, axis=1, keepdims=True)
    return w, 1.0 / tot


FOX_WIN = 520
FOX_VROW0 = 256


def _fox_sample_kernel(pt_ref, *refs, kpages, r0):
    pages = refs[:2 * kpages]
    qc_ref, new_ref, o_ref, qcb_s, m_s, l_s, s_s, acc_s, scan_s = refs[2 * kpages:]
    j = pl.program_id(1)
    sub = lax.broadcasted_iota(jnp.int32, (SUBLANES, LANES), 0)
    lane = lax.broadcasted_iota(jnp.int32, (SUBLANES, LANES), 1)
    lf_rows = (sub >= r0) & (sub < r0 + FOX_HEADS)
    n_kv = FOX_VROW0 // SUBLANES + 1

    def page(main, tail, valid):
        t = _tree_sum([main(i) * qcb_s[8 * i:8 * i + 8, :] for i in range(n_kv)])
        qk = t + pltpu.roll(t, 4, 0)
        lf = jnp.where(lf_rows, tail, 0.0)
        lf = lf + pltpu.roll(lf, 4, 0)
        if valid is not None:
            lf = jnp.where(valid, lf, 0.0)
        parts = jnp.concatenate([t.astype(F32) for t in _split3(lf)], axis=0).astype(BF16)
        scan = jnp.dot(parts, scan_s[...], preferred_element_type=F32)
        scan = scan[0:8, :] + scan[8:16, :] + scan[16:24, :]
        later, total = scan[:, 0:LANES], scan[:, LANES:2 * LANES]
        logits = qk + s_s[...] + later
        if valid is not None:
            logits = jnp.where(valid, logits, NEG)
        alpha, p = _stream_update(logits, m_s, l_s, valid)
        n_main = (FOX_WIN - SUBLANES - FOX_VROW0) // SUBLANES
        for i in range(n_main):
            r = 8 * i
            acc_s[r:r + 8, :] = acc_s[r:r + 8, :] * alpha + main(FOX_VROW0 // SUBLANES + i) * p
        r = 8 * n_main
        acc_s[r:r + 8, :] = acc_s[r:r + 8, :] * alpha + tail * p
        s_s[...] = s_s[...] + total

    @pl.when(j == 0)
    def _():
        src = lax.broadcasted_iota(jnp.int32, (LANES, 2 * LANES), 0)
        dst = lax.broadcasted_iota(jnp.int32, (LANES, 2 * LANES), 1)
        scan_s[...] = jnp.where((dst >= LANES) | (src > dst), 1.0, 0.0).astype(BF16)
        qcb_s[...] = jnp.broadcast_to(qc_ref[...], qcb_s.shape)
        m_s[...] = jnp.full(m_s.shape, NEG, F32)
        l_s[...] = jnp.zeros(l_s.shape, F32)
        s_s[...] = jnp.zeros(s_s.shape, F32)
        acc_s[...] = jnp.zeros(acc_s.shape, F32)
        page(lambda i: jnp.broadcast_to(new_ref[8 * i:8 * i + 8, :], (SUBLANES, LANES)),
             jnp.broadcast_to(new_ref[FOX_WIN - 8:FOX_WIN, :], (SUBLANES, LANES)), lane == 0)

    for k in range(kpages):
        main_ref, tail_ref = pages[2 * k], pages[2 * k + 1]
        page(lambda i, ref=main_ref: ref[8 * i:8 * i + 8, :], tail_ref[...], None)

    @pl.when(j == pl.num_programs(1) - 1)
    def _():
        w, inv = _lane_merge(m_s, l_s)
        for i in range((FOX_WIN - FOX_VROW0) // SUBLANES):
            r = 8 * i
            o_ref[r:r + 8, :] = jnp.sum(acc_s[r:r + 8, :] * w, axis=1, keepdims=True) * inv


def _fox_sample_attn(cache_fm, page_table, layer, qcol, newcol, kpages):
    nb, n_pages = page_table.shape
    r0 = 4 * layer
    n_chunks = n_pages // kpages

    def page_idx(b, j, pt, k):
        return pt[b, n_pages - 1 - (j * kpages + k)]

    in_specs = []
    for k in range(kpages):
        in_specs.append(pl.BlockSpec((None, 512, LANES), lambda b, j, pt, k=k: (page_idx(b, j, pt, k), layer, 0)))
        in_specs.append(pl.BlockSpec((None, 8, LANES), lambda b, j, pt, k=k: (page_idx(b, j, pt, k), 64 * (layer + 1), 0)))
    in_specs += [pl.BlockSpec((None, FOX_WIN, 1), lambda b, j, pt: (b, 0, 0)),
                 pl.BlockSpec((None, FOX_WIN, 1), lambda b, j, pt: (b, 0, 0))]
    n_out = FOX_WIN - FOX_VROW0
    grid_spec = pltpu.PrefetchScalarGridSpec(
        num_scalar_prefetch=1,
        grid=(nb, n_chunks),
        in_specs=in_specs,
        out_specs=pl.BlockSpec((None, n_out, 1), lambda b, j, pt: (b, 0, 0)),
        scratch_shapes=[
            pltpu.VMEM((FOX_WIN, LANES), F32),
            pltpu.VMEM((SUBLANES, LANES), F32), pltpu.VMEM((SUBLANES, LANES), F32), pltpu.VMEM((SUBLANES, LANES), F32),
            pltpu.VMEM((n_out, LANES), F32),
            pltpu.VMEM((LANES, 2 * LANES), BF16),
        ],
    )
    return pl.pallas_call(
        functools.partial(_fox_sample_kernel, kpages=kpages, r0=r0),
        grid_spec=grid_spec,
        out_shape=jax.ShapeDtypeStruct((nb, n_out, 1), F32),
        compiler_params=_cparams(("parallel", "arbitrary")),
        name="fox_sample",
    )(page_table, *([cache_fm] * (2 * kpages)), qcol, newcol)


def _bcast_row(x, s):
    return jnp.broadcast_to(x[s:s + 1, :], x.shape)


def _pack_streams(ts):
    sub = lax.broadcasted_iota(jnp.int32, ts[0].shape, 0)
    out = ts[0]
    for s in range(1, len(ts)):
        out = jnp.where(sub == s, ts[s], out)
    return out


def _kv_page_update(x, qcb_s, k_rows, v_rows, m_s, l_s, acc_s, valid):
    ts = []
    for s, (g0, cnt, c0) in enumerate(k_rows):
        t = _tree_sum([x(g0 + i) * qcb_s[c0 + 8 * i:c0 + 8 * i + 8, :] for i in range(cnt)])
        ts.append(_sublane_allsum(t))
    logits = _pack_streams(ts)
    if valid is not None:
        logits = jnp.where(valid, logits, NEG)
    alpha, p = _stream_update(logits, m_s, l_s, valid)
    for s, g0 in enumerate(v_rows):
        ab, pb = _bcast_row(alpha, s), _bcast_row(p, s)
        for i in range(8):
            r = s * 64 + 8 * i
            acc_s[r:r + 8, :] = acc_s[r:r + 8, :] * ab + x(g0 + i) * pb


def _merged_rows(acc_s, w, inv, s):
    wb = _bcast_row(w, s)
    return [jnp.sum(acc_s[s * 64 + 8 * i:s * 64 + 8 * i + 8, :] * wb, axis=1, keepdims=True) * inv[s:s + 1, :]
            for i in range(8)]


_DIFF_K_ROWS = tuple((16 * (s // 2) + 4 * (s % 2), 4, 128 * (s // 2) + 32 * (s % 2)) for s in range(8))
_DIFF_V_ROWS = tuple(16 * (s // 2) + 8 for s in range(8))


def _diff_sample_kernel(pt_ref, lam_ref, *refs, kpages, out_scale):
    pages = refs[:kpages]
    qc_ref, new_ref, g_ref, o_ref, qcb_s, m_s, l_s, acc_s = refs[kpages:]
    j = pl.program_id(1)
    lane = lax.broadcasted_iota(jnp.int32, (SUBLANES, LANES), 1)

    @pl.when(j == 0)
    def _():
        qcb_s[...] = jnp.broadcast_to(qc_ref[...], qcb_s.shape)
        m_s[...] = jnp.full(m_s.shape, NEG, F32)
        l_s[...] = jnp.zeros(l_s.shape, F32)
        acc_s[...] = jnp.zeros(acc_s.shape, F32)
        _kv_page_update(lambda i: jnp.broadcast_to(new_ref[8 * i:8 * i + 8, :], (SUBLANES, LANES)),
                        qcb_s, _DIFF_K_ROWS, _DIFF_V_ROWS, m_s, l_s, acc_s, lane == 0)

    for k in range(kpages):
        _kv_page_update(lambda i, ref=pages[k]: ref[8 * i:8 * i + 8, :],
                        qcb_s, _DIFF_K_ROWS, _DIFF_V_ROWS, m_s, l_s, acc_s, None)

    @pl.when(j == pl.num_programs(1) - 1)
    def _():
        w, inv = _lane_merge(m_s, l_s)
        lam = lam_ref[0]
        for h in range(DIFF_HEADS):
            o1 = _merged_rows(acc_s, w, inv, 2 * h)
            o2 = _merged_rows(acc_s, w, inv, 2 * h + 1)
            o = [a - lam * b for a, b in zip(o1, o2)]
            ss = o[0] * o[0]
            for i in range(1, 8):
                ss = ss + o[i] * o[i]
            r = lax.rsqrt(_sublane_allsum(ss) * (1.0 / HEAD_DIM) + RMS_EPS)
            for i in range(8):
                o_ref[h * 64 + 8 * i:h * 64 + 8 * i + 8, :] = o[i] * r * g_ref[8 * i:8 * i + 8, :] * out_scale


def _diff_sample_attn(cache_fm, page_table, layer, lam, qcol, newcol, gcol, out_scale, kpages):
    nb, n_pages = page_table.shape
    in_specs = [pl.BlockSpec((None, 512, LANES), lambda b, j, pt, lam, k=k: (pt[b, j * kpages + k], layer, 0))
                for k in range(kpages)]
    in_specs += [pl.BlockSpec((None, 512, 1), lambda b, j, pt, lam: (b, 0, 0)),
                 pl.BlockSpec((None, 512, 1), lambda b, j, pt, lam: (b, 0, 0)),
                 pl.BlockSpec((HEAD_DIM, 1), lambda b, j, pt, lam: (0, 0))]
    grid_spec = pltpu.PrefetchScalarGridSpec(
        num_scalar_prefetch=2,
        grid=(nb, n_pages // kpages),
        in_specs=in_specs,
        out_specs=pl.BlockSpec((None, 256, 1), lambda b, j, pt, lam: (b, 0, 0)),
        scratch_shapes=[pltpu.VMEM((512, LANES), F32), pltpu.VMEM((SUBLANES, LANES), F32),
                        pltpu.VMEM((SUBLANES, LANES), F32), pltpu.VMEM((512, LANES), F32)],
    )
    return pl.pallas_call(
        functools.partial(_diff_sample_kernel, kpages=kpages, out_scale=out_scale),
        grid_spec=grid_spec,
        out_shape=jax.ShapeDtypeStruct((nb, 256, 1), F32),
        compiler_params=_cparams(("parallel", "arbitrary")),
        name="diff_sample",
    )(page_table, lam, *([cache_fm] * kpages), qcol, newcol, gcol)


def _nsa_compress_sample_kernel(pt_ref, *refs, kpages, n_pages):
    pages = refs[:kpages]
    w2_ref, pe2_ref, qh_ref, g0_ref, oc_ref, sel_ref, flag_ref, buf_s = refs[kpages:]
    j = pl.program_id(1)
    scale = HEAD_DIM ** -0.5
    chunk = jnp.stack([pages[k][...] for k in range(kpages)], axis=0)
    buf_s[:, pl.ds(pl.multiple_of(j * kpages, kpages), kpages), :] = pltpu.einshape("prl->rpl", chunk)

    @pl.when(j == pl.num_programs(1) - 1)
    def _():
        kv = []
        for t in range(2):
            acc = jnp.zeros((2 * n_pages, LANES), F32)
            for d in range(HEAD_DIM):
                x = jnp.concatenate([buf_s[t * 128 + g * 64 + d] for g in range(NSA_KV_GROUPS)], axis=0)
                x = (x + pe2_ref[t, d:d + 1, :]).astype(BF16)
                acc = acc + jnp.dot(x, w2_ref[t, d], preferred_element_type=F32)
            kv.append(acc)
        any_sel = jnp.zeros((n_pages, 1), F32)
        for g in range(NSA_KV_GROUPS):
            kc = kv[0][g * n_pages:(g + 1) * n_pages, :].astype(BF16)
            vc = kv[1][g * n_pages:(g + 1) * n_pages, :].astype(BF16)
            lc = [jnp.dot(kc, qh_ref[g, hf].astype(BF16), preferred_element_type=F32) * scale for hf in range(2)]
            mx = jnp.maximum(jnp.max(lc[0], axis=0, keepdims=True), jnp.max(lc[1], axis=0, keepdims=True))
            pe = [jnp.exp(c - mx) for c in lc]
            den = jnp.maximum(jnp.sum(pe[0], axis=0, keepdims=True) + jnp.sum(pe[1], axis=0, keepdims=True), 1e-30)
            pc = [e / den for e in pe]
            res = [lax.dot_general(c.astype(BF16), vc, (((0,), (0,)), ((), ())), preferred_element_type=F32)
                   for c in pc]
            oc = res[0][:, 0:HEAD_DIM] + res[1][:, HEAD_DIM:2 * HEAD_DIM]
            oc_ref[g * NSA_HPG:(g + 1) * NSA_HPG, :] = oc * g0_ref[g * NSA_HPG:(g + 1) * NSA_HPG, :]
            imp = [jnp.sum(c, axis=1, keepdims=True) for c in pc]
            pidx = lax.broadcasted_iota(jnp.int32, (n_pages, 1), 0)
            score = [jnp.where(pidx == 0, FORCE_SCORE, imp[0]), jnp.where(pidx == n_pages - 1, FORCE_SCORE, imp[1])]
            rows = [_col_to_row(c) for c in score]
            prow = lax.broadcasted_iota(jnp.int32, (n_pages, n_pages), 1)
            pcol = lax.broadcasted_iota(jnp.int32, (n_pages, n_pages), 0)
            sel = []
            for hf in range(2):
                cnt = jnp.zeros((n_pages, 1), F32)
                for hf2 in range(2):
                    before = (2 * prow + hf2) < (2 * pcol + hf)
                    ahead = (rows[hf2] > score[hf]) | ((rows[hf2] == score[hf]) & before)
                    cnt = cnt + jnp.sum(jnp.where(ahead, 1.0, 0.0), axis=1, keepdims=True)
                cnt = cnt + jnp.where(score[hf] < FORCE_SCORE, 1.0, 0.0)
                sel.append(jnp.where(cnt < float(NSA_TOP_N), 1.0, 0.0))
            lane_p = lax.broadcasted_iota(jnp.int32, (n_pages, LANES), 1)
            sel_ref[g] = jnp.where(lane_p < NSA_BLOCK, sel[0], sel[1])
            any_sel = jnp.maximum(any_sel, jnp.maximum(sel[0], sel[1]))
        flag_ref[...] = any_sel


def _nsa_compress_sample(cache_fm, page_table, layer, w2, pe2, qh, g0, kpages):
    nb, n_pages = page_table.shape
    in_specs = [pl.BlockSpec((None, 256, LANES), lambda b, j, pt, k=k: (pt[b, j * kpages + k], 2 * layer, 0))
                for k in range(kpages)]
    in_specs += [pl.BlockSpec((2, HEAD_DIM, LANES, LANES), lambda b, j, pt: (0, 0, 0, 0)),
                 pl.BlockSpec((2, HEAD_DIM, LANES), lambda b, j, pt: (0, 0, 0)),
                 pl.BlockSpec((None, NSA_KV_GROUPS, 2, LANES, NSA_HPG), lambda b, j, pt: (b, 0, 0, 0, 0)),
                 pl.BlockSpec((None, NSA_HEADS, HEAD_DIM), lambda b, j, pt: (b, 0, 0))]
    grid_spec = pltpu.PrefetchScalarGridSpec(
        num_scalar_prefetch=1,
        grid=(nb, n_pages // kpages),
        in_specs=in_specs,
        out_specs=[pl.BlockSpec((None, NSA_HEADS, HEAD_DIM), lambda b, j, pt: (b, 0, 0)),
                   pl.BlockSpec((None, NSA_KV_GROUPS, n_pages, LANES), lambda b, j, pt: (b, 0, 0, 0)),
                   pl.BlockSpec((None, n_pages, 1), lambda b, j, pt: (b, 0, 0))],
        scratch_shapes=[pltpu.VMEM((256, n_pages, LANES), F32)],
    )
    return pl.pallas_call(
        functools.partial(_nsa_compress_sample_kernel, kpages=kpages, n_pages=n_pages),
        grid_spec=grid_spec,
        out_shape=[jax.ShapeDtypeStruct((nb, NSA_HEADS, HEAD_DIM), F32),
                   jax.ShapeDtypeStruct((nb, NSA_KV_GROUPS, n_pages, LANES), F32),
                   jax.ShapeDtypeStruct((nb, n_pages, 1), F32)],
        compiler_params=_cparams(("parallel", "arbitrary")),
        name="nsa_compress_sample",
    )(page_table, *([cache_fm] * kpages), w2, pe2, qh, g0)


_NSA_K_ROWS = tuple((8 * (s // NSA_HPG), 8, 64 * s) for s in range(NSA_HEADS))
_NSA_V_ROWS = tuple(16 + 8 * (s // NSA_HPG) for s in range(NSA_HEADS))


def _col_to_row(col):
    n = col.shape[0]
    eye = lax.broadcasted_iota(jnp.int32, (n, n), 0) == lax.broadcasted_iota(jnp.int32, (n, n), 1)
    return jnp.sum(jnp.where(eye, col, 0.0), axis=0, keepdims=True)


def _nsa_sample_kernel(eff_ref, flag_ref, *refs, kpages, n_pages):
    pages = refs[:kpages]
    (sel_ref, qc_ref, newsel_ref, newwin_ref, win_ref, g12_ref, osw_ref, wout_ref,
     qcb_s, m_s, l_s, acc_s, mw_s, lw_s, accw_s) = refs[kpages:]
    b = pl.program_id(0)
    j = pl.program_id(1)
    lane = lax.broadcasted_iota(jnp.int32, (SUBLANES, LANES), 1)

    @pl.when(j == 0)
    def _():
        qcb_s[...] = jnp.broadcast_to(qc_ref[...], qcb_s.shape)
        for ref in (m_s, mw_s):
            ref[...] = jnp.full(ref.shape, NEG, F32)
        for ref in (l_s, acc_s, lw_s, accw_s):
            ref[...] = jnp.zeros(ref.shape, F32)
        _kv_page_update(lambda i: jnp.broadcast_to(newsel_ref[8 * i:8 * i + 8, :], (SUBLANES, LANES)),
                        qcb_s, _NSA_K_ROWS, _NSA_V_ROWS, m_s, l_s, acc_s, lane == 0)
        _kv_page_update(lambda i: jnp.broadcast_to(newwin_ref[8 * i:8 * i + 8, :], (SUBLANES, LANES)),
                        qcb_s, _NSA_K_ROWS, _NSA_V_ROWS, mw_s, lw_s, accw_s, lane == 0)
        for c in range(NSA_WINDOW // LANES):
            valid = (lane >= 1) if c == 0 else None
            _kv_page_update(lambda i, c=c: win_ref[8 * i:8 * i + 8, c * LANES:(c + 1) * LANES],
                            qcb_s, _NSA_K_ROWS, _NSA_V_ROWS, mw_s, lw_s, accw_s, valid)
        lane_w = lax.broadcasted_iota(jnp.int32, (256, NSA_WINDOW), 1)
        shifted = pltpu.roll(win_ref[...], NSA_WINDOW - 1, 1)
        wout_ref[...] = jnp.where(lane_w == NSA_WINDOW - 1, newwin_ref[...], shifted)

    sub = lax.broadcasted_iota(jnp.int32, (SUBLANES, LANES), 0)
    for k in range(kpages):
        p_idx = j * kpages + k

        @pl.when(flag_ref[b, p_idx] != 0)
        def _(k=k, p_idx=p_idx):
            picked = jnp.where(sub < NSA_HPG, jnp.broadcast_to(sel_ref[0, pl.ds(p_idx, 1), :], (SUBLANES, LANES)),
                               jnp.broadcast_to(sel_ref[1, pl.ds(p_idx, 1), :], (SUBLANES, LANES)))
            _kv_page_update(lambda i, ref=pages[k]: ref[8 * i:8 * i + 8, :],
                            qcb_s, _NSA_K_ROWS, _NSA_V_ROWS, m_s, l_s, acc_s, picked > 0.5)

    @pl.when(j == pl.num_programs(1) - 1)
    def _():
        w, inv = _lane_merge(m_s, l_s)
        ww, invw = _lane_merge(mw_s, lw_s)
        for s in range(NSA_HEADS):
            o_sel = _merged_rows(acc_s, w, inv, s)
            o_win = _merged_rows(accw_s, ww, invw, s)
            for i in range(8):
                r = s * 64 + 8 * i
                osw_ref[r:r + 8, :] = o_sel[i] * g12_ref[0, r:r + 8, :] + o_win[i] * g12_ref[1, r:r + 8, :]


def _nsa_sample_attn(cache_fm, win_fm, page_table, layer, sel, flags, qcol, newsel, newwin, g12, kpages):
    nb, n_pages = page_table.shape
    n_chunks = n_pages // kpages
    step = jnp.arange(n_chunks, dtype=jnp.int32)[None, :, None]
    last = lax.cummax(jnp.where(flags.reshape(nb, n_chunks, kpages) != 0, step, -1), axis=1)
    slot = jnp.arange(kpages, dtype=jnp.int32)[None, None, :]
    eff_idx = (jnp.maximum(last, 0) * kpages + slot).reshape(nb, n_pages)
    eff = jnp.take_along_axis(page_table, eff_idx, axis=1).astype(jnp.int32)
    c3 = lambda b, j, eff, fl: (b, 0, 0)
    c4 = lambda b, j, eff, fl: (b, 0, 0, 0)
    in_specs = [pl.BlockSpec((None, 256, LANES),
                             lambda b, j, eff, fl, k=k: (eff[b, j * kpages + k], 2 * layer + 1, 0))
                for k in range(kpages)]
    in_specs += [
        pl.BlockSpec((None, NSA_KV_GROUPS, n_pages, LANES), c4),
        pl.BlockSpec((None, 512, 1), c3),
        pl.BlockSpec((None, 256, 1), c3),
        pl.BlockSpec((None, 256, 1), c3),
        pl.BlockSpec((None, None, 256, NSA_WINDOW), lambda b, j, eff, fl: (b, layer, 0, 0)),
        pl.BlockSpec((None, 2, 512, 1), c4),
    ]
    grid_spec = pltpu.PrefetchScalarGridSpec(
        num_scalar_prefetch=2,
        grid=(nb, n_chunks),
        in_specs=in_specs,
        out_specs=[pl.BlockSpec((None, 512, 1), c3),
                   pl.BlockSpec((None, 256, NSA_WINDOW), c3)],
        scratch_shapes=[
            pltpu.VMEM((512, LANES), F32),
            pltpu.VMEM((SUBLANES, LANES), F32), pltpu.VMEM((SUBLANES, LANES), F32), pltpu.VMEM((512, LANES), F32),
            pltpu.VMEM((SUBLANES, LANES), F32), pltpu.VMEM((SUBLANES, LANES), F32), pltpu.VMEM((512, LANES), F32),
        ],
    )
    return pl.pallas_call(
        functools.partial(_nsa_sample_kernel, kpages=kpages, n_pages=n_pages),
        grid_spec=grid_spec,
        out_shape=[jax.ShapeDtypeStruct((nb, 512, 1), F32),
                   jax.ShapeDtypeStruct((nb, 256, NSA_WINDOW), F32)],
        compiler_params=_cparams(("parallel", "arbitrary")),
        name="nsa_sample",
    )(eff, flags, *([cache_fm] * kpages), sel, qcol, newsel, newwin, win_fm, g12)


SampleProj = collections.namedtuple(
    "SampleProj", ["fq", "fox_rows", "nq", "ng", "nsa_rows", "win_rows", "dq", "diff_rows"])


def _fox_cache_view(c):
    return c.transpose(0, 1, 4, 3, 2).reshape(c.shape[0], 2 * 516, PAGE_SIZE)


def _kv_cache_view(c):
    return c.transpose(0, 1, 3, 4, 5, 2).reshape(c.shape[0], 2 * 512, PAGE_SIZE)


def _win_state_view(s):
    return s.transpose(0, 1, 3, 4, 5, 2).reshape(s.shape[0], DEPTH, 256, NSA_WINDOW)


def _fox_sample(cache_fm, page_table, layer, sp, kpages):
    nb = sp.fq.shape[0]
    r0 = 4 * layer
    qk = (sp.fq.reshape(nb, FOX_HEADS, HEAD_DIM) * (HEAD_DIM ** -0.5)).transpose(0, 2, 1).reshape(nb, 256)
    qcol = jnp.zeros((nb, FOX_WIN), F32).at[:, r0:r0 + 256].set(qk)
    rows = sp.fox_rows.reshape(nb, FOX_HEADS, 129).transpose(0, 2, 1).reshape(nb, 516)
    newcol = jnp.zeros((nb, FOX_WIN), F32).at[:, r0:r0 + 516].set(rows)
    o = _fox_sample_attn(cache_fm, page_table, layer, qcol[..., None], newcol[..., None], kpages)[..., 0]
    return o[:, r0:r0 + 256].reshape(nb, HEAD_DIM, FOX_HEADS).transpose(0, 2, 1).reshape(nb, 256)


def _diff_sample(cache_fm, page_table, layer, sp, lam, lam_init, g, kpages):
    nb = sp.dq.shape[0]
    q = sp.dq.reshape(nb, DIFF_HEADS, HEAD_DIM) * (DIFF_QK_DIM ** -0.5)
    qcol = jnp.concatenate([q, jnp.zeros_like(q)], axis=-1).reshape(nb, 512, 1)
    newcol = sp.diff_rows.reshape(nb, 512, 1)
    o = _diff_sample_attn(cache_fm, page_table, layer, lam.reshape(1).astype(F32), qcol, newcol,
                          g.reshape(HEAD_DIM, 1), 1.0 - lam_init, kpages)
    return o.reshape(nb, 256)


def _nsa_sample(cache_fm, win_fm, page_table, layer, sp, cmp_pos, cmp_w, kpages):
    nb = sp.nq.shape[0]
    w = cmp_w.transpose(0, 2, 1, 3)
    z = jnp.zeros_like(w)
    w2 = jnp.concatenate([jnp.concatenate([w, z], -1), jnp.concatenate([z, w], -1)], -2).astype(BF16)
    pe = cmp_pos.transpose(0, 2, 1)
    pe2 = jnp.concatenate([pe, pe], -1)
    q = sp.nq.reshape(nb, NSA_KV_GROUPS, NSA_HPG, HEAD_DIM)
    qt = q.transpose(0, 1, 3, 2)
    zq = jnp.zeros_like(qt)
    qh = jnp.stack([jnp.concatenate([qt, zq], 2), jnp.concatenate([zq, qt], 2)], 2)
    gates = sp.ng.reshape(nb, 3, NSA_HEADS)
    g0 = jnp.broadcast_to(gates[:, 0, :, None], (nb, NSA_HEADS, HEAD_DIM))
    g12 = jnp.broadcast_to(gates[:, 1:, :, None], (nb, 2, NSA_HEADS, HEAD_DIM)).reshape(nb, 2, 512, 1)
    oc, sel, flags = _nsa_compress_sample(cache_fm, page_table, layer, w2, pe2, qh, g0, kpages)
    qcol = (sp.nq.reshape(nb, 512) * (HEAD_DIM ** -0.5))[..., None]
    newsel = sp.nsa_rows.reshape(nb, 512)[:, 256:, None]
    newwin = sp.win_rows.reshape(nb, 256, 1)
    osw, wnew = _nsa_sample_attn(cache_fm, win_fm, page_table, layer, sel, (flags[..., 0] > 0.5).astype(jnp.int32),
                                 qcol, newsel, newwin, g12, kpages)
    win_new = wnew.reshape(nb, 2, NSA_KV_GROUPS, HEAD_DIM, NSA_WINDOW).transpose(0, 4, 1, 2, 3)
    return oc.reshape(nb, 512) + osw.reshape(nb, 512), win_new


def _diff_lambda(lp, layer):
    lam_init = 0.8 - 0.6 * math.exp(-0.3 * layer)
    lp = lp.astype(F32)
    lam = jnp.exp(jnp.sum(lp[0] * lp[1])) - jnp.exp(jnp.sum(lp[2] * lp[3])) + lam_init
    return lam, lam_init


def _pad_rows(a, n):
    return jnp.concatenate([a, jnp.zeros((n - a.shape[0],) + a.shape[1:], a.dtype)], 0)


def kernel(x_prompt, x_sample, cache_fox, cache_nsa, cache_diff, state_nsa_win, page_table,
           w_in, b_forget, b_nsa_gate, nsa_cmp_pos, nsa_cmp_w, diff_lambda, diff_subln_g,
           w_out, ln_attn_g, ln_attn_b, w_router, b_router, w_up, b_up, w_down, b_down,
           ln_ffn_g, ln_ffn_b):
    nb, seq, _ = x_prompt.shape
    n_dec, n_new, _ = x_sample.shape
    past = page_table.shape[1] * cache_fox.shape[2]
    tabs_p = _rope_tables(jnp.arange(seq, dtype=F32))
    n_dec_pad = LANES
    tabs_s = _rope_tables(jnp.full((n_dec_pad,), past, F32))
    perm = jnp.asarray(np.maximum(_PERM, 0), jnp.int32)
    valid = jnp.asarray((_PERM >= 0).astype(np.float32))[:, None]

    assert n_new == 1 and past % NSA_BLOCK == 0 and state_nsa_win.shape[2] == NSA_WINDOW
    fox_fm = _fox_cache_view(cache_fox)
    nsa_fm = _kv_cache_view(cache_nsa)
    diff_fm = _kv_cache_view(cache_diff)
    win_fm = _win_state_view(state_nsa_win)

    hp = x_prompt
    hs = x_sample.reshape(n_dec, D_MODEL)
    m_p = nb * seq
    m_all = m_p + n_dec
    m_pad = -(-m_all // 512) * 512
    outs = {k: [] for k in ('fox_p', 'fox_s', 'nsa_p', 'nsa_s', 'diff_p', 'diff_s', 'win_p', 'win_s')}
    for l in range(DEPTH):
        lam, lam_init = _diff_lambda(diff_lambda[l], l)
        lam_arr = lam.reshape(1).astype(F32)
        wt = (jnp.take(w_in[l], perm, axis=1).T * valid).astype(BF16)
        bgate_col = jnp.pad(b_nsa_gate[l], (0, 8)).reshape(32, 1)
        bforget_col = jnp.pad(b_forget[l], (0, 4)).reshape(8, 1)
        w_out_b = w_out[l].astype(BF16)
        wr = jnp.pad(w_router[l], ((0, 0), (0, LANES - N_EXPERTS))).astype(BF16)
        br = jnp.pad(b_router[l], (0, LANES - N_EXPERTS)).reshape(1, LANES)
        moe_w = (_deinterleave(w_up, 512, layer=l), b_up[l][:, None, 0::2], b_up[l][:, None, 1::2],
                 w_down, b_down[l][:, None, :])
        g1, b1 = ln_attn_g[l].reshape(1, -1), ln_attn_b[l].reshape(1, -1)
        g2, b2 = ln_ffn_g[l].reshape(1, -1), ln_ffn_b[l].reshape(1, -1)

        pt = _inproj(hp, wt, bgate_col, bforget_col, tabs_p, 256)
        outs['fox_p'].append(pt[:, R_FOX:R_FOX + 516].reshape(nb, 129, FOX_HEADS, seq).transpose(0, 3, 2, 1))
        outs['nsa_p'].append(pt[:, R_NSA:R_NSA + 512].reshape(nb, 4, NSA_KV_GROUPS, HEAD_DIM, seq)
                             .transpose(0, 4, 1, 2, 3))
        outs['diff_p'].append(pt[:, R_DIFF:R_DIFF + 512].reshape(nb, DIFF_HEADS, 2, HEAD_DIM, seq)
                              .transpose(0, 4, 1, 2, 3))
        outs['win_p'].append(pt[:, R_WIN:R_WIN + 256, seq - NSA_WINDOW:]
                             .reshape(nb, 2, NSA_KV_GROUPS, HEAD_DIM, NSA_WINDOW).transpose(0, 4, 1, 2, 3))
        c = jnp.cumsum(pt[:, R_LOGF:R_LOGF + FOX_HEADS, :], axis=-1)
        o_fox = _fox_attention_t(pt, c, 512)
        o_diff = _diff_attention_t(pt, lam_arr, diff_subln_g[l].reshape(HEAD_DIM, 1), 1.0 - lam_init, 512)
        n_cmp = seq // NSA_BLOCK
        kcv = pt[:, R_NSA:R_NSA + 256].reshape(nb, 2, NSA_KV_GROUPS, HEAD_DIM, n_cmp, NSA_BLOCK)
        xb = kcv.transpose(1, 0, 2, 4, 5, 3).reshape(2, nb * NSA_KV_GROUPS * n_cmp, NSA_BLOCK * HEAD_DIM)
        cmp = _compress(xb, nsa_cmp_pos[l].reshape(2, 1, NSA_BLOCK * HEAD_DIM),
                        nsa_cmp_w[l].reshape(2, NSA_BLOCK * HEAD_DIM, HEAD_DIM), 128)
        cmp = cmp.reshape(2, nb, NSA_KV_GROUPS, n_cmp, HEAD_DIM)
        o_cmp, sel = _nsa_cmp_t(pt, cmp[0], cmp[1].transpose(0, 1, 3, 2), 256)
        o_sel = _nsa_band_t(pt, sel, 256, NSA_SEL_TK, True)
        o_win = _nsa_band_t(pt, None, 256, 256, False)

        xs_pad = _pad_rows(hs, n_dec_pad)[None]
        pt_s = _inproj(xs_pad, wt, bgate_col, bforget_col, tabs_s, n_dec_pad)
        ps = pt_s[0, :, :n_dec].T
        sp = SampleProj(
            fq=ps[:, R_FQ:R_FQ + 256].reshape(n_dec, 1, FOX_HEADS, HEAD_DIM),
            fox_rows=ps[:, R_FOX:R_FOX + 516].reshape(n_dec, 1, 129, FOX_HEADS).transpose(0, 1, 3, 2),
            nq=ps[:, R_NQ:R_NQ + 512].reshape(n_dec, 1, NSA_HEADS, HEAD_DIM),
            ng=ps[:, R_GATE:R_GATE + 24].reshape(n_dec, 1, 3, NSA_HEADS),
            nsa_rows=ps[:, R_NSA:R_NSA + 512].reshape(n_dec, 1, 4, NSA_KV_GROUPS, HEAD_DIM),
            win_rows=ps[:, R_WIN:R_WIN + 256].reshape(n_dec, 1, 2, NSA_KV_GROUPS, HEAD_DIM),
            dq=ps[:, R_DQ:R_DQ + 256].reshape(n_dec, 1, DIFF_HEADS, 2, DIFF_QK_DIM),
            diff_rows=ps[:, R_DIFF:R_DIFF + 512].reshape(n_dec, 1, DIFF_HEADS, 2, HEAD_DIM))
        s_fox = _fox_sample(fox_fm, page_table, l, sp, SAMPLE_PAGES_PER_STEP)
        s_nsa, win_new = _nsa_sample(nsa_fm, win_fm, page_table, l, sp, nsa_cmp_pos[l], nsa_cmp_w[l],
                                     SAMPLE_PAGES_PER_STEP)
        s_diff = _diff_sample(diff_fm, page_table, l, sp, lam, lam_init, diff_subln_g[l], SAMPLE_PAGES_PER_STEP)
        outs['fox_s'].append(sp.fox_rows)
        outs['nsa_s'].append(sp.nsa_rows)
        outs['diff_s'].append(sp.diff_rows)
        outs['win_s'].append(win_new)

        def rows(a_p, a_s):
            return _pad_rows(jnp.concatenate([a_p.reshape(m_p, -1), a_s], 0), m_pad)

        zero_s = jnp.zeros((n_dec, 512), F32)
        h_all = rows(hp, hs)
        h1, logits = _outproj(rows(o_fox, s_fox), rows(o_cmp, s_nsa), rows(o_sel, zero_s), rows(o_win, zero_s),
                              rows(o_diff, s_diff), h_all, w_out_b, g1, b1, wr, br, 512)
        h2 = _moe_block(h1, logits[:, :N_EXPERTS], moe_w, g2, b2, 512, 512, wd_layer=l)
        hp = h2[:m_p].reshape(nb, seq, D_MODEL)
        hs = h2[m_p:m_all]

    st = lambda k: jnp.stack(outs[k], axis=1)
    return (hp, hs.reshape(n_dec, n_new, D_MODEL),
            st('fox_p'), st('fox_s'), st('nsa_p'), st('nsa_s'),
            st('diff_p'), st('diff_s'), st('win_p'), st('win_s'))
```
